```python
import math
import jax, jax.numpy as jnp
from jax import lax
import numpy as np

D_MODEL = 1024
BATCH = 8
SEQ = 4096
DEPTH = 2

HEAD_DIM = 64
BRANCH_WIDTH = D_MODEL // 2
DIFF_HEADS = BRANCH_WIDTH // (2 * HEAD_DIM)
DIFF_VDIM = 2 * HEAD_DIM
FOX_HEADS = BRANCH_WIDTH // HEAD_DIM
DIL_HEADS = BRANCH_WIDTH // HEAD_DIM
DIL_PATTERNS = ((128, 1), (512, 4), (2048, 16))
N_BRANCHES = 3
Q_BLOCK = 128
RMS_EPS = 1e-6
ALIBI_MAX_EXP = 8.0

kernel_name = "hybrid_diff_fox_dilated_gated_block"

_SPLITS = (
    ("diff_q", 2 * DIFF_HEADS * HEAD_DIM), ("diff_k", 2 * DIFF_HEADS * HEAD_DIM),
    ("diff_v", DIFF_HEADS * DIFF_VDIM), ("diff_z", BRANCH_WIDTH),
    ("fox_q", FOX_HEADS * HEAD_DIM), ("fox_k", FOX_HEADS * HEAD_DIM),
    ("fox_v", FOX_HEADS * HEAD_DIM), ("fox_f", FOX_HEADS), ("fox_z", BRANCH_WIDTH),
    ("dil_q", DIL_HEADS * HEAD_DIM), ("dil_k", DIL_HEADS * HEAD_DIM),
    ("dil_v", DIL_HEADS * HEAD_DIM), ("dil_z", BRANCH_WIDTH),
    ("merge_g", N_BRANCHES * D_MODEL),
)
_NAMES = [n for n, _ in _SPLITS]
_OFFSETS = [int(o) for o in np.cumsum([w for _, w in _SPLITS])[:-1]]
IN_WIDTH = int(sum(w for _, w in _SPLITS))


def rmsnorm(x, g):
    xf = x.astype(jnp.float32)
    y = xf * lax.rsqrt(jnp.mean(xf * xf, axis=-1, keepdims=True) + RMS_EPS)
    return (y * g.astype(jnp.float32)).astype(x.dtype)


def alibi_slopes(n_heads):
    return jnp.asarray(np.power(2.0, -ALIBI_MAX_EXP * np.arange(1, n_heads + 1) / n_heads), dtype=jnp.float32)


def causal_sweep(block_fn, seq_len):
    return jnp.concatenate([block_fn(s, s + Q_BLOCK) for s in range(0, seq_len, Q_BLOCK)], axis=1)


def diff_attention(q, k, v, lam, slopes):
    seq_len = q.shape[1]
    scale = HEAD_DIM ** -0.5

    def block(start, end):
        s = jnp.einsum('bqhcd,bkhcd->bhcqk', q[:, start:end], k[:, :end]).astype(jnp.float32) * scale
        dist = (jnp.arange(start, end)[:, None] - jnp.arange(end)[None, :]).astype(jnp.float32)
        s = s - slopes[:, None, None, None] * dist
        s = jnp.where(dist >= 0, s, -jnp.inf)
        p = jax.nn.softmax(s, axis=-1)
        pd = p[:, :, 0] - lam * p[:, :, 1]
        return jnp.einsum('bhqk,bkhe->bqhe', pd, v[:, :end].astype(jnp.float32))

    return causal_sweep(block, seq_len)


def forgetting_attention(q, k, v, cum_logf):
    seq_len = q.shape[1]
    scale = HEAD_DIM ** -0.5
    c = cum_logf.transpose(0, 2, 1)

    def block(start, end):
        s = jnp.einsum('bqhd,bkhd->bhqk', q[:, start:end], k[:, :end]).astype(jnp.float32) * scale
        s = s + c[:, :, start:end, None] - c[:, :, None, :end]
        causal = jnp.arange(start, end)[:, None] >= jnp.arange(end)[None, :]
        s = jnp.where(causal, s, -jnp.inf)
        p = jax.nn.softmax(s, axis=-1)
        return jnp.einsum('bhqk,bkhd->bqhd', p, v[:, :end].astype(jnp.float32))

    return causal_sweep(block, seq_len)


def dilated_pattern(q, k, v, window, dilation, slopes):
    B, S, H, E = q.shape
    n = S // dilation
    L = window // dilation
    n_pad = -(-n // L) * L
    nb = n_pad // L
    scale = HEAD_DIM ** -0.5

    def to_sub(a):
        a = a.reshape(B, n, dilation, H, E).transpose(0, 2, 3, 1, 4)
        a = jnp.pad(a, ((0, 0), (0, 0), (0, 0), (0, n_pad - n), (0, 0)))
        return a.reshape(B, dilation, H, nb, L, E)

    def with_prev(a):
        prev = jnp.pad(a[:, :, :, :-1], ((0, 0), (0, 0), (0, 0), (1, 0), (0, 0), (0, 0)))
        return jnp.concatenate([prev, a], axis=4)

    qs = to_sub(q)
    kk = with_prev(to_sub(k))
    vv = with_prev(to_sub(v))
    s = jnp.einsum('brhnqe,brhnke->brhnqk', qs, kk).astype(jnp.float32) * scale
    i = jnp.arange(L)[:, None]
    j = jnp.arange(2 * L)[None, :]
    delta = L + i - j
    blk = jnp.arange(nb)[:, None, None]
    valid = (delta >= 0) & (delta <= L) & (blk * L + j - L >= 0)
    s = s - slopes[:, None, None, None] * (dilation * delta).astype(jnp.float32)
    s = jnp.where(valid, s, -jnp.inf)
    m = jnp.max(s, axis=-1, keepdims=True)
    e = jnp.exp(s - m)
    den = jnp.sum(e, axis=-1, keepdims=True)
    num = jnp.einsum('brhnqk,brhnke->brhnqe', e, vv.astype(jnp.float32))

    def from_sub(a):
        F = a.shape[-1]
        a = a.reshape(B, dilation, H, n_pad, F)[:, :, :, :n]
        return a.transpose(0, 3, 1, 2, 4).reshape(B, S, H, F)

    return from_sub(num), from_sub(m)[..., 0], from_sub(den)[..., 0]


def dilated_attention(q, k, v, slopes):
    res = [dilated_pattern(q, k, v, w, d, slopes) for w, d in DIL_PATTERNS]
    mx = jnp.max(jnp.stack([r[1] for r in res], axis=0), axis=0)
    num = sum(jnp.exp(r[1] - mx)[..., None] * r[0] for r in res)
    den = sum(jnp.exp(r[1] - mx) * r[2] for r in res)
    return num / den[..., None]


def setup_inputs(seed: int = 0) -> dict:
    key = jax.random.key(seed)
    ks = jax.random.split(key, 10)
    f32 = jnp.float32
    x = jax.random.normal(ks[0], (BATCH, SEQ, D_MODEL), f32)
    norm_g = 1.0 + 0.05 * jax.random.normal(ks[1], (DEPTH, D_MODEL), f32)
    w_in = jax.random.normal(ks[2], (DEPTH, D_MODEL, IN_WIDTH), f32) * D_MODEL ** -0.5
    fox_fb = jax.random.uniform(ks[3], (DEPTH, FOX_HEADS), f32, minval=1.0, maxval=4.0)
    diff_lam = 0.1 * jax.random.normal(ks[4], (DEPTH, 4, HEAD_DIM), f32)
    diff_norm_g = 1.0 + 0.05 * jax.random.normal(ks[5], (DEPTH, DIFF_HEADS * DIFF_VDIM), f32)
    w_branch = jax.random.normal(ks[6], (DEPTH, N_BRANCHES, BRANCH_WIDTH, D_MODEL), f32) * BRANCH_WIDTH ** -0.5
    w_out = jax.random.normal(ks[7], (DEPTH, D_MODEL, D_MODEL), f32) * D_MODEL ** -0.5
    final_g = 1.0 + 0.05 * jax.random.normal(ks[8], (D_MODEL,), f32)
    return {"x": x, "norm_g": norm_g, "w_in": w_in, "fox_fb": fox_fb, "diff_lam": diff_lam,
            "diff_norm_g": diff_norm_g, "w_branch": w_branch, "w_out": w_out, "final_g": final_g}


def reference(x, norm_g, w_in, fox_fb, diff_lam, diff_norm_g, w_branch, w_out, final_g):
    B, S, _ = x.shape
    diff_slopes = alibi_slopes(DIFF_HEADS)
    dil_slopes = alibi_slopes(DIL_HEADS)
    for l in range(DEPTH):
        h = rmsnorm(x, norm_g[l])
        proj = h @ w_in[l]
        p = dict(zip(_NAMES, jnp.split(proj, _OFFSETS, axis=-1)))

        lam_init = 0.8 - 0.6 * math.exp(-0.3 * l)
        dl = diff_lam[l].astype(jnp.float32)
        lam = jnp.exp(jnp.sum(dl[0] * dl[1])) - jnp.exp(jnp.sum(dl[2] * dl[3])) + lam_init
        qa = p["diff_q"].reshape(B, S, DIFF_HEADS, 2, HEAD_DIM)
        ka = p["diff_k"].reshape(B, S, DIFF_HEADS, 2, HEAD_DIM)
        va = p["diff_v"].reshape(B, S, DIFF_HEADS, DIFF_VDIM)
        oa = diff_attention(qa, ka, va, lam, diff_slopes)
        oa = rmsnorm(oa, diff_norm_g[l].reshape(DIFF_HEADS, DIFF_VDIM)) * (1.0 - lam_init)
        ya = oa.reshape(B, S, BRANCH_WIDTH).astype(x.dtype) * jax.nn.silu(p["diff_z"])

        logf = jax.nn.log_sigmoid(p["fox_f"].astype(jnp.float32) + fox_fb[l].astype(jnp.float32))
        cum_logf = jnp.cumsum(logf, axis=1)
        qb = p["fox_q"].reshape(B, S, FOX_HEADS, HEAD_DIM)
        kb = p["fox_k"].reshape(B, S, FOX_HEADS, HEAD_DIM)
        vb = p["fox_v"].reshape(B, S, FOX_HEADS, HEAD_DIM)
        ob = forgetting_attention(qb, kb, vb, cum_logf)
        yb = ob.reshape(B, S, BRANCH_WIDTH).astype(x.dtype) * jax.nn.silu(p["fox_z"])

        qc = p["dil_q"].reshape(B, S, DIL_HEADS, HEAD_DIM)
        kc = p["dil_k"].reshape(B, S, DIL_HEADS, HEAD_DIM)
        vc = p["dil_v"].reshape(B, S, DIL_HEADS, HEAD_DIM)
        oc = dilated_attention(qc, kc, vc, dil_slopes)
        yc = oc.reshape(B, S, BRANCH_WIDTH).astype(x.dtype) * jax.nn.silu(p["dil_z"])

        y = jnp.stack([ya, yb, yc], axis=2)
        gates = jax.nn.sigmoid(p["merge_g"]).reshape(B, S, N_BRANCHES, D_MODEL)
        branch = jnp.einsum('bsne,nef->bsnf', y, w_branch[l])
        merged = jnp.sum(gates * branch, axis=2)
        x = x + merged @ w_out[l]
    return rmsnorm(x, final_g)
```

```python
import functools
import math

import jax
import jax.numpy as jnp
import numpy as np
from jax import lax
from jax.experimental import pallas as pl
from jax.experimental.pallas import tpu as pltpu

F32 = jnp.float32
BF16 = jnp.bfloat16

D_MODEL = 1024
HEAD_DIM = 64
BRANCH_WIDTH = 512
LANES = 128
N_PAIRS = BRANCH_WIDTH // LANES
DIFF_HEADS = 4
N_HEADS = 8
DIL_PATTERNS = ((128, 1), (512, 4), (2048, 16))
DIL_BLOCK = 128
RMS_EPS = 1e-6
ALIBI_MAX_EXP = 8.0
N_QKV_GROUPS = 9
TQ = 256
TK = 256
TM_PROJ = 512
TM_MERGE = 512
VMEM_LIMIT = 56 * 1024 * 1024


def _split3(a):
    a1 = a.astype(BF16)
    r1 = a - a1.astype(F32)
    a2 = r1.astype(BF16)
    a3 = (r1 - a2.astype(F32)).astype(BF16)
    return a1, a2, a3


def _proj_kernel(x_ref, g_ref, w_ref, wf_ref, fb_ref, pmat_ref,
                 dqT_ref, dk_ref, dvT_ref, fqT_ref, fk_ref, fvT_ref, fkb_ref,
                 cq_ref, ck_ref, cv_ref, carry_ref, *, tiles_per_seq):
    i = pl.program_id(0)
    tm = x_ref.shape[0]
    x = x_ref[...]
    ms = jnp.mean(x * x, axis=-1, keepdims=True)
    h = (x * lax.rsqrt(ms + RMS_EPS) * g_ref[...]).astype(BF16)

    def proj(g):
        return jnp.dot(h, w_ref[:, g * BRANCH_WIDTH:(g + 1) * BRANCH_WIDTH],
                       preferred_element_type=F32)

    def put_t(ref, val):
        for t in range(tm // TQ):
            ref[t] = val[t * TQ:(t + 1) * TQ, :].T.astype(BF16)

    put_t(dqT_ref, proj(0))
    dk_ref[...] = proj(1).astype(BF16)
    put_t(dvT_ref, proj(2))
    put_t(fqT_ref, proj(3))
    fk_ref[...] = proj(4).astype(BF16)
    put_t(fvT_ref, proj(5))
    cq_ref[...] = proj(6).astype(BF16)
    ck_ref[...] = proj(7).astype(BF16)
    cv_ref[...] = proj(8).astype(BF16)

    f = jnp.dot(h, wf_ref[...], preferred_element_type=F32) + fb_ref[...]
    logf = jnp.minimum(f, 0.0) - jnp.log1p(jnp.exp(-jnp.abs(f)))

    @pl.when(i % tiles_per_seq == 0)
    def _():
        carry_ref[...] = jnp.zeros_like(carry_ref)

    row = lax.broadcasted_iota(jnp.int32, (tm, tm), 0)
    col = lax.broadcasted_iota(jnp.int32, (tm, tm), 1)
    tri = jnp.where(row >= col, 1.0, 0.0).astype(BF16)
    a1, a2, a3 = _split3(logf)
    c = (jnp.dot(tri, a1, preferred_element_type=F32)
         + jnp.dot(tri, a2, preferred_element_type=F32)
         + jnp.dot(tri, a3, preferred_element_type=F32)) + carry_ref[0:1, :]
    carry_ref[0:1, :] = c[tm - 1:tm, :]
    b1, b2, b3 = _split3(-c)
    kb = jnp.dot(jnp.concatenate([b1, b2, b3], axis=1), pmat_ref[...],
                 preferred_element_type=F32)
    fkb_ref[...] = kb.astype(BF16)


def _proj_call(x2d, g, w_qkv, w_f, fb, pmat, *, seq_len):
    n_tok = x2d.shape[0]
    tm = TM_PROJ
    nt = n_tok // tm
    const = lambda i: (0, 0)
    tok = pl.BlockSpec((tm, BRANCH_WIDTH), lambda i: (i, 0))
    tr = pl.BlockSpec((tm // TQ, BRANCH_WIDTH, TQ), lambda i: (i, 0, 0))
    nat_shape = jax.ShapeDtypeStruct((n_tok, BRANCH_WIDTH), BF16)
    tr_shape = jax.ShapeDtypeStruct((n_tok // TQ, BRANCH_WIDTH, TQ), BF16)
    return pl.pallas_call(
        functools.partial(_proj_kernel, tiles_per_seq=seq_len // tm),
        grid=(nt,),
        in_specs=[
            pl.BlockSpec((tm, D_MODEL), lambda i: (i, 0)),
            pl.BlockSpec((1, D_MODEL), const),
            pl.BlockSpec((D_MODEL, N_QKV_GROUPS * BRANCH_WIDTH), const),
            pl.BlockSpec((D_MODEL, LANES), const),
            pl.BlockSpec((1, LANES), const),
            pl.BlockSpec((3 * LANES, BRANCH_WIDTH), const),
        ],
        out_specs=[tr, tok, tr, tr, tok, tr, tok, tok, tok, tok],
        out_shape=[tr_shape, nat_shape, tr_shape, tr_shape, nat_shape, tr_shape,
                   nat_shape, nat_shape, nat_shape, nat_shape],
        scratch_shapes=[pltpu.VMEM((8, LANES), F32)],
        compiler_params=pltpu.CompilerParams(
            dimension_semantics=("arbitrary",), vmem_limit_bytes=VMEM_LIMIT),
        name="proj",
    )(x2d, g, w_qkv, w_f, fb, pmat)


def _pair_attn_kernel(qT_ref, k_ref, kb_ref, vT_ref, lam_ref, g_ref, o_ref,
                      qa_sc, m_sc, l_sc, acc_sc, *, diff, lam_init):
    i = pl.program_id(2)
    qT = qT_ref[0, 0].astype(F32)
    row = lax.broadcasted_iota(jnp.int32, (LANES, TQ), 0)
    qa_sc[0] = jnp.concatenate(
        [jnp.where(row < HEAD_DIM, qT, 0.0), jnp.where(row < 3, 1.0, 0.0)],
        axis=0).astype(BF16)
    qa_sc[1] = jnp.concatenate(
        [jnp.where(row >= HEAD_DIM, qT, 0.0),
         jnp.where((row >= 3) & (row < 6), 1.0, 0.0)], axis=0).astype(BF16)
    m_sc[...] = jnp.full(m_sc.shape, -jnp.inf, F32)
    l_sc[...] = jnp.zeros(l_sc.shape, F32)
    acc_sc[...] = jnp.zeros(acc_sc.shape, F32)

    def tile(j, masked):
        start = pl.multiple_of(j * TK, TK)
        ka = jnp.concatenate([k_ref[0, pl.ds(start, TK), :],
                              kb_ref[0, pl.ds(start, TK), :]], axis=1)
        vT = vT_ref[0, j]
        for c in range(2):
            s = jnp.dot(ka, qa_sc[c], preferred_element_type=F32)
            if masked:
                kr = lax.broadcasted_iota(jnp.int32, (TK, TQ), 0)
                qc = lax.broadcasted_iota(jnp.int32, (TK, TQ), 1)
                s = jnp.where(kr <= qc, s, -jnp.inf)
            m_old = m_sc[c]
            m_new = jnp.maximum(m_old, jnp.max(s, axis=0, keepdims=True))
            alpha = jnp.exp(m_old - m_new)
            p = jnp.exp(s - m_new)
            l_sc[c] = alpha * l_sc[c] + jnp.sum(p, axis=0, keepdims=True)
            acc_sc[c] = alpha * acc_sc[c] + jnp.dot(
                vT, p.astype(BF16), preferred_element_type=F32)
            m_sc[c] = m_new

    def body(j, carry):
        tile(j, False)
        return carry

    lax.fori_loop(0, i, body, 0)
    tile(i, True)

    a0 = acc_sc[0] / l_sc[0]
    a1 = acc_sc[1] / l_sc[1]
    if diff:
        dl = lam_ref[...]
        lam = (jnp.exp(jnp.sum(dl[0:1] * dl[1:2], axis=1, keepdims=True))
               - jnp.exp(jnp.sum(dl[2:3] * dl[3:4], axis=1, keepdims=True)) + lam_init)
        oT = a0 - lam * a1
        ms = jnp.mean(oT * oT, axis=0, keepdims=True)
        oT = oT * lax.rsqrt(ms + RMS_EPS)
        o_ref[0] = (oT.T * g_ref[...]) * (1.0 - lam_init)
    else:
        o_ref[0] = jnp.where(row < HEAD_DIM, a0, a1).T


def _pair_attn_call(qT, k, kb, vT, lam_pad, g, *, batch, seq_len, diff, lam_init):
    nq = seq_len // TQ
    nk = seq_len // TK
    qT = qT.reshape(batch, nq, BRANCH_WIDTH, TQ)
    vT = vT.reshape(batch, nk, BRANCH_WIDTH, TK)
    k = k.reshape(batch, seq_len, BRANCH_WIDTH)
    kb_batched = kb.shape[0] != 1
    kb_map = (lambda b, p, i: (b, 0, p)) if kb_batched else (lambda b, p, i: (0, 0, p))
    return pl.pallas_call(
        functools.partial(_pair_attn_kernel, diff=diff, lam_init=lam_init),
        grid=(batch, N_PAIRS, nq),
        in_specs=[
            pl.BlockSpec((1, 1, LANES, TQ), lambda b, p, i: (b, i, p, 0)),
            pl.BlockSpec((1, seq_len, LANES), lambda b, p, i: (b, 0, p)),
            pl.BlockSpec((1, seq_len, LANES), kb_map),
            pl.BlockSpec((1, nk, LANES, TK), lambda b, p, i: (b, 0, p, 0)),
            pl.BlockSpec((4, LANES), lambda b, p, i: (0, 0)),
            pl.BlockSpec((1, LANES), lambda b, p, i: (0, p)),
        ],
        out_specs=pl.BlockSpec((1, TQ, LANES), lambda b, p, i: (b, i, p)),
        out_shape=jax.ShapeDtypeStruct((batch, seq_len, BRANCH_WIDTH), F32),
        scratch_shapes=[
            pltpu.VMEM((2, 2 * LANES, TQ), BF16),
            pltpu.VMEM((2, 1, TQ), F32),
            pltpu.VMEM((2, 1, TQ), F32),
            pltpu.VMEM((2, LANES, TQ), F32),
        ],
        compiler_params=pltpu.CompilerParams(
            dimension_semantics=("arbitrary", "arbitrary", "arbitrary"),
            vmem_limit_bytes=VMEM_LIMIT),
        name="diff_attn" if diff else "fox_attn",
    )(qT, k, kb, vT, lam_pad, g)


def _dil_kernel(q_ref, k_ref, v_ref, kb_ref, o_ref,
                qf, kf, vf, kbf, acc, m_sc, l_sc, *, seq_len):
    L = DIL_BLOCK
    qf[...] = q_ref[0].astype(F32)
    kf[...] = k_ref[0].astype(F32)
    vf[...] = v_ref[0].astype(F32)
    kbf[...] = kb_ref[0].astype(F32)
    acc[...] = jnp.zeros(acc.shape, F32)
    m_sc[...] = jnp.full(m_sc.shape, -jnp.inf, F32)
    l_sc[...] = jnp.zeros(l_sc.shape, F32)

    lane = lax.broadcasted_iota(jnp.int32, (L, LANES), 1)
    first = lane < HEAD_DIM
    sel = [jnp.where(lane < 3, 1.0, 0.0), jnp.where((lane >= 3) & (lane < 6), 1.0, 0.0)]
    qi = lax.broadcasted_iota(jnp.int32, (L, 2 * L), 0)
    kj = lax.broadcasted_iota(jnp.int32, (L, 2 * L), 1)
    band = (kj >= qi) & (kj <= qi + L)

    for _, d in DIL_PATTERNS:
        nb = seq_len // (L * d)

        def block(t, carry, d=d, nb=nb):
            r = t // nb
            jb = t % nb
            start = r + d * L * jb
            prev = jnp.where(jb == 0, start, start - d * L)
            rows = pl.ds(start, L, stride=d)
            prows = pl.ds(prev, L, stride=d)
            qb = qf[rows, :]
            ka = jnp.concatenate(
                [jnp.concatenate([kf[prows, :], kbf[prows, :]], axis=1),
                 jnp.concatenate([kf[rows, :], kbf[rows, :]], axis=1)], axis=0).astype(BF16)
            vv = jnp.concatenate([vf[prows, :], vf[rows, :]], axis=0).astype(BF16)
            valid = band & ((kj >= L) | (jb > 0))
            s = []
            for hh in range(2):
                qh = jnp.where(first if hh == 0 else ~first, qb, 0.0)
                qa = jnp.concatenate([qh, sel[hh]], axis=1).astype(BF16)
                sh = lax.dot_general(qa, ka, (((1,), (1,)), ((), ())),
                                     preferred_element_type=F32)
                s.append(jnp.where(valid, sh, -jnp.inf))
            mx = jnp.where(first, jnp.max(s[0], axis=1, keepdims=True),
                           jnp.max(s[1], axis=1, keepdims=True))
            m_old = m_sc[rows, :]
            m_new = jnp.maximum(m_old, mx)
            alpha = jnp.exp(m_old - m_new)
            mcol = [m_new[:, 0:1], m_new[:, HEAD_DIM:HEAD_DIM + 1]]
            p = [jnp.exp(s[hh] - mcol[hh]) for hh in range(2)]
            ls = jnp.where(first, jnp.sum(p[0], axis=1, keepdims=True),
                           jnp.sum(p[1], axis=1, keepdims=True))
            pv = [jnp.dot(p[hh].astype(BF16), vv, preferred_element_type=F32)
                  for hh in range(2)]
            l_sc[rows, :] = alpha * l_sc[rows, :] + ls
            acc[rows, :] = alpha * acc[rows, :] + jnp.where(first, pv[0], pv[1])
            m_sc[rows, :] = m_new
            return carry

        lax.fori_loop(0, seq_len // L, block, 0)

    o_ref[0] = acc[...] / l_sc[...]


def _dil_call(q, k, v, kb, *, batch, seq_len):
    q = q.reshape(batch, seq_len, BRANCH_WIDTH)
    k = k.reshape(batch, seq_len, BRANCH_WIDTH)
    v = v.reshape(batch, seq_len, BRANCH_WIDTH)
    blk = pl.BlockSpec((1, seq_len, LANES), lambda b, p: (b, 0, p))
    return pl.pallas_call(
        functools.partial(_dil_kernel, seq_len=seq_len),
        grid=(batch, N_PAIRS),
        in_specs=[blk, blk, blk, pl.BlockSpec((1, seq_len, LANES), lambda b, p: (0, 0, p))],
        out_specs=blk,
        out_shape=jax.ShapeDtypeStruct((batch, seq_len, BRANCH_WIDTH), F32),
        scratch_shapes=[pltpu.VMEM((seq_len, LANES), F32) for _ in range(7)],
        compiler_params=pltpu.CompilerParams(
            dimension_semantics=("arbitrary", "arbitrary"), vmem_limit_bytes=VMEM_LIMIT),
        name="dil_attn",
    )(q, k, v, kb)


def _merge_kernel(x_ref, oa_ref, ob_ref, oc_ref, g_ref, wz_ref, wg_ref, wb_ref, wo_ref,
                  fg_ref, out_ref, *, last):
    x = x_ref[...]
    ms = jnp.mean(x * x, axis=-1, keepdims=True)
    h = (x * lax.rsqrt(ms + RMS_EPS) * g_ref[...]).astype(BF16)
    merged = None
    for n, o_ref in enumerate((oa_ref, ob_ref, oc_ref)):
        z = jnp.dot(h, wz_ref[:, n * BRANCH_WIDTH:(n + 1) * BRANCH_WIDTH],
                    preferred_element_type=F32)
        y = (o_ref[...] * (z * jax.nn.sigmoid(z))).astype(BF16)
        br = jnp.dot(y, wb_ref[n], preferred_element_type=F32)
        gate = jax.nn.sigmoid(jnp.dot(h, wg_ref[:, n * D_MODEL:(n + 1) * D_MODEL],
                                      preferred_element_type=F32))
        merged = gate * br if merged is None else merged + gate * br
    out = x + jnp.dot(merged.astype(BF16), wo_ref[...], preferred_element_type=F32)
    if last:
        ms = jnp.mean(out * out, axis=-1, keepdims=True)
        out = out * lax.rsqrt(ms + RMS_EPS) * fg_ref[...]
    out_ref[...] = out


def _merge_call(x2d, oa, ob, oc, g, wz, wg, wb, wo, fg, *, last):
    n_tok = x2d.shape[0]
    tm = TM_MERGE
    const2 = lambda i: (0, 0)
    tok = pl.BlockSpec((tm, BRANCH_WIDTH), lambda i: (i, 0))
    return pl.pallas_call(
        functools.partial(_merge_kernel, last=last),
        grid=(n_tok // tm,),
        in_specs=[
            pl.BlockSpec((tm, D_MODEL), lambda i: (i, 0)), tok, tok, tok,
            pl.BlockSpec((1, D_MODEL), const2),
            pl.BlockSpec((D_MODEL, 3 * BRANCH_WIDTH), const2),
            pl.BlockSpec((D_MODEL, 3 * D_MODEL), const2),
            pl.BlockSpec((3, BRANCH_WIDTH, D_MODEL), lambda i: (0, 0, 0)),
            pl.BlockSpec((D_MODEL, D_MODEL), const2),
            pl.BlockSpec((1, D_MODEL), const2),
        ],
        out_specs=pl.BlockSpec((tm, D_MODEL), lambda i: (i, 0)),
        out_shape=jax.ShapeDtypeStruct((n_tok, D_MODEL), F32),
        compiler_params=pltpu.CompilerParams(
            dimension_semantics=("arbitrary",), vmem_limit_bytes=VMEM_LIMIT),
        name="merge",
    )(x2d, oa.reshape(n_tok, BRANCH_WIDTH), ob.reshape(n_tok, BRANCH_WIDTH),
      oc.reshape(n_tok, BRANCH_WIDTH), g, wz, wg, wb, wo, fg)


def _alibi_slopes(n_heads):
    return [2.0 ** (-ALIBI_MAX_EXP * (h + 1) / n_heads) for h in range(n_heads)]


def _alibi_key_bias(seq_len, slopes_per_pair):
    assert seq_len <= 1 << 16
    lane_slope = np.zeros((1, BRANCH_WIDTH), np.float32)
    lane_is_high = np.zeros((1, BRANCH_WIDTH), bool)
    for p, pair in enumerate(slopes_per_pair):
        for c, slope in enumerate(pair):
            assert math.frexp(slope)[0] == 0.5, "ALiBi slopes are powers of two"
            lane_slope[0, p * LANES + 3 * c:p * LANES + 3 * c + 2] = slope
            lane_is_high[0, p * LANES + 3 * c] = True
    pos = jnp.arange(seq_len, dtype=jnp.int32)[:, None]
    piece = jnp.where(lane_is_high, pos - pos % 256, pos % 256).astype(F32)
    return (piece * lane_slope).astype(BF16)[None]


def _fox_bias_scatter():
    m = [[0.0] * BRANCH_WIDTH for _ in range(3 * LANES)]
    for x in range(3):
        for h in range(N_HEADS):
            m[x * LANES + h][(h // 2) * LANES + 3 * (h % 2) + x] = 1.0
    return jnp.asarray(m, dtype=BF16)


def kernel(x, norm_g, w_in, fox_fb, diff_lam, diff_norm_g, w_branch, w_out, final_g):
    batch, seq_len, _ = x.shape
    depth = norm_g.shape[0]
    n_tok = batch * seq_len
    scale = HEAD_DIM ** -0.5

    w = BRANCH_WIDTH
    off = {}
    pos = 0
    for name, width in (("diff_q", w), ("diff_k", w), ("diff_v", w), ("diff_z", w),
                        ("fox_q", w), ("fox_k", w), ("fox_v", w), ("fox_f", N_HEADS),
                        ("fox_z", w), ("dil_q", w), ("dil_k", w), ("dil_v", w),
                        ("dil_z", w), ("merge_g", 3 * D_MODEL)):
        off[name] = (pos, pos + width)
        pos += width

    def cols(wl, name, mult=1.0):
        a, b = off[name]
        return wl[:, a:b] * mult if mult != 1.0 else wl[:, a:b]

    ds = _alibi_slopes(DIFF_HEADS)
    cs = _alibi_slopes(N_HEADS)
    diff_kb = _alibi_key_bias(seq_len, [(ds[h], ds[h]) for h in range(DIFF_HEADS)])
    dil_kb = _alibi_key_bias(seq_len, [(cs[2 * p], cs[2 * p + 1]) for p in range(N_PAIRS)])
    pmat = _fox_bias_scatter()
    ones_g = jnp.ones((1, BRANCH_WIDTH), F32)
    zero_lam = jnp.zeros((4, LANES), F32)

    x2d = x.reshape(n_tok, D_MODEL)
    for l in range(depth):
        wl = w_in[l]
        w_qkv = jnp.concatenate(
            [cols(wl, "diff_q", scale), cols(wl, "diff_k"), cols(wl, "diff_v"),
             cols(wl, "fox_q", scale), cols(wl, "fox_k"), cols(wl, "fox_v"),
             cols(wl, "dil_q", scale), cols(wl, "dil_k"), cols(wl, "dil_v")],
            axis=1).astype(BF16)
        w_f = jnp.pad(cols(wl, "fox_f"), ((0, 0), (0, LANES - N_HEADS))).astype(BF16)
        fb = jnp.pad(fox_fb[l].astype(F32), (0, LANES - N_HEADS))[None]
        g = norm_g[l].astype(F32)[None]

        (dqT, dk, dvT, fqT, fk, fvT, fkb, cq, ck, cv) = _proj_call(
            x2d, g, w_qkv, w_f, fb, pmat, seq_len=seq_len)

        lam_init = 0.8 - 0.6 * math.exp(-0.3 * l)
        lam_pad = jnp.pad(diff_lam[l].astype(F32), ((0, 0), (0, LANES - HEAD_DIM)))
        oa = _pair_attn_call(dqT, dk, diff_kb, dvT, lam_pad,
                             diff_norm_g[l].astype(F32)[None],
                             batch=batch, seq_len=seq_len, diff=True, lam_init=lam_init)
        ob = _pair_attn_call(fqT, fk, fkb.reshape(batch, seq_len, BRANCH_WIDTH), fvT,
                             zero_lam, ones_g,
                             batch=batch, seq_len=seq_len, diff=False, lam_init=0.0)
        oc = _dil_call(cq, ck, cv, dil_kb, batch=batch, seq_len=seq_len)

        wz = jnp.concatenate([cols(wl, "diff_z"), cols(wl, "fox_z"), cols(wl, "dil_z")],
                             axis=1).astype(BF16)
        wg = cols(wl, "merge_g").astype(BF16)
        x2d = _merge_call(x2d, oa, ob, oc, g, wz, wg, w_branch[l].astype(BF16),
                          w_out[l].astype(BF16), final_g.astype(F32)[None],
                          last=(l == depth - 1))
    return x2d.reshape(batch, seq_len, D_MODEL)
```

```python
import functools
import math

import jax
import jax.numpy as jnp
import numpy as np
from jax import lax
from jax.experimental import pallas as pl
from jax.experimental.pallas import tpu as pltpu

F32 = jnp.float32
BF16 = jnp.bfloat16

D_MODEL = 1024
HEAD_DIM = 64
BRANCH_WIDTH = 512
LANES = 128
N_PAIRS = BRANCH_WIDTH // LANES
DIFF_HEADS = 4
N_HEADS = 8
DIL_PATTERNS = ((128, 1), (512, 4), (2048, 16))
DIL_BLOCK = 128
DIL_GROUP = 4
RMS_EPS = 1e-6
ALIBI_MAX_EXP = 8.0
N_QKV_GROUPS = 9
TQ = 256
TK = 256
KV_SUBS = 4
TM_PROJ = 512
TM_MERGE = 512
VMEM_LIMIT = 56 * 1024 * 1024


def _split3(a):
    a1 = a.astype(BF16)
    r1 = a - a1.astype(F32)
    a2 = r1.astype(BF16)
    a3 = (r1 - a2.astype(F32)).astype(BF16)
    return a1, a2, a3


def _proj_kernel(x_ref, g_ref, w_ref, wf_ref, fb_ref, pmat_ref,
                 dqT_ref, dk_ref, dvT_ref, fqT_ref, fk_ref, fvT_ref, fkb_ref,
                 cq_ref, ck_ref, cv_ref, carry_ref, *, tiles_per_seq):
    i = pl.program_id(0)
    tm = x_ref.shape[0]
    x = x_ref[...]
    ms = jnp.mean(x * x, axis=-1, keepdims=True)
    h = (x * lax.rsqrt(ms + RMS_EPS) * g_ref[...]).astype(BF16)

    def proj(g):
        return jnp.dot(h, w_ref[:, g * BRANCH_WIDTH:(g + 1) * BRANCH_WIDTH],
                       preferred_element_type=F32)

    def put_t(ref, val):
        for t in range(tm // TQ):
            ref[t] = val[t * TQ:(t + 1) * TQ, :].T.astype(BF16)

    put_t(dqT_ref, proj(0))
    dk_ref[...] = proj(1).astype(BF16)
    put_t(dvT_ref, proj(2))
    put_t(fqT_ref, proj(3))
    fk_ref[...] = proj(4).astype(BF16)
    put_t(fvT_ref, proj(5))
    cq_ref[...] = proj(6).astype(BF16)
    ck_ref[...] = proj(7).astype(BF16)
    cv_ref[...] = proj(8).astype(BF16)

    f = jnp.dot(h, wf_ref[...], preferred_element_type=F32) + fb_ref[...]
    logf = jnp.minimum(f, 0.0) - jnp.log1p(jnp.exp(-jnp.abs(f)))

    @pl.when(i % tiles_per_seq == 0)
    def _():
        carry_ref[...] = jnp.zeros_like(carry_ref)

    row = lax.broadcasted_iota(jnp.int32, (tm, tm), 0)
    col = lax.broadcasted_iota(jnp.int32, (tm, tm), 1)
    tri = jnp.where(row >= col, 1.0, 0.0).astype(BF16)
    a1, a2, a3 = _split3(logf)
    c = (jnp.dot(tri, a1, preferred_element_type=F32)
         + jnp.dot(tri, a2, preferred_element_type=F32)
         + jnp.dot(tri, a3, preferred_element_type=F32)) + carry_ref[0:1, :]
    carry_ref[0:1, :] = c[tm - 1:tm, :]
    b1, b2, b3 = _split3(-c)
    kb = jnp.dot(jnp.concatenate([b1, b2, b3], axis=1), pmat_ref[...],
                 preferred_element_type=F32)
    fkb_ref[...] = kb.astype(BF16)


def _proj_call(x2d, g, w_qkv, w_f, fb, pmat, *, seq_len):
    n_tok = x2d.shape[0]
    tm = TM_PROJ
    nt = n_tok // tm
    const = lambda i: (0, 0)
    tok = pl.BlockSpec((tm, BRANCH_WIDTH), lambda i: (i, 0))
    tr = pl.BlockSpec((tm // TQ, BRANCH_WIDTH, TQ), lambda i: (i, 0, 0))
    nat_shape = jax.ShapeDtypeStruct((n_tok, BRANCH_WIDTH), BF16)
    tr_shape = jax.ShapeDtypeStruct((n_tok // TQ, BRANCH_WIDTH, TQ), BF16)
    return pl.pallas_call(
        functools.partial(_proj_kernel, tiles_per_seq=seq_len // tm),
        grid=(nt,),
        in_specs=[
            pl.BlockSpec((tm, D_MODEL), lambda i: (i, 0)),
            pl.BlockSpec((1, D_MODEL), const),
            pl.BlockSpec((D_MODEL, N_QKV_GROUPS * BRANCH_WIDTH), const),
            pl.BlockSpec((D_MODEL, LANES), const),
            pl.BlockSpec((1, LANES), const),
            pl.BlockSpec((3 * LANES, BRANCH_WIDTH), const),
        ],
        out_specs=[tr, tok, tr, tr, tok, tr, tok, tok, tok, tok],
        out_shape=[tr_shape, nat_shape, tr_shape, tr_shape, nat_shape, tr_shape,
                   nat_shape, nat_shape, nat_shape, nat_shape],
        scratch_shapes=[pltpu.VMEM((8, LANES), F32)],
        compiler_params=pltpu.CompilerParams(
            dimension_semantics=("arbitrary",), vmem_limit_bytes=VMEM_LIMIT),
        name="proj",
    )(x2d, g, w_qkv, w_f, fb, pmat)


def _pair_attn_kernel(qT_ref, k_ref, kb_ref, vT_ref, lam_ref, g_ref, o_ref,
                      qa_sc, m_sc, l_sc, acc_sc, *, diff, lam_init):
    i = pl.program_id(2)
    qT = qT_ref[0, 0].astype(F32)
    row = lax.broadcasted_iota(jnp.int32, (LANES, TQ), 0)
    qa_sc[:, 0:TQ] = jnp.concatenate(
        [jnp.where(row < HEAD_DIM, qT, 0.0), jnp.where(row < 3, 1.0, 0.0)],
        axis=0).astype(BF16)
    qa_sc[:, TQ:2 * TQ] = jnp.concatenate(
        [jnp.where(row >= HEAD_DIM, qT, 0.0),
         jnp.where((row >= 3) & (row < 6), 1.0, 0.0)], axis=0).astype(BF16)
    m_sc[...] = jnp.full(m_sc.shape, -jnp.inf, F32)
    l_sc[...] = jnp.zeros(l_sc.shape, F32)
    acc_sc[...] = jnp.zeros(acc_sc.shape, F32)

    def tile(key_base, sub0, n_sub, masked):
        rows = n_sub * TK
        ka = jnp.concatenate([k_ref[0, pl.ds(key_base, rows), :],
                              kb_ref[0, pl.ds(key_base, rows), :]], axis=1)
        s = jnp.dot(ka, qa_sc[...], preferred_element_type=F32)
        if masked:
            kr = lax.broadcasted_iota(jnp.int32, (TK, 2 * TQ), 0)
            qc = lax.broadcasted_iota(jnp.int32, (TK, 2 * TQ), 1) % TQ
            tail = jnp.where(kr <= qc, s[rows - TK:], -jnp.inf)
            s = tail if n_sub == 1 else jnp.concatenate([s[:rows - TK], tail], axis=0)
        m_old = m_sc[...]
        m_new = jnp.maximum(m_old, jnp.max(s, axis=0, keepdims=True))
        alpha = jnp.exp(m_old - m_new)
        p = jnp.exp(s - m_new)
        l_sc[...] = alpha * l_sc[...] + jnp.sum(p, axis=0, keepdims=True)
        pb = p.astype(BF16)
        pv = jnp.dot(vT_ref[0, sub0], pb[0:TK], preferred_element_type=F32)
        for t in range(1, n_sub):
            pv = pv + jnp.dot(vT_ref[0, sub0 + t], pb[t * TK:(t + 1) * TK],
                              preferred_element_type=F32)
        acc_sc[...] = alpha * acc_sc[...] + pv
        m_sc[...] = m_new

    n_full = i // KV_SUBS
    rem = i % KV_SUBS

    def body(j, carry):
        tile(pl.multiple_of(j * (KV_SUBS * TK), KV_SUBS * TK), j * KV_SUBS, KV_SUBS, False)
        return carry

    lax.fori_loop(0, n_full, body, 0)
    for r in range(KV_SUBS):
        @pl.when(rem == r)
        def _(r=r):
            tile(pl.multiple_of(n_full * (KV_SUBS * TK), KV_SUBS * TK), n_full * KV_SUBS,
                 r + 1, True)

    a = acc_sc[...] / l_sc[...]
    a0 = a[:, 0:TQ]
    a1 = a[:, TQ:2 * TQ]
    if diff:
        dl = lam_ref[...]
        lam = (jnp.exp(jnp.sum(dl[0:1] * dl[1:2], axis=1, keepdims=True))
               - jnp.exp(jnp.sum(dl[2:3] * dl[3:4], axis=1, keepdims=True)) + lam_init)
        oT = a0 - lam * a1
        ms = jnp.mean(oT * oT, axis=0, keepdims=True)
        oT = oT * lax.rsqrt(ms + RMS_EPS)
        o_ref[0] = (oT.T * g_ref[...]) * (1.0 - lam_init)
    else:
        o_ref[0] = jnp.where(row < HEAD_DIM, a0, a1).T


def _pair_attn_call(qT, k, kb, vT, lam_pad, g, *, batch, seq_len, diff, lam_init):
    nq = seq_len // TQ
    nk = seq_len // TK
    qT = qT.reshape(batch, nq, BRANCH_WIDTH, TQ)
    vT = vT.reshape(batch, nk, BRANCH_WIDTH, TK)
    k = k.reshape(batch, seq_len, BRANCH_WIDTH)
    kb_batched = kb.shape[0] != 1
    kb_map = (lambda b, p, i: (b, 0, p)) if kb_batched else (lambda b, p, i: (0, 0, p))
    return pl.pallas_call(
        functools.partial(_pair_attn_kernel, diff=diff, lam_init=lam_init),
        grid=(batch, N_PAIRS, nq),
        in_specs=[
            pl.BlockSpec((1, 1, LANES, TQ), lambda b, p, i: (b, i, p, 0)),
            pl.BlockSpec((1, seq_len, LANES), lambda b, p, i: (b, 0, p)),
            pl.BlockSpec((1, seq_len, LANES), kb_map),
            pl.BlockSpec((1, nk, LANES, TK), lambda b, p, i: (b, 0, p, 0)),
            pl.BlockSpec((4, LANES), lambda b, p, i: (0, 0)),
            pl.BlockSpec((1, LANES), lambda b, p, i: (0, p)),
        ],
        out_specs=pl.BlockSpec((1, TQ, LANES), lambda b, p, i: (b, i, p)),
        out_shape=jax.ShapeDtypeStruct((batch, seq_len, BRANCH_WIDTH), F32),
        scratch_shapes=[
            pltpu.VMEM((2 * LANES, 2 * TQ), BF16),
            pltpu.VMEM((1, 2 * TQ), F32),
            pltpu.VMEM((1, 2 * TQ), F32),
            pltpu.VMEM((LANES, 2 * TQ), F32),
        ],
        compiler_params=pltpu.CompilerParams(
            dimension_semantics=("arbitrary", "arbitrary", "arbitrary"),
            vmem_limit_bytes=VMEM_LIMIT),
        name="diff_attn" if diff else "fox_attn",
    )(qT, k, kb, vT, lam_pad, g)


def _dil_kernel(q_ref, k_ref, v_ref, kb_ref, o_ref,
                qf, kf, vf, kbf, acc, m_sc, l_sc, mask_sc, *, seq_len):
    L = DIL_BLOCK
    qf[...] = q_ref[0].astype(F32)
    kf[...] = k_ref[0].astype(F32)
    vf[...] = v_ref[0].astype(F32)
    kbf[...] = kb_ref[0].astype(F32)
    acc[...] = jnp.zeros(acc.shape, F32)
    m_sc[...] = jnp.full(m_sc.shape, -jnp.inf, F32)
    l_sc[...] = jnp.zeros(l_sc.shape, F32)

    qi = lax.broadcasted_iota(jnp.int32, (L, 4 * L), 0)
    kj = lax.broadcasted_iota(jnp.int32, (L, 4 * L), 1) % (2 * L)
    band = (kj >= qi) & (kj <= qi + L)
    mask_sc[0] = jnp.where(band & (kj >= L), 0.0, -jnp.inf)
    mask_sc[1] = jnp.where(band, 0.0, -jnp.inf)

    lane = lax.broadcasted_iota(jnp.int32, (L, LANES), 1)
    first = lane < HEAD_DIM
    ones_sel = jnp.where(lane < 6, 1.0, 0.0)
    lane2 = lax.broadcasted_iota(jnp.int32, (2 * L, 2 * LANES), 1)
    keep0 = (lane2 < HEAD_DIM) | ((lane2 >= LANES) & (lane2 < LANES + 3))
    keep1 = ((lane2 >= HEAD_DIM) & (lane2 < LANES)) | ((lane2 >= LANES + 3) & (lane2 < LANES + 6))
    vlane = lax.broadcasted_iota(jnp.int32, (2 * L, LANES), 1)
    vfirst = vlane < HEAD_DIM

    for _, d in DIL_PATTERNS:
        nb = seq_len // (L * d)

        def group(u, carry, d=d, nb=nb):
            rows, prows, slot = [], [], []
            for g in range(DIL_GROUP):
                t = u * DIL_GROUP + g
                r = t // nb
                jb = t % nb
                start = r + d * L * jb
                prev = jnp.where(jb == 0, start, start - d * L)
                rows.append(pl.ds(start, L, stride=d))
                prows.append(pl.ds(prev, L, stride=d))
                slot.append(jnp.minimum(jb, 1))
            m_old = [m_sc[rows[g], :] for g in range(DIL_GROUP)]
            l_old = [l_sc[rows[g], :] for g in range(DIL_GROUP)]
            a_old = [acc[rows[g], :] for g in range(DIL_GROUP)]
            s, vst = [], []
            for g in range(DIL_GROUP):
                qa = jnp.concatenate([qf[rows[g], :], ones_sel], axis=1).astype(BF16)
                ka = jnp.concatenate(
                    [jnp.concatenate([kf[prows[g], :], kbf[prows[g], :]], axis=1),
                     jnp.concatenate([kf[rows[g], :], kbf[rows[g], :]], axis=1)], axis=0)
                kboth = jnp.concatenate([jnp.where(keep0, ka, 0.0), jnp.where(keep1, ka, 0.0)],
                                        axis=0).astype(BF16)
                sg = lax.dot_general(qa, kboth, (((1,), (1,)), ((), ())),
                                     preferred_element_type=F32)
                s.append(sg + mask_sc[slot[g]])
                vv = jnp.concatenate([vf[prows[g], :], vf[rows[g], :]], axis=0)
                vst.append(jnp.concatenate([jnp.where(vfirst, vv, 0.0),
                                            jnp.where(vfirst, 0.0, vv)], axis=0).astype(BF16))
            m_new, alpha = [], []
            for g in range(DIL_GROUP):
                mx = jnp.where(first, jnp.max(s[g][:, :2 * L], axis=1, keepdims=True),
                               jnp.max(s[g][:, 2 * L:], axis=1, keepdims=True))
                m_new.append(jnp.maximum(m_old[g], mx))
                alpha.append(jnp.exp(m_old[g] - m_new[g]))
            for g in range(DIL_GROUP):
                mcol = jnp.concatenate(
                    [jnp.broadcast_to(m_new[g][:, 0:1], (L, 2 * L)),
                     jnp.broadcast_to(m_new[g][:, HEAD_DIM:HEAD_DIM + 1], (L, 2 * L))], axis=1)
                p = jnp.exp(s[g] - mcol)
                ls = jnp.where(first, jnp.sum(p[:, :2 * L], axis=1, keepdims=True),
                               jnp.sum(p[:, 2 * L:], axis=1, keepdims=True))
                pv = jnp.dot(p.astype(BF16), vst[g], preferred_element_type=F32)
                l_old[g] = alpha[g] * l_old[g] + ls
                a_old[g] = alpha[g] * a_old[g] + pv
            for g in range(DIL_GROUP):
                l_sc[rows[g], :] = l_old[g]
                acc[rows[g], :] = a_old[g]
                m_sc[rows[g], :] = m_new[g]
            return carry

        lax.fori_loop(0, seq_len // (L * DIL_GROUP), group, 0)

    o_ref[0] = acc[...] / l_sc[...]


def _dil_call(q, k, v, kb, *, batch, seq_len):
    q = q.reshape(batch, seq_len, BRANCH_WIDTH)
    k = k.reshape(batch, seq_len, BRANCH_WIDTH)
    v = v.reshape(batch, seq_len, BRANCH_WIDTH)
    blk = pl.BlockSpec((1, seq_len, LANES), lambda b, p: (b, 0, p))
    return pl.pallas_call(
        functools.partial(_dil_kernel, seq_len=seq_len),
        grid=(batch, N_PAIRS),
        in_specs=[blk, blk, blk, pl.BlockSpec((1, seq_len, LANES), lambda b, p: (0, 0, p))],
        out_specs=blk,
        out_shape=jax.ShapeDtypeStruct((batch, seq_len, BRANCH_WIDTH), F32),
        scratch_shapes=[pltpu.VMEM((seq_len, LANES), F32) for _ in range(7)]
        + [pltpu.VMEM((2, DIL_BLOCK, 4 * DIL_BLOCK), F32)],
        compiler_params=pltpu.CompilerParams(
            dimension_semantics=("arbitrary", "arbitrary"), vmem_limit_bytes=VMEM_LIMIT),
        name="dil_attn",
    )(q, k, v, kb)


def _merge_kernel(x_ref, oa_ref, ob_ref, oc_ref, g_ref, wz_ref, wg_ref, wb_ref, wo_ref,
                  fg_ref, out_ref, *, last):
    x = x_ref[...]
    ms = jnp.mean(x * x, axis=-1, keepdims=True)
    h = (x * lax.rsqrt(ms + RMS_EPS) * g_ref[...]).astype(BF16)
    merged = None
    for n, o_ref in enumerate((oa_ref, ob_ref, oc_ref)):
        z = jnp.dot(h, wz_ref[:, n * BRANCH_WIDTH:(n + 1) * BRANCH_WIDTH],
                    preferred_element_type=F32)
        y = (o_ref[...] * (z * jax.nn.sigmoid(z))).astype(BF16)
        br = jnp.dot(y, wb_ref[n], preferred_element_type=F32)
        gate = jax.nn.sigmoid(jnp.dot(h, wg_ref[:, n * D_MODEL:(n + 1) * D_MODEL],
                                      preferred_element_type=F32))
        merged = gate * br if merged is None else merged + gate * br
    out = x + jnp.dot(merged.astype(BF16), wo_ref[...], preferred_element_type=F32)
    if last:
        ms = jnp.mean(out * out, axis=-1, keepdims=True)
        out = out * lax.rsqrt(ms + RMS_EPS) * fg_ref[...]
    out_ref[...] = out


def _merge_call(x2d, oa, ob, oc, g, wz, wg, wb, wo, fg, *, last):
    n_tok = x2d.shape[0]
    tm = TM_MERGE
    const2 = lambda i: (0, 0)
    tok = pl.BlockSpec((tm, BRANCH_WIDTH), lambda i: (i, 0))
    return pl.pallas_call(
        functools.partial(_merge_kernel, last=last),
        grid=(n_tok // tm,),
        in_specs=[
            pl.BlockSpec((tm, D_MODEL), lambda i: (i, 0)), tok, tok, tok,
            pl.BlockSpec((1, D_MODEL), const2),
            pl.BlockSpec((D_MODEL, 3 * BRANCH_WIDTH), const2),
            pl.BlockSpec((D_MODEL, 3 * D_MODEL), const2),
            pl.BlockSpec((3, BRANCH_WIDTH, D_MODEL), lambda i: (0, 0, 0)),
            pl.BlockSpec((D_MODEL, D_MODEL), const2),
            pl.BlockSpec((1, D_MODEL), const2),
        ],
        out_specs=pl.BlockSpec((tm, D_MODEL), lambda i: (i, 0)),
        out_shape=jax.ShapeDtypeStruct((n_tok, D_MODEL), F32),
        compiler_params=pltpu.CompilerParams(
            dimension_semantics=("arbitrary",), vmem_limit_bytes=VMEM_LIMIT),
        name="merge",
    )(x2d, oa.reshape(n_tok, BRANCH_WIDTH), ob.reshape(n_tok, BRANCH_WIDTH),
      oc.reshape(n_tok, BRANCH_WIDTH), g, wz, wg, wb, wo, fg)


def _alibi_slopes(n_heads):
    return [2.0 ** (-ALIBI_MAX_EXP * (h + 1) / n_heads) for h in range(n_heads)]


def _alibi_key_bias(seq_len, slopes_per_pair):
    assert seq_len <= 1 << 16
    lane_slope = np.zeros((1, BRANCH_WIDTH), np.float32)
    lane_is_high = np.zeros((1, BRANCH_WIDTH), bool)
    for p, pair in enumerate(slopes_per_pair):
        for c, slope in enumerate(pair):
            assert math.frexp(slope)[0] == 0.5, "ALiBi slopes are powers of two"
            lane_slope[0, p * LANES + 3 * c:p * LANES + 3 * c + 2] = slope
            lane_is_high[0, p * LANES + 3 * c] = True
    pos = jnp.arange(seq_len, dtype=jnp.int32)[:, None]
    piece = jnp.where(lane_is_high, pos - pos % 256, pos % 256).astype(F32)
    return (piece * lane_slope).astype(BF16)[None]


def _fox_bias_scatter():
    m = [[0.0] * BRANCH_WIDTH for _ in range(3 * LANES)]
    for x in range(3):
        for h in range(N_HEADS):
            m[x * LANES + h][(h // 2) * LANES + 3 * (h % 2) + x] = 1.0
    return jnp.asarray(m, dtype=BF16)


def kernel(x, norm_g, w_in, fox_fb, diff_lam, diff_norm_g, w_branch, w_out, final_g):
    batch, seq_len, _ = x.shape
    depth = norm_g.shape[0]
    n_tok = batch * seq_len
    scale = HEAD_DIM ** -0.5

    w = BRANCH_WIDTH
    off = {}
    pos = 0
    for name, width in (("diff_q", w), ("diff_k", w), ("diff_v", w), ("diff_z", w),
                        ("fox_q", w), ("fox_k", w), ("fox_v", w), ("fox_f", N_HEADS),
                        ("fox_z", w), ("dil_q", w), ("dil_k", w), ("dil_v", w),
                        ("dil_z", w), ("merge_g", 3 * D_MODEL)):
        off[name] = (pos, pos + width)
        pos += width

    def cols(wl, name, mult=1.0):
        a, b = off[name]
        return wl[:, a:b] * mult if mult != 1.0 else wl[:, a:b]

    ds = _alibi_slopes(DIFF_HEADS)
    cs = _alibi_slopes(N_HEADS)
    diff_kb = _alibi_key_bias(seq_len, [(ds[h], ds[h]) for h in range(DIFF_HEADS)])
    dil_kb = _alibi_key_bias(seq_len, [(cs[2 * p], cs[2 * p + 1]) for p in range(N_PAIRS)])
    pmat = _fox_bias_scatter()
    ones_g = jnp.ones((1, BRANCH_WIDTH), F32)
    zero_lam = jnp.zeros((4, LANES), F32)

    x2d = x.reshape(n_tok, D_MODEL)
    for l in range(depth):
        wl = w_in[l]
        w_qkv = jnp.concatenate(
            [cols(wl, "diff_q", scale), cols(wl, "diff_k"), cols(wl, "diff_v"),
             cols(wl, "fox_q", scale), cols(wl, "fox_k"), cols(wl, "fox_v"),
             cols(wl, "dil_q", scale), cols(wl, "dil_k"), cols(wl, "dil_v")],
            axis=1).astype(BF16)
        w_f = jnp.pad(cols(wl, "fox_f"), ((0, 0), (0, LANES - N_HEADS))).astype(BF16)
        fb = jnp.pad(fox_fb[l].astype(F32), (0, LANES - N_HEADS))[None]
        g = norm_g[l].astype(F32)[None]

        (dqT, dk, dvT, fqT, fk, fvT, fkb, cq, ck, cv) = _proj_call(
            x2d, g, w_qkv, w_f, fb, pmat, seq_len=seq_len)

        lam_init = 0.8 - 0.6 * math.exp(-0.3 * l)
        lam_pad = jnp.pad(diff_lam[l].astype(F32), ((0, 0), (0, LANES - HEAD_DIM)))
        oa = _pair_attn_call(dqT, dk, diff_kb, dvT, lam_pad,
                             diff_norm_g[l].astype(F32)[None],
                             batch=batch, seq_len=seq_len, diff=True, lam_init=lam_init)
        ob = _pair_attn_call(fqT, fk, fkb.reshape(batch, seq_len, BRANCH_WIDTH), fvT,
                             zero_lam, ones_g,
                             batch=batch, seq_len=seq_len, diff=False, lam_init=0.0)
        oc = _dil_call(cq, ck, cv, dil_kb, batch=batch, seq_len=seq_len)

        wz = jnp.concatenate([cols(wl, "diff_z"), cols(wl, "fox_z"), cols(wl, "dil_z")],
                             axis=1).astype(BF16)
        wg = cols(wl, "merge_g").astype(BF16)
        x2d = _merge_call(x2d, oa, ob, oc, g, wz, wg, w_branch[l].astype(BF16),
                          w_out[l].astype(BF16), final_g.astype(F32)[None],
                          last=(l == depth - 1))
    return x2d.reshape(batch, seq_len, D_MODEL)
```

```python
import functools
import math

import jax
import jax.numpy as jnp
import numpy as np
from jax import lax
from jax.experimental import pallas as pl
from jax.experimental.pallas import tpu as pltpu

F32 = jnp.float32
BF16 = jnp.bfloat16

D_MODEL = 1024
HEAD_DIM = 64
BRANCH_WIDTH = 512
LANES = 128
N_PAIRS = BRANCH_WIDTH // LANES
DIFF_HEADS = 4
N_HEADS = 8
DIL_PATTERNS = ((128, 1), (512, 4), (2048, 16))
DIL_BLOCK = 128
DIL_TQ = 256
DIL_GROUP = 2
RMS_EPS = 1e-6
ALIBI_MAX_EXP = 8.0
N_QKV_GROUPS = 9
TQ = 256
TK = 256
KV_SUBS = 4
TM_PROJ = 512
TM_MERGE = 512
VMEM_LIMIT = 56 * 1024 * 1024


def _split3(a):
    a1 = a.astype(BF16)
    r1 = a - a1.astype(F32)
    a2 = r1.astype(BF16)
    a3 = (r1 - a2.astype(F32)).astype(BF16)
    return a1, a2, a3


def _proj_kernel(x_ref, g_ref, w_ref, wf_ref, fb_ref, pmat_ref,
                 dqT_ref, dk_ref, dvT_ref, fqT_ref, fk_ref, fvT_ref, fkb_ref,
                 cq_ref, ck_ref, cv_ref, carry_ref, *, tiles_per_seq):
    i = pl.program_id(0)
    tm = x_ref.shape[0]
    x = x_ref[...]
    ms = jnp.mean(x * x, axis=-1, keepdims=True)
    h = (x * lax.rsqrt(ms + RMS_EPS) * g_ref[...]).astype(BF16)

    def proj(g):
        return jnp.dot(h, w_ref[:, g * BRANCH_WIDTH:(g + 1) * BRANCH_WIDTH],
                       preferred_element_type=F32)

    def put_t(ref, val):
        for t in range(tm // TQ):
            ref[t] = val[t * TQ:(t + 1) * TQ, :].T.astype(BF16)

    put_t(dqT_ref, proj(0))
    dk_ref[...] = proj(1).astype(BF16)
    put_t(dvT_ref, proj(2))
    put_t(fqT_ref, proj(3))
    fk_ref[...] = proj(4).astype(BF16)
    put_t(fvT_ref, proj(5))
    cq_ref[...] = proj(6).astype(BF16)
    ck_ref[...] = proj(7).astype(BF16)
    cv_ref[...] = proj(8).astype(BF16)

    f = jnp.dot(h, wf_ref[...], preferred_element_type=F32) + fb_ref[...]
    logf = jnp.minimum(f, 0.0) - jnp.log1p(jnp.exp(-jnp.abs(f)))

    @pl.when(i % tiles_per_seq == 0)
    def _():
        carry_ref[...] = jnp.zeros_like(carry_ref)

    row = lax.broadcasted_iota(jnp.int32, (tm, tm), 0)
    col = lax.broadcasted_iota(jnp.int32, (tm, tm), 1)
    tri = jnp.where(row >= col, 1.0, 0.0).astype(BF16)
    a1, a2, a3 = _split3(logf)
    c = (jnp.dot(tri, a1, preferred_element_type=F32)
         + jnp.dot(tri, a2, preferred_element_type=F32)
         + jnp.dot(tri, a3, preferred_element_type=F32)) + carry_ref[0:1, :]
    carry_ref[0:1, :] = c[tm - 1:tm, :]
    b1, b2, b3 = _split3(-c)
    kb = jnp.dot(jnp.concatenate([b1, b2, b3], axis=1), pmat_ref[...],
                 preferred_element_type=F32)
    fkb_ref[...] = kb.astype(BF16)


def _proj_call(x2d, g, w_qkv, w_f, fb, pmat, *, seq_len):
    n_tok = x2d.shape[0]
    tm = TM_PROJ
    nt = n_tok // tm
    const = lambda i: (0, 0)
    tok = pl.BlockSpec((tm, BRANCH_WIDTH), lambda i: (i, 0))
    tr = pl.BlockSpec((tm // TQ, BRANCH_WIDTH, TQ), lambda i: (i, 0, 0))
    nat_shape = jax.ShapeDtypeStruct((n_tok, BRANCH_WIDTH), BF16)
    tr_shape = jax.ShapeDtypeStruct((n_tok // TQ, BRANCH_WIDTH, TQ), BF16)
    return pl.pallas_call(
        functools.partial(_proj_kernel, tiles_per_seq=seq_len // tm),
        grid=(nt,),
        in_specs=[
            pl.BlockSpec((tm, D_MODEL), lambda i: (i, 0)),
            pl.BlockSpec((1, D_MODEL), const),
            pl.BlockSpec((D_MODEL, N_QKV_GROUPS * BRANCH_WIDTH), const),
            pl.BlockSpec((D_MODEL, LANES), const),
            pl.BlockSpec((1, LANES), const),
            pl.BlockSpec((3 * LANES, BRANCH_WIDTH), const),
        ],
        out_specs=[tr, tok, tr, tr, tok, tr, tok, tok, tok, tok],
        out_shape=[tr_shape, nat_shape, tr_shape, tr_shape, nat_shape, tr_shape,
                   nat_shape, nat_shape, nat_shape, nat_shape],
        scratch_shapes=[pltpu.VMEM((8, LANES), F32)],
        compiler_params=pltpu.CompilerParams(
            dimension_semantics=("arbitrary",), vmem_limit_bytes=VMEM_LIMIT),
        name="proj",
    )(x2d, g, w_qkv, w_f, fb, pmat)


def _pair_attn_kernel(qT_ref, k_ref, kb_ref, vT_ref, lam_ref, g_ref, o_ref,
                      qa_sc, m_sc, l_sc, acc_sc, *, diff, lam_init):
    i = pl.program_id(2)
    qT = qT_ref[0, 0].astype(F32)
    row = lax.broadcasted_iota(jnp.int32, (LANES, TQ), 0)
    qa_sc[:, 0:TQ] = jnp.concatenate(
        [jnp.where(row < HEAD_DIM, qT, 0.0), jnp.where(row < 3, 1.0, 0.0)],
        axis=0).astype(BF16)
    qa_sc[:, TQ:2 * TQ] = jnp.concatenate(
        [jnp.where(row >= HEAD_DIM, qT, 0.0),
         jnp.where((row >= 3) & (row < 6), 1.0, 0.0)], axis=0).astype(BF16)
    m_sc[...] = jnp.full(m_sc.shape, -jnp.inf, F32)
    l_sc[...] = jnp.zeros(l_sc.shape, F32)
    acc_sc[...] = jnp.zeros(acc_sc.shape, F32)

    def tile(key_base, sub0, n_sub, masked):
        rows = n_sub * TK
        ka = jnp.concatenate([k_ref[0, pl.ds(key_base, rows), :],
                              kb_ref[0, pl.ds(key_base, rows), :]], axis=1)
        s = jnp.dot(ka, qa_sc[...], preferred_element_type=F32)
        if masked:
            kr = lax.broadcasted_iota(jnp.int32, (TK, 2 * TQ), 0)
            qc = lax.broadcasted_iota(jnp.int32, (TK, 2 * TQ), 1) % TQ
            tail = jnp.where(kr <= qc, s[rows - TK:], -jnp.inf)
            s = tail if n_sub == 1 else jnp.concatenate([s[:rows - TK], tail], axis=0)
        m_old = m_sc[...]
        m_new = jnp.maximum(m_old, jnp.max(s, axis=0, keepdims=True))
        alpha = jnp.exp(m_old - m_new)
        p = jnp.exp(s - m_new)
        l_sc[...] = alpha * l_sc[...] + jnp.sum(p, axis=0, keepdims=True)
        pb = p.astype(BF16)
        pv = jnp.dot(vT_ref[0, sub0], pb[0:TK], preferred_element_type=F32)
        for t in range(1, n_sub):
            pv = pv + jnp.dot(vT_ref[0, sub0 + t], pb[t * TK:(t + 1) * TK],
                              preferred_element_type=F32)
        acc_sc[...] = alpha * acc_sc[...] + pv
        m_sc[...] = m_new

    n_full = i // KV_SUBS
    rem = i % KV_SUBS

    def body(j, carry):
        tile(pl.multiple_of(j * (KV_SUBS * TK), KV_SUBS * TK), j * KV_SUBS, KV_SUBS, False)
        return carry

    lax.fori_loop(0, n_full, body, 0)
    for r in range(KV_SUBS):
        @pl.when(rem == r)
        def _(r=r):
            tile(pl.multiple_of(n_full * (KV_SUBS * TK), KV_SUBS * TK), n_full * KV_SUBS,
                 r + 1, True)

    a = acc_sc[...] / l_sc[...]
    a0 = a[:, 0:TQ]
    a1 = a[:, TQ:2 * TQ]
    if diff:
        dl = lam_ref[...]
        lam = (jnp.exp(jnp.sum(dl[0:1] * dl[1:2], axis=1, keepdims=True))
               - jnp.exp(jnp.sum(dl[2:3] * dl[3:4], axis=1, keepdims=True)) + lam_init)
        oT = a0 - lam * a1
        ms = jnp.mean(oT * oT, axis=0, keepdims=True)
        oT = oT * lax.rsqrt(ms + RMS_EPS)
        o_ref[0] = (oT.T * g_ref[...]) * (1.0 - lam_init)
    else:
        o_ref[0] = jnp.where(row < HEAD_DIM, a0, a1).T


def _pair_attn_call(qT, k, kb, vT, lam_pad, g, *, batch, seq_len, diff, lam_init):
    nq = seq_len // TQ
    nk = seq_len // TK
    qT = qT.reshape(batch, nq, BRANCH_WIDTH, TQ)
    vT = vT.reshape(batch, nk, BRANCH_WIDTH, TK)
    k = k.reshape(batch, seq_len, BRANCH_WIDTH)
    kb_batched = kb.shape[0] != 1
    kb_map = (lambda b, p, i: (b, 0, p)) if kb_batched else (lambda b, p, i: (0, 0, p))
    return pl.pallas_call(
        functools.partial(_pair_attn_kernel, diff=diff, lam_init=lam_init),
        grid=(batch, N_PAIRS, nq),
        in_specs=[
            pl.BlockSpec((1, 1, LANES, TQ), lambda b, p, i: (b, i, p, 0)),
            pl.BlockSpec((1, seq_len, LANES), lambda b, p, i: (b, 0, p)),
            pl.BlockSpec((1, seq_len, LANES), kb_map),
            pl.BlockSpec((1, nk, LANES, TK), lambda b, p, i: (b, 0, p, 0)),
            pl.BlockSpec((4, LANES), lambda b, p, i: (0, 0)),
            pl.BlockSpec((1, LANES), lambda b, p, i: (0, p)),
        ],
        out_specs=pl.BlockSpec((1, TQ, LANES), lambda b, p, i: (b, i, p)),
        out_shape=jax.ShapeDtypeStruct((batch, seq_len, BRANCH_WIDTH), F32),
        scratch_shapes=[
            pltpu.VMEM((2 * LANES, 2 * TQ), BF16),
            pltpu.VMEM((1, 2 * TQ), F32),
            pltpu.VMEM((1, 2 * TQ), F32),
            pltpu.VMEM((LANES, 2 * TQ), F32),
        ],
        compiler_params=pltpu.CompilerParams(
            dimension_semantics=("arbitrary", "arbitrary", "arbitrary"),
            vmem_limit_bytes=VMEM_LIMIT),
        name="diff_attn" if diff else "fox_attn",
    )(qT, k, kb, vT, lam_pad, g)


def _dil_kernel(q_ref, k_ref, v_ref, kb_ref, o_ref,
                qf, kf, vf, kbf, acc, m_sc, l_sc, mask_sc, *, seq_len):
    L = DIL_BLOCK
    T = DIL_TQ
    KEYS = T + L
    qf[...] = q_ref[0].astype(F32)
    kf[...] = k_ref[0].astype(F32)
    vf[...] = v_ref[0].astype(F32)
    kbf[...] = kb_ref[0].astype(F32)
    acc[...] = jnp.zeros(acc.shape, F32)
    m_sc[...] = jnp.full(m_sc.shape, -jnp.inf, F32)
    l_sc[...] = jnp.zeros(l_sc.shape, F32)

    kr = lax.broadcasted_iota(jnp.int32, (2 * L, 2 * L), 0)
    qc = lax.broadcasted_iota(jnp.int32, (2 * L, 2 * L), 1) % L
    band = (kr >= qc) & (kr <= qc + L)
    mask_sc[0] = jnp.where(band & (kr >= L), 0.0, -jnp.inf)
    mask_sc[1] = jnp.where(band, 0.0, -jnp.inf)

    lane = lax.broadcasted_iota(jnp.int32, (T, LANES), 1)
    first = lane < HEAD_DIM
    sel0 = jnp.where(lane < 3, 1.0, 0.0)
    sel1 = jnp.where((lane >= 3) & (lane < 6), 1.0, 0.0)
    frow = lax.broadcasted_iota(jnp.int32, (LANES, L), 0) < HEAD_DIM
    n_qb = T // L

    for _, d in DIL_PATTERNS:
        nt = seq_len // (T * d)

        def group(u, carry, d=d, nt=nt):
            rows, prows, slot = [], [], []
            for g in range(DIL_GROUP):
                t = u * DIL_GROUP + g
                r = t // nt
                it = t % nt
                start = r + d * T * it
                prev = jnp.where(it == 0, start, start - d * L)
                rows.append(pl.ds(start, T, stride=d))
                prows.append(pl.ds(prev, L, stride=d))
                slot.append(jnp.minimum(it, 1))
            m_old = [m_sc[rows[g], :] for g in range(DIL_GROUP)]
            l_old = [l_sc[rows[g], :] for g in range(DIL_GROUP)]
            a_old = [acc[rows[g], :] for g in range(DIL_GROUP)]
            s, vT = [], []
            for g in range(DIL_GROUP):
                qb = qf[rows[g], :]
                q0 = jnp.concatenate([jnp.where(first, qb, 0.0), sel0], axis=1).astype(BF16)
                q1 = jnp.concatenate([jnp.where(first, 0.0, qb), sel1], axis=1).astype(BF16)
                ka = jnp.concatenate(
                    [jnp.concatenate([kf[prows[g], :], kbf[prows[g], :]], axis=1),
                     jnp.concatenate([kf[rows[g], :], kbf[rows[g], :]], axis=1)],
                    axis=0).astype(BF16)
                sg = []
                for b in range(n_qb):
                    qa = jnp.concatenate([q0[b * L:(b + 1) * L], q1[b * L:(b + 1) * L]], axis=0)
                    sb = lax.dot_general(ka[b * L:(b + 2) * L], qa, (((1,), (1,)), ((), ())),
                                         preferred_element_type=F32)
                    sg.append(sb + mask_sc[slot[g] if b == 0 else 1])
                s.append(sg)
                vv = jnp.concatenate([vf[prows[g], :], vf[rows[g], :]], axis=0)
                vT.append(vv.T.astype(BF16))
            num, m_blk, l_blk = [], [], []
            for g in range(DIL_GROUP):
                nb_, mb_, lb_ = [], [], []
                for b in range(n_qb):
                    mg = jnp.max(s[g][b], axis=0, keepdims=True)
                    p = jnp.exp(s[g][b] - mg)
                    lg = jnp.sum(p, axis=0, keepdims=True)
                    oT = jnp.dot(vT[g][:, b * L:(b + 2) * L], p.astype(BF16),
                                 preferred_element_type=F32)
                    nb_.append(jnp.where(frow, oT[:, :L], oT[:, L:]))
                    mb_.append(jnp.where(frow, jnp.broadcast_to(mg[:, :L], (LANES, L)),
                                         jnp.broadcast_to(mg[:, L:], (LANES, L))))
                    lb_.append(jnp.where(frow, jnp.broadcast_to(lg[:, :L], (LANES, L)),
                                         jnp.broadcast_to(lg[:, L:], (LANES, L))))
                num.append(jnp.concatenate(nb_, axis=1).T)
                m_blk.append(jnp.concatenate(mb_, axis=1).T)
                l_blk.append(jnp.concatenate(lb_, axis=1).T)
            for g in range(DIL_GROUP):
                m_new = jnp.maximum(m_old[g], m_blk[g])
                w_old = jnp.exp(m_old[g] - m_new)
                w_blk = jnp.exp(m_blk[g] - m_new)
                l_sc[rows[g], :] = w_old * l_old[g] + w_blk * l_blk[g]
                acc[rows[g], :] = w_old * a_old[g] + w_blk * num[g]
                m_sc[rows[g], :] = m_new
            return carry

        lax.fori_loop(0, seq_len // (T * DIL_GROUP), group, 0)

    o_ref[0] = acc[...] / l_sc[...]


def _dil_call(q, k, v, kb, *, batch, seq_len):
    q = q.reshape(batch, seq_len, BRANCH_WIDTH)
    k = k.reshape(batch, seq_len, BRANCH_WIDTH)
    v = v.reshape(batch, seq_len, BRANCH_WIDTH)
    blk = pl.BlockSpec((1, seq_len, LANES), lambda b, p: (b, 0, p))
    return pl.pallas_call(
        functools.partial(_dil_kernel, seq_len=seq_len),
        grid=(batch, N_PAIRS),
        in_specs=[blk, blk, blk, pl.BlockSpec((1, seq_len, LANES), lambda b, p: (0, 0, p))],
        out_specs=blk,
        out_shape=jax.ShapeDtypeStruct((batch, seq_len, BRANCH_WIDTH), F32),
        scratch_shapes=[pltpu.VMEM((seq_len, LANES), F32) for _ in range(7)]
        + [pltpu.VMEM((2, 2 * DIL_BLOCK, 2 * DIL_BLOCK), F32)],
        compiler_params=pltpu.CompilerParams(
            dimension_semantics=("arbitrary", "arbitrary"), vmem_limit_bytes=VMEM_LIMIT),
        name="dil_attn",
    )(q, k, v, kb)


def _merge_kernel(x_ref, oa_ref, ob_ref, oc_ref, g_ref, wz_ref, wg_ref, wb_ref, wo_ref,
                  fg_ref, out_ref, *, last):
    x = x_ref[...]
    ms = jnp.mean(x * x, axis=-1, keepdims=True)
    h = (x * lax.rsqrt(ms + RMS_EPS) * g_ref[...]).astype(BF16)
    merged = None
    for n, o_ref in enumerate((oa_ref, ob_ref, oc_ref)):
        z = jnp.dot(h, wz_ref[:, n * BRANCH_WIDTH:(n + 1) * BRANCH_WIDTH],
                    preferred_element_type=F32)
        y = (o_ref[...] * (z * jax.nn.sigmoid(z))).astype(BF16)
        br = jnp.dot(y, wb_ref[n], preferred_element_type=F32)
        gate = jax.nn.sigmoid(jnp.dot(h, wg_ref[:, n * D_MODEL:(n + 1) * D_MODEL],
                                      preferred_element_type=F32))
        merged = gate * br if merged is None else merged + gate * br
    out = x + jnp.dot(merged.astype(BF16), wo_ref[...], preferred_element_type=F32)
    if last:
        ms = jnp.mean(out * out, axis=-1, keepdims=True)
        out = out * lax.rsqrt(ms + RMS_EPS) * fg_ref[...]
    out_ref[...] = out


def _merge_call(x2d, oa, ob, oc, g, wz, wg, wb, wo, fg, *, last):
    n_tok = x2d.shape[0]
    tm = TM_MERGE
    const2 = lambda i: (0, 0)
    tok = pl.BlockSpec((tm, BRANCH_WIDTH), lambda i: (i, 0))
    return pl.pallas_call(
        functools.partial(_merge_kernel, last=last),
        grid=(n_tok // tm,),
        in_specs=[
            pl.BlockSpec((tm, D_MODEL), lambda i: (i, 0)), tok, tok, tok,
            pl.BlockSpec((1, D_MODEL), const2),
            pl.BlockSpec((D_MODEL, 3 * BRANCH_WIDTH), const2),
            pl.BlockSpec((D_MODEL, 3 * D_MODEL), const2),
            pl.BlockSpec((3, BRANCH_WIDTH, D_MODEL), lambda i: (0, 0, 0)),
            pl.BlockSpec((D_MODEL, D_MODEL), const2),
            pl.BlockSpec((1, D_MODEL), const2),
        ],
        out_specs=pl.BlockSpec((tm, D_MODEL), lambda i: (i, 0)),
        out_shape=jax.ShapeDtypeStruct((n_tok, D_MODEL), F32),
        compiler_params=pltpu.CompilerParams(
            dimension_semantics=("arbitrary",), vmem_limit_bytes=VMEM_LIMIT),
        name="merge",
    )(x2d, oa.reshape(n_tok, BRANCH_WIDTH), ob.reshape(n_tok, BRANCH_WIDTH),
      oc.reshape(n_tok, BRANCH_WIDTH), g, wz, wg, wb, wo, fg)


def _alibi_slopes(n_heads):
    return [2.0 ** (-ALIBI_MAX_EXP * (h + 1) / n_heads) for h in range(n_heads)]


def _alibi_key_bias(seq_len, slopes_per_pair):
    assert seq_len <= 1 << 16
    lane_slope = np.zeros((1, BRANCH_WIDTH), np.float32)
    lane_is_high = np.zeros((1, BRANCH_WIDTH), bool)
    for p, pair in enumerate(slopes_per_pair):
        for c, slope in enumerate(pair):
            assert math.frexp(slope)[0] == 0.5, "ALiBi slopes are powers of two"
            lane_slope[0, p * LANES + 3 * c:p * LANES + 3 * c + 2] = slope
            lane_is_high[0, p * LANES + 3 * c] = True
    pos = jnp.arange(seq_len, dtype=jnp.int32)[:, None]
    piece = jnp.where(lane_is_high, pos - pos % 256, pos % 256).astype(F32)
    return (piece * lane_slope).astype(BF16)[None]


def _fox_bias_scatter():
    m = [[0.0] * BRANCH_WIDTH for _ in range(3 * LANES)]
    for x in range(3):
        for h in range(N_HEADS):
            m[x * LANES + h][(h // 2) * LANES + 3 * (h % 2) + x] = 1.0
    return jnp.asarray(m, dtype=BF16)


def kernel(x, norm_g, w_in, fox_fb, diff_lam, diff_norm_g, w_branch, w_out, final_g):
    batch, seq_len, _ = x.shape
    depth = norm_g.shape[0]
    n_tok = batch * seq_len
    scale = HEAD_DIM ** -0.5

    w = BRANCH_WIDTH
    off = {}
    pos = 0
    for name, width in (("diff_q", w), ("diff_k", w), ("diff_v", w), ("diff_z", w),
                        ("fox_q", w), ("fox_k", w), ("fox_v", w), ("fox_f", N_HEADS),
                        ("fox_z", w), ("dil_q", w), ("dil_k", w), ("dil_v", w),
                        ("dil_z", w), ("merge_g", 3 * D_MODEL)):
        off[name] = (pos, pos + width)
        pos += width

    def cols(wl, name, mult=1.0):
        a, b = off[name]
        return wl[:, a:b] * mult if mult != 1.0 else wl[:, a:b]

    ds = _alibi_slopes(DIFF_HEADS)
    cs = _alibi_slopes(N_HEADS)
    diff_kb = _alibi_key_bias(seq_len, [(ds[h], ds[h]) for h in range(DIFF_HEADS)])
    dil_kb = _alibi_key_bias(seq_len, [(cs[2 * p], cs[2 * p + 1]) for p in range(N_PAIRS)])
    pmat = _fox_bias_scatter()
    ones_g = jnp.ones((1, BRANCH_WIDTH), F32)
    zero_lam = jnp.zeros((4, LANES), F32)

    x2d = x.reshape(n_tok, D_MODEL)
    for l in range(depth):
        wl = w_in[l]
        w_qkv = jnp.concatenate(
            [cols(wl, "diff_q", scale), cols(wl, "diff_k"), cols(wl, "diff_v"),
             cols(wl, "fox_q", scale), cols(wl, "fox_k"), cols(wl, "fox_v"),
             cols(wl, "dil_q", scale), cols(wl, "dil_k"), cols(wl, "dil_v")],
            axis=1).astype(BF16)
        w_f = jnp.pad(cols(wl, "fox_f"), ((0, 0), (0, LANES - N_HEADS))).astype(BF16)
        fb = jnp.pad(fox_fb[l].astype(F32), (0, LANES - N_HEADS))[None]
        g = norm_g[l].astype(F32)[None]

        (dqT, dk, dvT, fqT, fk, fvT, fkb, cq, ck, cv) = _proj_call(
            x2d, g, w_qkv, w_f, fb, pmat, seq_len=seq_len)

        lam_init = 0.8 - 0.6 * math.exp(-0.3 * l)
        lam_pad = jnp.pad(diff_lam[l].astype(F32), ((0, 0), (0, LANES - HEAD_DIM)))
        oa = _pair_attn_call(dqT, dk, diff_kb, dvT, lam_pad,
                             diff_norm_g[l].astype(F32)[None],
                             batch=batch, seq_len=seq_len, diff=True, lam_init=lam_init)
        ob = _pair_attn_call(fqT, fk, fkb.reshape(batch, seq_len, BRANCH_WIDTH), fvT,
                             zero_lam, ones_g,
                             batch=batch, seq_len=seq_len, diff=False, lam_init=0.0)
        oc = _dil_call(cq, ck, cv, dil_kb, batch=batch, seq_len=seq_len)

        wz = jnp.concatenate([cols(wl, "diff_z"), cols(wl, "fox_z"), cols(wl, "dil_z")],
                             axis=1).astype(BF16)
        wg = cols(wl, "merge_g").astype(BF16)
        x2d = _merge_call(x2d, oa, ob, oc, g, wz, wg, w_branch[l].astype(BF16),
                          w_out[l].astype(BF16), final_g.astype(F32)[None],
                          last=(l == depth - 1))
    return x2d.reshape(batch, seq_len, D_MODEL)
```

```python
import functools
import math

import jax
import jax.numpy as jnp
import numpy as np
from jax import lax
from jax.experimental import pallas as pl
from jax.experimental.pallas import tpu as pltpu

F32 = jnp.float32
BF16 = jnp.bfloat16

D_MODEL = 1024
HEAD_DIM = 64
BRANCH_WIDTH = 512
LANES = 128
N_PAIRS = BRANCH_WIDTH // LANES
DIFF_HEADS = 4
N_HEADS = 8
DIL_PATTERNS = ((128, 1), (512, 4), (2048, 16))
DIL_BLOCK = 128
DIL_TQ = 256
DIL_GROUP = 2
RMS_EPS = 1e-6
ALIBI_MAX_EXP = 8.0
N_QKV_GROUPS = 9
TQ = 256
TK = 256
KV_SUBS = 4
TM_PROJ = 512
TM_MERGE = 512
VMEM_LIMIT = 56 * 1024 * 1024


def _split3(a):
    a1 = a.astype(BF16)
    r1 = a - a1.astype(F32)
    a2 = r1.astype(BF16)
    a3 = (r1 - a2.astype(F32)).astype(BF16)
    return a1, a2, a3


def _proj_kernel(x_ref, g_ref, w_ref, wf_ref, fb_ref, pmat_ref,
                 dqT_ref, dk_ref, dvT_ref, fqT_ref, fk_ref, fvT_ref, fkb_ref,
                 cq_ref, ck_ref, cv_ref, carry_ref, *, tiles_per_seq):
    i = pl.program_id(0)
    tm = x_ref.shape[0]
    x = x_ref[...]
    ms = jnp.mean(x * x, axis=-1, keepdims=True)
    h = (x * lax.rsqrt(ms + RMS_EPS) * g_ref[...]).astype(BF16)

    def proj(g):
        return jnp.dot(h, w_ref[:, g * BRANCH_WIDTH:(g + 1) * BRANCH_WIDTH],
                       preferred_element_type=F32)

    def put_t(ref, val):
        for t in range(tm // TQ):
            ref[t] = val[t * TQ:(t + 1) * TQ, :].T.astype(BF16)

    put_t(dqT_ref, proj(0))
    dk_ref[...] = proj(1).astype(BF16)
    put_t(dvT_ref, proj(2))
    put_t(fqT_ref, proj(3))
    fk_ref[...] = proj(4).astype(BF16)
    put_t(fvT_ref, proj(5))
    cq_ref[...] = proj(6).astype(BF16)
    ck_ref[...] = proj(7).astype(BF16)
    cv_ref[...] = proj(8).astype(BF16)

    f = jnp.dot(h, wf_ref[...], preferred_element_type=F32) + fb_ref[...]
    logf = jnp.minimum(f, 0.0) - jnp.log1p(jnp.exp(-jnp.abs(f)))

    @pl.when(i % tiles_per_seq == 0)
    def _():
        carry_ref[...] = jnp.zeros_like(carry_ref)

    row = lax.broadcasted_iota(jnp.int32, (tm, tm), 0)
    col = lax.broadcasted_iota(jnp.int32, (tm, tm), 1)
    tri = jnp.where(row >= col, 1.0, 0.0).astype(BF16)
    a1, a2, a3 = _split3(logf)
    c = (jnp.dot(tri, a1, preferred_element_type=F32)
         + jnp.dot(tri, a2, preferred_element_type=F32)
         + jnp.dot(tri, a3, preferred_element_type=F32)) + carry_ref[0:1, :]
    carry_ref[0:1, :] = c[tm - 1:tm, :]
    b1, b2, b3 = _split3(-c)
    kb = jnp.dot(jnp.concatenate([b1, b2, b3], axis=1), pmat_ref[...],
                 preferred_element_type=F32)
    fkb_ref[...] = kb.astype(BF16)


def _proj_call(x2d, g, w_qkv, w_f, fb, pmat, *, seq_len):
    n_tok = x2d.shape[0]
    tm = TM_PROJ
    nt = n_tok // tm
    const = lambda i: (0, 0)
    tok = pl.BlockSpec((tm, BRANCH_WIDTH), lambda i: (i, 0))
    tr = pl.BlockSpec((tm // TQ, BRANCH_WIDTH, TQ), lambda i: (i, 0, 0))
    nat_shape = jax.ShapeDtypeStruct((n_tok, BRANCH_WIDTH), BF16)
    tr_shape = jax.ShapeDtypeStruct((n_tok // TQ, BRANCH_WIDTH, TQ), BF16)
    return pl.pallas_call(
        functools.partial(_proj_kernel, tiles_per_seq=seq_len // tm),
        grid=(nt,),
        in_specs=[
            pl.BlockSpec((tm, D_MODEL), lambda i: (i, 0)),
            pl.BlockSpec((1, D_MODEL), const),
            pl.BlockSpec((D_MODEL, N_QKV_GROUPS * BRANCH_WIDTH), const),
            pl.BlockSpec((D_MODEL, LANES), const),
            pl.BlockSpec((1, LANES), const),
            pl.BlockSpec((3 * LANES, BRANCH_WIDTH), const),
        ],
        out_specs=[tr, tok, tr, tr, tok, tr, tok, tok, tok, tok],
        out_shape=[tr_shape, nat_shape, tr_shape, tr_shape, nat_shape, tr_shape,
                   nat_shape, nat_shape, nat_shape, nat_shape],
        scratch_shapes=[pltpu.VMEM((8, LANES), F32)],
        compiler_params=pltpu.CompilerParams(
            dimension_semantics=("arbitrary",), vmem_limit_bytes=VMEM_LIMIT),
        name="proj",
    )(x2d, g, w_qkv, w_f, fb, pmat)


def _pair_attn_kernel(qT_ref, k_ref, kb_ref, vT_ref, lam_ref, g_ref, o_ref,
                      qa_sc, m_sc, l_sc, acc_sc, s_sc, *, diff, lam_init, max_full):
    i = pl.program_id(2)
    qT = qT_ref[0, 0].astype(F32)
    row = lax.broadcasted_iota(jnp.int32, (LANES, TQ), 0)
    qa_sc[:, 0:TQ] = jnp.concatenate(
        [jnp.where(row < HEAD_DIM, qT, 0.0), jnp.where(row < 3, 1.0, 0.0)],
        axis=0).astype(BF16)
    qa_sc[:, TQ:2 * TQ] = jnp.concatenate(
        [jnp.where(row >= HEAD_DIM, qT, 0.0),
         jnp.where((row >= 3) & (row < 6), 1.0, 0.0)], axis=0).astype(BF16)
    m_sc[...] = jnp.full(m_sc.shape, -jnp.inf, F32)
    l_sc[...] = jnp.zeros(l_sc.shape, F32)
    acc_sc[...] = jnp.zeros(acc_sc.shape, F32)

    tile_keys = KV_SUBS * TK

    def score_tile(t, slot):
        base = pl.multiple_of(t * tile_keys, tile_keys)
        ka = jnp.concatenate([k_ref[0, pl.ds(base, tile_keys), :],
                              kb_ref[0, pl.ds(base, tile_keys), :]], axis=1)
        s_sc[slot] = jnp.dot(ka, qa_sc[...], preferred_element_type=F32)

    def consume(s, sub0, n_sub, masked):
        rows = n_sub * TK
        if masked:
            kr = lax.broadcasted_iota(jnp.int32, (TK, 2 * TQ), 0)
            qc = lax.broadcasted_iota(jnp.int32, (TK, 2 * TQ), 1) % TQ
            tail = jnp.where(kr <= qc, s[rows - TK:], -jnp.inf)
            s = tail if n_sub == 1 else jnp.concatenate([s[:rows - TK], tail], axis=0)
        m_old = m_sc[...]
        m_new = jnp.maximum(m_old, jnp.max(s, axis=0, keepdims=True))
        alpha = jnp.exp(m_old - m_new)
        p = jnp.exp(s - m_new)
        l_sc[...] = alpha * l_sc[...] + jnp.sum(p, axis=0, keepdims=True)
        pb = p.astype(BF16)
        pv = jnp.dot(vT_ref[0, sub0], pb[0:TK], preferred_element_type=F32)
        for t in range(1, n_sub):
            pv = pv + jnp.dot(vT_ref[0, sub0 + t], pb[t * TK:(t + 1) * TK],
                              preferred_element_type=F32)
        acc_sc[...] = alpha * acc_sc[...] + pv
        m_sc[...] = m_new

    n_full = i // KV_SUBS
    rem = i % KV_SUBS
    score_tile(0, 0)
    for j in range(max_full):
        @pl.when(j < n_full)
        def _(j=j):
            score_tile(j + 1, (j + 1) % 2)
            consume(s_sc[j % 2], j * KV_SUBS, KV_SUBS, False)
    for r in range(KV_SUBS):
        @pl.when(rem == r)
        def _(r=r):
            consume(s_sc[n_full % 2, 0:(r + 1) * TK, :], n_full * KV_SUBS, r + 1, True)

    a = acc_sc[...] / l_sc[...]
    a0 = a[:, 0:TQ]
    a1 = a[:, TQ:2 * TQ]
    if diff:
        dl = lam_ref[...]
        lam = (jnp.exp(jnp.sum(dl[0:1] * dl[1:2], axis=1, keepdims=True))
               - jnp.exp(jnp.sum(dl[2:3] * dl[3:4], axis=1, keepdims=True)) + lam_init)
        oT = a0 - lam * a1
        ms = jnp.mean(oT * oT, axis=0, keepdims=True)
        oT = oT * lax.rsqrt(ms + RMS_EPS)
        o_ref[0] = (oT.T * g_ref[...]) * (1.0 - lam_init)
    else:
        o_ref[0] = jnp.where(row < HEAD_DIM, a0, a1).T


def _pair_attn_call(qT, k, kb, vT, lam_pad, g, *, batch, seq_len, diff, lam_init):
    nq = seq_len // TQ
    nk = seq_len // TK
    qT = qT.reshape(batch, nq, BRANCH_WIDTH, TQ)
    vT = vT.reshape(batch, nk, BRANCH_WIDTH, TK)
    k = k.reshape(batch, seq_len, BRANCH_WIDTH)
    kb_batched = kb.shape[0] != 1
    kb_map = (lambda b, p, i: (b, 0, p)) if kb_batched else (lambda b, p, i: (0, 0, p))
    return pl.pallas_call(
        functools.partial(_pair_attn_kernel, diff=diff, lam_init=lam_init,
                          max_full=seq_len // (KV_SUBS * TK) - 1),
        grid=(batch, N_PAIRS, nq),
        in_specs=[
            pl.BlockSpec((1, 1, LANES, TQ), lambda b, p, i: (b, i, p, 0)),
            pl.BlockSpec((1, seq_len, LANES), lambda b, p, i: (b, 0, p)),
            pl.BlockSpec((1, seq_len, LANES), kb_map),
            pl.BlockSpec((1, nk, LANES, TK), lambda b, p, i: (b, 0, p, 0)),
            pl.BlockSpec((4, LANES), lambda b, p, i: (0, 0)),
            pl.BlockSpec((1, LANES), lambda b, p, i: (0, p)),
        ],
        out_specs=pl.BlockSpec((1, TQ, LANES), lambda b, p, i: (b, i, p)),
        out_shape=jax.ShapeDtypeStruct((batch, seq_len, BRANCH_WIDTH), F32),
        scratch_shapes=[
            pltpu.VMEM((2 * LANES, 2 * TQ), BF16),
            pltpu.VMEM((1, 2 * TQ), F32),
            pltpu.VMEM((1, 2 * TQ), F32),
            pltpu.VMEM((LANES, 2 * TQ), F32),
            pltpu.VMEM((2, KV_SUBS * TK, 2 * TQ), F32),
        ],
        compiler_params=pltpu.CompilerParams(
            dimension_semantics=("arbitrary", "arbitrary", "arbitrary"),
            vmem_limit_bytes=VMEM_LIMIT),
        name="diff_attn" if diff else "fox_attn",
    )(qT, k, kb, vT, lam_pad, g)


def _dil_kernel(q_ref, k_ref, v_ref, kb_ref, o_ref,
                qf, kf, vf, kbf, acc, m_sc, l_sc, mask_sc, *, seq_len):
    L = DIL_BLOCK
    T = DIL_TQ
    KEYS = T + L
    qf[...] = q_ref[0].astype(F32)
    kf[...] = k_ref[0].astype(F32)
    vf[...] = v_ref[0].astype(F32)
    kbf[...] = kb_ref[0].astype(F32)
    acc[...] = jnp.zeros(acc.shape, F32)
    m_sc[...] = jnp.full(m_sc.shape, -jnp.inf, F32)
    l_sc[...] = jnp.zeros(l_sc.shape, F32)

    kr = lax.broadcasted_iota(jnp.int32, (2 * L, 2 * L), 0)
    qc = lax.broadcasted_iota(jnp.int32, (2 * L, 2 * L), 1) % L
    band = (kr >= qc) & (kr <= qc + L)
    mask_sc[0] = jnp.where(band & (kr >= L), 0.0, -jnp.inf)
    mask_sc[1] = jnp.where(band, 0.0, -jnp.inf)

    lane = lax.broadcasted_iota(jnp.int32, (T, LANES), 1)
    first = lane < HEAD_DIM
    sel0 = jnp.where(lane < 3, 1.0, 0.0)
    sel1 = jnp.where((lane >= 3) & (lane < 6), 1.0, 0.0)
    frow = lax.broadcasted_iota(jnp.int32, (LANES, L), 0) < HEAD_DIM
    n_qb = T // L

    for _, d in DIL_PATTERNS:
        nt = seq_len // (T * d)

        def group(u, carry, d=d, nt=nt):
            rows, prows, slot = [], [], []
            for g in range(DIL_GROUP):
                t = u * DIL_GROUP + g
                r = t // nt
                it = t % nt
                start = r + d * T * it
                prev = jnp.where(it == 0, start, start - d * L)
                rows.append(pl.ds(start, T, stride=d))
                prows.append(pl.ds(prev, L, stride=d))
                slot.append(jnp.minimum(it, 1))
            m_old = [m_sc[rows[g], :] for g in range(DIL_GROUP)]
            l_old = [l_sc[rows[g], :] for g in range(DIL_GROUP)]
            a_old = [acc[rows[g], :] for g in range(DIL_GROUP)]
            s, vT = [], []
            for g in range(DIL_GROUP):
                qb = qf[rows[g], :]
                q0 = jnp.concatenate([jnp.where(first, qb, 0.0), sel0], axis=1).astype(BF16)
                q1 = jnp.concatenate([jnp.where(first, 0.0, qb), sel1], axis=1).astype(BF16)
                ka = jnp.concatenate(
                    [jnp.concatenate([kf[prows[g], :], kbf[prows[g], :]], axis=1),
                     jnp.concatenate([kf[rows[g], :], kbf[rows[g], :]], axis=1)],
                    axis=0).astype(BF16)
                sg = []
                for b in range(n_qb):
                    qa = jnp.concatenate([q0[b * L:(b + 1) * L], q1[b * L:(b + 1) * L]], axis=0)
                    sb = lax.dot_general(ka[b * L:(b + 2) * L], qa, (((1,), (1,)), ((), ())),
                                         preferred_element_type=F32)
                    sg.append(sb + mask_sc[slot[g] if b == 0 else 1])
                s.append(sg)
                vv = jnp.concatenate([vf[prows[g], :], vf[rows[g], :]], axis=0)
                vT.append(vv.T.astype(BF16))
            num, m_blk, l_blk = [], [], []
            for g in range(DIL_GROUP):
                nb_, mb_, lb_ = [], [], []
                for b in range(n_qb):
                    mg = jnp.max(s[g][b], axis=0, keepdims=True)
                    p = jnp.exp(s[g][b] - mg)
                    lg = jnp.sum(p, axis=0, keepdims=True)
                    oT = jnp.dot(vT[g][:, b * L:(b + 2) * L], p.astype(BF16),
                                 preferred_element_type=F32)
                    nb_.append(jnp.where(frow, oT[:, :L], oT[:, L:]))
                    mb_.append(jnp.where(frow, jnp.broadcast_to(mg[:, :L], (LANES, L)),
                                         jnp.broadcast_to(mg[:, L:], (LANES, L))))
                    lb_.append(jnp.where(frow, jnp.broadcast_to(lg[:, :L], (LANES, L)),
                                         jnp.broadcast_to(lg[:, L:], (LANES, L))))
                num.append(jnp.concatenate(nb_, axis=1).T)
                m_blk.append(jnp.concatenate(mb_, axis=1).T)
                l_blk.append(jnp.concatenate(lb_, axis=1).T)
            for g in range(DIL_GROUP):
                m_new = jnp.maximum(m_old[g], m_blk[g])
                w_old = jnp.exp(m_old[g] - m_new)
                w_blk = jnp.exp(m_blk[g] - m_new)
                l_sc[rows[g], :] = w_old * l_old[g] + w_blk * l_blk[g]
                acc[rows[g], :] = w_old * a_old[g] + w_blk * num[g]
                m_sc[rows[g], :] = m_new
            return carry

        lax.fori_loop(0, seq_len // (T * DIL_GROUP), group, 0)

    o_ref[0] = acc[...] / l_sc[...]


def _dil_call(q, k, v, kb, *, batch, seq_len):
    q = q.reshape(batch, seq_len, BRANCH_WIDTH)
    k = k.reshape(batch, seq_len, BRANCH_WIDTH)
    v = v.reshape(batch, seq_len, BRANCH_WIDTH)
    blk = pl.BlockSpec((1, seq_len, LANES), lambda b, p: (b, 0, p))
    return pl.pallas_call(
        functools.partial(_dil_kernel, seq_len=seq_len),
        grid=(batch, N_PAIRS),
        in_specs=[blk, blk, blk, pl.BlockSpec((1, seq_len, LANES), lambda b, p: (0, 0, p))],
        out_specs=blk,
        out_shape=jax.ShapeDtypeStruct((batch, seq_len, BRANCH_WIDTH), F32),
        scratch_shapes=[pltpu.VMEM((seq_len, LANES), F32) for _ in range(7)]
        + [pltpu.VMEM((2, 2 * DIL_BLOCK, 2 * DIL_BLOCK), F32)],
        compiler_params=pltpu.CompilerParams(
            dimension_semantics=("arbitrary", "arbitrary"), vmem_limit_bytes=VMEM_LIMIT),
        name="dil_attn",
    )(q, k, v, kb)


def _merge_kernel(x_ref, oa_ref, ob_ref, oc_ref, g_ref, wz_ref, wg_ref, wb_ref, wo_ref,
                  fg_ref, out_ref, *, last):
    x = x_ref[...]
    ms = jnp.mean(x * x, axis=-1, keepdims=True)
    h = (x * lax.rsqrt(ms + RMS_EPS) * g_ref[...]).astype(BF16)
    merged = None
    for n, o_ref in enumerate((oa_ref, ob_ref, oc_ref)):
        z = jnp.dot(h, wz_ref[:, n * BRANCH_WIDTH:(n + 1) * BRANCH_WIDTH],
                    preferred_element_type=F32)
        y = (o_ref[...] * (z * jax.nn.sigmoid(z))).astype(BF16)
        br = jnp.dot(y, wb_ref[n], preferred_element_type=F32)
        gate = jax.nn.sigmoid(jnp.dot(h, wg_ref[:, n * D_MODEL:(n + 1) * D_MODEL],
                                      preferred_element_type=F32))
        merged = gate * br if merged is None else merged + gate * br
    out = x + jnp.dot(merged.astype(BF16), wo_ref[...], preferred_element_type=F32)
    if last:
        ms = jnp.mean(out * out, axis=-1, keepdims=True)
        out = out * lax.rsqrt(ms + RMS_EPS) * fg_ref[...]
    out_ref[...] = out


def _merge_call(x2d, oa, ob, oc, g, wz, wg, wb, wo, fg, *, last):
    n_tok = x2d.shape[0]
    tm = TM_MERGE
    const2 = lambda i: (0, 0)
    tok = pl.BlockSpec((tm, BRANCH_WIDTH), lambda i: (i, 0))
    return pl.pallas_call(
        functools.partial(_merge_kernel, last=last),
        grid=(n_tok // tm,),
        in_specs=[
            pl.BlockSpec((tm, D_MODEL), lambda i: (i, 0)), tok, tok, tok,
            pl.BlockSpec((1, D_MODEL), const2),
            pl.BlockSpec((D_MODEL, 3 * BRANCH_WIDTH), const2),
            pl.BlockSpec((D_MODEL, 3 * D_MODEL), const2),
            pl.BlockSpec((3, BRANCH_WIDTH, D_MODEL), lambda i: (0, 0, 0)),
            pl.BlockSpec((D_MODEL, D_MODEL), const2),
            pl.BlockSpec((1, D_MODEL), const2),
        ],
        out_specs=pl.BlockSpec((tm, D_MODEL), lambda i: (i, 0)),
        out_shape=jax.ShapeDtypeStruct((n_tok, D_MODEL), F32),
        compiler_params=pltpu.CompilerParams(
            dimension_semantics=("arbitrary",), vmem_limit_bytes=VMEM_LIMIT),
        name="merge",
    )(x2d, oa.reshape(n_tok, BRANCH_WIDTH), ob.reshape(n_tok, BRANCH_WIDTH),
      oc.reshape(n_tok, BRANCH_WIDTH), g, wz, wg, wb, wo, fg)


def _alibi_slopes(n_heads):
    return [2.0 ** (-ALIBI_MAX_EXP * (h + 1) / n_heads) for h in range(n_heads)]


def _alibi_key_bias(seq_len, slopes_per_pair):
    assert seq_len <= 1 << 16
    lane_slope = np.zeros((1, BRANCH_WIDTH), np.float32)
    lane_is_high = np.zeros((1, BRANCH_WIDTH), bool)
    for p, pair in enumerate(slopes_per_pair):
        for c, slope in enumerate(pair):
            assert math.frexp(slope)[0] == 0.5, "ALiBi slopes are powers of two"
            lane_slope[0, p * LANES + 3 * c:p * LANES + 3 * c + 2] = slope
            lane_is_high[0, p * LANES + 3 * c] = True
    pos = jnp.arange(seq_len, dtype=jnp.int32)[:, None]
    piece = jnp.where(lane_is_high, pos - pos % 256, pos % 256).astype(F32)
    return (piece * lane_slope).astype(BF16)[None]


def _fox_bias_scatter():
    m = [[0.0] * BRANCH_WIDTH for _ in range(3 * LANES)]
    for x in range(3):
        for h in range(N_HEADS):
            m[x * LANES + h][(h // 2) * LANES + 3 * (h % 2) + x] = 1.0
    return jnp.asarray(m, dtype=BF16)


def kernel(x, norm_g, w_in, fox_fb, diff_lam, diff_norm_g, w_branch, w_out, final_g):
    batch, seq_len, _ = x.shape
    depth = norm_g.shape[0]
    n_tok = batch * seq_len
    scale = HEAD_DIM ** -0.5

    w = BRANCH_WIDTH
    off = {}
    pos = 0
    for name, width in (("diff_q", w), ("diff_k", w), ("diff_v", w), ("diff_z", w),
                        ("fox_q", w), ("fox_k", w), ("fox_v", w), ("fox_f", N_HEADS),
                        ("fox_z", w), ("dil_q", w), ("dil_k", w), ("dil_v", w),
                        ("dil_z", w), ("merge_g", 3 * D_MODEL)):
        off[name] = (pos, pos + width)
        pos += width

    def cols(wl, name, mult=1.0):
        a, b = off[name]
        return wl[:, a:b] * mult if mult != 1.0 else wl[:, a:b]

    ds = _alibi_slopes(DIFF_HEADS)
    cs = _alibi_slopes(N_HEADS)
    diff_kb = _alibi_key_bias(seq_len, [(ds[h], ds[h]) for h in range(DIFF_HEADS)])
    dil_kb = _alibi_key_bias(seq_len, [(cs[2 * p], cs[2 * p + 1]) for p in range(N_PAIRS)])
    pmat = _fox_bias_scatter()
    ones_g = jnp.ones((1, BRANCH_WIDTH), F32)
    zero_lam = jnp.zeros((4, LANES), F32)

    x2d = x.reshape(n_tok, D_MODEL)
    for l in range(depth):
        wl = w_in[l]
        w_qkv = jnp.concatenate(
            [cols(wl, "diff_q", scale), cols(wl, "diff_k"), cols(wl, "diff_v"),
             cols(wl, "fox_q", scale), cols(wl, "fox_k"), cols(wl, "fox_v"),
             cols(wl, "dil_q", scale), cols(wl, "dil_k"), cols(wl, "dil_v")],
            axis=1).astype(BF16)
        w_f = jnp.pad(cols(wl, "fox_f"), ((0, 0), (0, LANES - N_HEADS))).astype(BF16)
        fb = jnp.pad(fox_fb[l].astype(F32), (0, LANES - N_HEADS))[None]
        g = norm_g[l].astype(F32)[None]

        (dqT, dk, dvT, fqT, fk, fvT, fkb, cq, ck, cv) = _proj_call(
            x2d, g, w_qkv, w_f, fb, pmat, seq_len=seq_len)

        lam_init = 0.8 - 0.6 * math.exp(-0.3 * l)
        lam_pad = jnp.pad(diff_lam[l].astype(F32), ((0, 0), (0, LANES - HEAD_DIM)))
        oa = _pair_attn_call(dqT, dk, diff_kb, dvT, lam_pad,
                             diff_norm_g[l].astype(F32)[None],
                             batch=batch, seq_len=seq_len, diff=True, lam_init=lam_init)
        ob = _pair_attn_call(fqT, fk, fkb.reshape(batch, seq_len, BRANCH_WIDTH), fvT,
                             zero_lam, ones_g,
                             batch=batch, seq_len=seq_len, diff=False, lam_init=0.0)
        oc = _dil_call(cq, ck, cv, dil_kb, batch=batch, seq_len=seq_len)

        wz = jnp.concatenate([cols(wl, "diff_z"), cols(wl, "fox_z"), cols(wl, "dil_z")],
                             axis=1).astype(BF16)
        wg = cols(wl, "merge_g").astype(BF16)
        x2d = _merge_call(x2d, oa, ob, oc, g, wz, wg, w_branch[l].astype(BF16),
                          w_out[l].astype(BF16), final_g.astype(F32)[None],
                          last=(l == depth - 1))
    return x2d.reshape(batch, seq_len, D_MODEL)
```

```python
import functools
import math

import jax
import jax.numpy as jnp
import numpy as np
from jax import lax
from jax.experimental import pallas as pl
from jax.experimental.pallas import tpu as pltpu

F32 = jnp.float32
BF16 = jnp.bfloat16

D_MODEL = 1024
HEAD_DIM = 64
BRANCH_WIDTH = 512
LANES = 128
N_PAIRS = BRANCH_WIDTH // LANES
DIFF_HEADS = 4
N_HEADS = 8
DIL_PATTERNS = ((128, 1), (512, 4), (2048, 16))
DIL_BLOCK = 128
DIL_TQ = 256
DIL_GROUP = 2
RMS_EPS = 1e-6
ALIBI_MAX_EXP = 8.0
N_QKV_GROUPS = 9
TQ = 256
TK = 256
KV_SUBS = 4
TM_PROJ = 512
TM_MERGE = 512
VMEM_LIMIT = 56 * 1024 * 1024


def _split3(a):
    a1 = a.astype(BF16)
    r1 = a - a1.astype(F32)
    a2 = r1.astype(BF16)
    a3 = (r1 - a2.astype(F32)).astype(BF16)
    return a1, a2, a3


def _proj_kernel(x_ref, g_ref, w_ref, wf_ref, fb_ref, pmat_ref,
                 dqT_ref, dk_ref, dvT_ref, fqT_ref, fk_ref, fvT_ref, fkb_ref,
                 cq_ref, ck_ref, cv_ref, carry_ref, *, tiles_per_seq):
    i = pl.program_id(0)
    tm = x_ref.shape[0]
    x = x_ref[...]
    ms = jnp.mean(x * x, axis=-1, keepdims=True)
    h = (x * lax.rsqrt(ms + RMS_EPS) * g_ref[...]).astype(BF16)

    def proj(g):
        return jnp.dot(h, w_ref[:, g * BRANCH_WIDTH:(g + 1) * BRANCH_WIDTH],
                       preferred_element_type=F32)

    def put_t(ref, val):
        for t in range(tm // TQ):
            ref[t] = val[t * TQ:(t + 1) * TQ, :].T.astype(BF16)

    put_t(dqT_ref, proj(0))
    dk_ref[...] = proj(1).astype(BF16)
    put_t(dvT_ref, proj(2))
    put_t(fqT_ref, proj(3))
    fk_ref[...] = proj(4).astype(BF16)
    put_t(fvT_ref, proj(5))
    cq_ref[...] = proj(6).astype(BF16)
    ck_ref[...] = proj(7).astype(BF16)
    cv_ref[...] = proj(8).astype(BF16)

    f = jnp.dot(h, wf_ref[...], preferred_element_type=F32) + fb_ref[...]
    logf = jnp.minimum(f, 0.0) - jnp.log1p(jnp.exp(-jnp.abs(f)))

    @pl.when(i % tiles_per_seq == 0)
    def _():
        carry_ref[...] = jnp.zeros_like(carry_ref)

    row = lax.broadcasted_iota(jnp.int32, (tm, tm), 0)
    col = lax.broadcasted_iota(jnp.int32, (tm, tm), 1)
    tri = jnp.where(row >= col, 1.0, 0.0).astype(BF16)
    a1, a2, a3 = _split3(logf)
    c = (jnp.dot(tri, a1, preferred_element_type=F32)
         + jnp.dot(tri, a2, preferred_element_type=F32)
         + jnp.dot(tri, a3, preferred_element_type=F32)) + carry_ref[0:1, :]
    carry_ref[0:1, :] = c[tm - 1:tm, :]
    b1, b2, b3 = _split3(-c)
    kb = jnp.dot(jnp.concatenate([b1, b2, b3], axis=1), pmat_ref[...],
                 preferred_element_type=F32)
    fkb_ref[...] = kb.astype(BF16)


def _proj_call(x2d, g, w_qkv, w_f, fb, pmat, *, seq_len):
    n_tok = x2d.shape[0]
    tm = TM_PROJ
    nt = n_tok // tm
    const = lambda i: (0, 0)
    tok = pl.BlockSpec((tm, BRANCH_WIDTH), lambda i: (i, 0))
    tr = pl.BlockSpec((tm // TQ, BRANCH_WIDTH, TQ), lambda i: (i, 0, 0))
    nat_shape = jax.ShapeDtypeStruct((n_tok, BRANCH_WIDTH), BF16)
    tr_shape = jax.ShapeDtypeStruct((n_tok // TQ, BRANCH_WIDTH, TQ), BF16)
    return pl.pallas_call(
        functools.partial(_proj_kernel, tiles_per_seq=seq_len // tm),
        grid=(nt,),
        in_specs=[
            pl.BlockSpec((tm, D_MODEL), lambda i: (i, 0)),
            pl.BlockSpec((1, D_MODEL), const),
            pl.BlockSpec((D_MODEL, N_QKV_GROUPS * BRANCH_WIDTH), const),
            pl.BlockSpec((D_MODEL, LANES), const),
            pl.BlockSpec((1, LANES), const),
            pl.BlockSpec((3 * LANES, BRANCH_WIDTH), const),
        ],
        out_specs=[tr, tok, tr, tr, tok, tr, tok, tok, tok, tok],
        out_shape=[tr_shape, nat_shape, tr_shape, tr_shape, nat_shape, tr_shape,
                   nat_shape, nat_shape, nat_shape, nat_shape],
        scratch_shapes=[pltpu.VMEM((8, LANES), F32)],
        compiler_params=pltpu.CompilerParams(
            dimension_semantics=("arbitrary",), vmem_limit_bytes=VMEM_LIMIT),
        name="proj",
    )(x2d, g, w_qkv, w_f, fb, pmat)


def _pair_attn_kernel(qT_ref, qTn_ref, k_ref, kb_ref, vT_ref, lam_ref, g_ref, o_ref,
                      qa_sc, m_sc, l_sc, acc_sc, s_sc, t0_sc, *, diff, lam_init, max_full):
    i = pl.program_id(2)
    par = i % 2
    row = lax.broadcasted_iota(jnp.int32, (LANES, TQ), 0)

    def build_qa(q_ref, slot):
        qT = q_ref[0, 0].astype(F32)
        qa_sc[slot, :, 0:TQ] = jnp.concatenate(
            [jnp.where(row < HEAD_DIM, qT, 0.0), jnp.where(row < 3, 1.0, 0.0)],
            axis=0).astype(BF16)
        qa_sc[slot, :, TQ:2 * TQ] = jnp.concatenate(
            [jnp.where(row >= HEAD_DIM, qT, 0.0),
             jnp.where((row >= 3) & (row < 6), 1.0, 0.0)], axis=0).astype(BF16)

    tile_keys = KV_SUBS * TK

    def score_tile(t, qa_slot, dst, dst_slot):
        base = pl.multiple_of(t * tile_keys, tile_keys)
        ka = jnp.concatenate([k_ref[0, pl.ds(base, tile_keys), :],
                              kb_ref[0, pl.ds(base, tile_keys), :]], axis=1)
        dst[dst_slot] = jnp.dot(ka, qa_sc[qa_slot], preferred_element_type=F32)

    @pl.when(i == 0)
    def _():
        build_qa(qT_ref, 0)
        score_tile(0, 0, t0_sc, 0)

    build_qa(qTn_ref, 1 - par)
    m_sc[...] = jnp.full(m_sc.shape, -jnp.inf, F32)
    l_sc[...] = jnp.zeros(l_sc.shape, F32)
    acc_sc[...] = jnp.zeros(acc_sc.shape, F32)

    def consume(s, sub0, n_sub, masked):
        rows = n_sub * TK
        if masked:
            kr = lax.broadcasted_iota(jnp.int32, (TK, 2 * TQ), 0)
            qc = lax.broadcasted_iota(jnp.int32, (TK, 2 * TQ), 1) % TQ
            tail = jnp.where(kr <= qc, s[rows - TK:], -jnp.inf)
            s = tail if n_sub == 1 else jnp.concatenate([s[:rows - TK], tail], axis=0)
        m_old = m_sc[...]
        m_new = jnp.maximum(m_old, jnp.max(s, axis=0, keepdims=True))
        alpha = jnp.exp(m_old - m_new)
        p = jnp.exp(s - m_new)
        l_sc[...] = alpha * l_sc[...] + jnp.sum(p, axis=0, keepdims=True)
        pb = p.astype(BF16)
        pv = jnp.dot(vT_ref[0, sub0], pb[0:TK], preferred_element_type=F32)
        for t in range(1, n_sub):
            pv = pv + jnp.dot(vT_ref[0, sub0 + t], pb[t * TK:(t + 1) * TK],
                              preferred_element_type=F32)
        acc_sc[...] = alpha * acc_sc[...] + pv
        m_sc[...] = m_new

    n_full = i // KV_SUBS
    rem = i % KV_SUBS
    for j in range(max_full):
        @pl.when(j < n_full)
        def _(j=j):
            score_tile(j + 1, par, s_sc, (j + 1) % 2)
            consume(t0_sc[par] if j == 0 else s_sc[j % 2], j * KV_SUBS, KV_SUBS, False)
    for r in range(KV_SUBS):
        for diag_is_tile0 in (True, False):
            @pl.when((rem == r) & ((n_full == 0) == diag_is_tile0))
            def _(r=r, diag_is_tile0=diag_is_tile0):
                score_tile(0, 1 - par, t0_sc, (r + 1) % 2)
                rows = (r + 1) * TK
                s = t0_sc[r % 2, 0:rows, :] if diag_is_tile0 else s_sc[n_full % 2, 0:rows, :]
                consume(s, n_full * KV_SUBS, r + 1, True)

    a = acc_sc[...] / l_sc[...]
    a0 = a[:, 0:TQ]
    a1 = a[:, TQ:2 * TQ]
    if diff:
        dl = lam_ref[...]
        lam = (jnp.exp(jnp.sum(dl[0:1] * dl[1:2], axis=1, keepdims=True))
               - jnp.exp(jnp.sum(dl[2:3] * dl[3:4], axis=1, keepdims=True)) + lam_init)
        oT = a0 - lam * a1
        ms = jnp.mean(oT * oT, axis=0, keepdims=True)
        oT = oT * lax.rsqrt(ms + RMS_EPS)
        o_ref[0] = (oT.T * g_ref[...]) * (1.0 - lam_init)
    else:
        o_ref[0] = jnp.where(row < HEAD_DIM, a0, a1).T


def _pair_attn_call(qT, k, kb, vT, lam_pad, g, *, batch, seq_len, diff, lam_init):
    nq = seq_len // TQ
    nk = seq_len // TK
    qT = qT.reshape(batch, nq, BRANCH_WIDTH, TQ)
    vT = vT.reshape(batch, nk, BRANCH_WIDTH, TK)
    k = k.reshape(batch, seq_len, BRANCH_WIDTH)
    kb_batched = kb.shape[0] != 1
    kb_map = (lambda b, p, i: (b, 0, p)) if kb_batched else (lambda b, p, i: (0, 0, p))
    return pl.pallas_call(
        functools.partial(_pair_attn_kernel, diff=diff, lam_init=lam_init,
                          max_full=seq_len // (KV_SUBS * TK) - 1),
        grid=(batch, N_PAIRS, nq),
        in_specs=[
            pl.BlockSpec((1, 1, LANES, TQ), lambda b, p, i: (b, i, p, 0)),
            pl.BlockSpec((1, 1, LANES, TQ),
                         lambda b, p, i: (b, jnp.minimum(i + 1, nq - 1), p, 0)),
            pl.BlockSpec((1, seq_len, LANES), lambda b, p, i: (b, 0, p)),
            pl.BlockSpec((1, seq_len, LANES), kb_map),
            pl.BlockSpec((1, nk, LANES, TK), lambda b, p, i: (b, 0, p, 0)),
            pl.BlockSpec((4, LANES), lambda b, p, i: (0, 0)),
            pl.BlockSpec((1, LANES), lambda b, p, i: (0, p)),
        ],
        out_specs=pl.BlockSpec((1, TQ, LANES), lambda b, p, i: (b, i, p)),
        out_shape=jax.ShapeDtypeStruct((batch, seq_len, BRANCH_WIDTH), F32),
        scratch_shapes=[
            pltpu.VMEM((2, 2 * LANES, 2 * TQ), BF16),
            pltpu.VMEM((1, 2 * TQ), F32),
            pltpu.VMEM((1, 2 * TQ), F32),
            pltpu.VMEM((LANES, 2 * TQ), F32),
            pltpu.VMEM((2, KV_SUBS * TK, 2 * TQ), F32),
            pltpu.VMEM((2, KV_SUBS * TK, 2 * TQ), F32),
        ],
        compiler_params=pltpu.CompilerParams(
            dimension_semantics=("arbitrary", "arbitrary", "arbitrary"),
            vmem_limit_bytes=VMEM_LIMIT),
        name="diff_attn" if diff else "fox_attn",
    )(qT, qT, k, kb, vT, lam_pad, g)


def _dil_kernel(q_ref, k_ref, v_ref, kb_ref, o_ref,
                qf, kf, vf, kbf, acc, m_sc, l_sc, mask_sc, *, seq_len):
    L = DIL_BLOCK
    T = DIL_TQ
    KEYS = T + L
    qf[...] = q_ref[0].astype(F32)
    kf[...] = k_ref[0].astype(F32)
    vf[...] = v_ref[0].astype(F32)
    kbf[...] = kb_ref[0].astype(F32)
    acc[...] = jnp.zeros(acc.shape, F32)
    m_sc[...] = jnp.full(m_sc.shape, -jnp.inf, F32)
    l_sc[...] = jnp.zeros(l_sc.shape, F32)

    kr = lax.broadcasted_iota(jnp.int32, (2 * L, 2 * L), 0)
    qc = lax.broadcasted_iota(jnp.int32, (2 * L, 2 * L), 1) % L
    band = (kr >= qc) & (kr <= qc + L)
    mask_sc[0] = jnp.where(band & (kr >= L), 0.0, -jnp.inf)
    mask_sc[1] = jnp.where(band, 0.0, -jnp.inf)

    lane = lax.broadcasted_iota(jnp.int32, (T, LANES), 1)
    first = lane < HEAD_DIM
    sel0 = jnp.where(lane < 3, 1.0, 0.0)
    sel1 = jnp.where((lane >= 3) & (lane < 6), 1.0, 0.0)
    frow = lax.broadcasted_iota(jnp.int32, (LANES, L), 0) < HEAD_DIM
    n_qb = T // L

    for _, d in DIL_PATTERNS:
        nt = seq_len // (T * d)

        def group(u, carry, d=d, nt=nt):
            rows, prows, slot = [], [], []
            for g in range(DIL_GROUP):
                t = u * DIL_GROUP + g
                r = t // nt
                it = t % nt
                start = r + d * T * it
                prev = jnp.where(it == 0, start, start - d * L)
                rows.append(pl.ds(start, T, stride=d))
                prows.append(pl.ds(prev, L, stride=d))
                slot.append(jnp.minimum(it, 1))
            m_old = [m_sc[rows[g], :] for g in range(DIL_GROUP)]
            l_old = [l_sc[rows[g], :] for g in range(DIL_GROUP)]
            a_old = [acc[rows[g], :] for g in range(DIL_GROUP)]
            s, vT = [], []
            for g in range(DIL_GROUP):
                qb = qf[rows[g], :]
                q0 = jnp.concatenate([jnp.where(first, qb, 0.0), sel0], axis=1).astype(BF16)
                q1 = jnp.concatenate([jnp.where(first, 0.0, qb), sel1], axis=1).astype(BF16)
                ka = jnp.concatenate(
                    [jnp.concatenate([kf[prows[g], :], kbf[prows[g], :]], axis=1),
                     jnp.concatenate([kf[rows[g], :], kbf[rows[g], :]], axis=1)],
                    axis=0).astype(BF16)
                sg = []
                for b in range(n_qb):
                    qa = jnp.concatenate([q0[b * L:(b + 1) * L], q1[b * L:(b + 1) * L]], axis=0)
                    sb = lax.dot_general(ka[b * L:(b + 2) * L], qa, (((1,), (1,)), ((), ())),
                                         preferred_element_type=F32)
                    sg.append(sb + mask_sc[slot[g] if b == 0 else 1])
                s.append(sg)
                vv = jnp.concatenate([vf[prows[g], :], vf[rows[g], :]], axis=0)
                vT.append(vv.T.astype(BF16))
            num, m_blk, l_blk = [], [], []
            for g in range(DIL_GROUP):
                nb_, mb_, lb_ = [], [], []
                for b in range(n_qb):
                    mg = jnp.max(s[g][b], axis=0, keepdims=True)
                    p = jnp.exp(s[g][b] - mg)
                    lg = jnp.sum(p, axis=0, keepdims=True)
                    oT = jnp.dot(vT[g][:, b * L:(b + 2) * L], p.astype(BF16),
                                 preferred_element_type=F32)
                    nb_.append(jnp.where(frow, oT[:, :L], oT[:, L:]))
                    mb_.append(jnp.where(frow, jnp.broadcast_to(mg[:, :L], (LANES, L)),
                                         jnp.broadcast_to(mg[:, L:], (LANES, L))))
                    lb_.append(jnp.where(frow, jnp.broadcast_to(lg[:, :L], (LANES, L)),
                                         jnp.broadcast_to(lg[:, L:], (LANES, L))))
                num.append(jnp.concatenate(nb_, axis=1).T)
                m_blk.append(jnp.concatenate(mb_, axis=1).T)
                l_blk.append(jnp.concatenate(lb_, axis=1).T)
            for g in range(DIL_GROUP):
                m_new = jnp.maximum(m_old[g], m_blk[g])
                w_old = jnp.exp(m_old[g] - m_new)
                w_blk = jnp.exp(m_blk[g] - m_new)
                l_sc[rows[g], :] = w_old * l_old[g] + w_blk * l_blk[g]
                acc[rows[g], :] = w_old * a_old[g] + w_blk * num[g]
                m_sc[rows[g], :] = m_new
            return carry

        lax.fori_loop(0, seq_len // (T * DIL_GROUP), group, 0)

    o_ref[0] = acc[...] / l_sc[...]


def _dil_call(q, k, v, kb, *, batch, seq_len):
    q = q.reshape(batch, seq_len, BRANCH_WIDTH)
    k = k.reshape(batch, seq_len, BRANCH_WIDTH)
    v = v.reshape(batch, seq_len, BRANCH_WIDTH)
    blk = pl.BlockSpec((1, seq_len, LANES), lambda b, p: (b, 0, p))
    return pl.pallas_call(
        functools.partial(_dil_kernel, seq_len=seq_len),
        grid=(batch, N_PAIRS),
        in_specs=[blk, blk, blk, pl.BlockSpec((1, seq_len, LANES), lambda b, p: (0, 0, p))],
        out_specs=blk,
        out_shape=jax.ShapeDtypeStruct((batch, seq_len, BRANCH_WIDTH), F32),
        scratch_shapes=[pltpu.VMEM((seq_len, LANES), F32) for _ in range(7)]
        + [pltpu.VMEM((2, 2 * DIL_BLOCK, 2 * DIL_BLOCK), F32)],
        compiler_params=pltpu.CompilerParams(
            dimension_semantics=("arbitrary", "arbitrary"), vmem_limit_bytes=VMEM_LIMIT),
        name="dil_attn",
    )(q, k, v, kb)


def _merge_kernel(x_ref, oa_ref, ob_ref, oc_ref, g_ref, wz_ref, wg_ref, wb_ref, wo_ref,
                  fg_ref, out_ref, *, last):
    x = x_ref[...]
    ms = jnp.mean(x * x, axis=-1, keepdims=True)
    h = (x * lax.rsqrt(ms + RMS_EPS) * g_ref[...]).astype(BF16)
    merged = None
    for n, o_ref in enumerate((oa_ref, ob_ref, oc_ref)):
        z = jnp.dot(h, wz_ref[:, n * BRANCH_WIDTH:(n + 1) * BRANCH_WIDTH],
                    preferred_element_type=F32)
        y = (o_ref[...] * (z * jax.nn.sigmoid(z))).astype(BF16)
        br = jnp.dot(y, wb_ref[n], preferred_element_type=F32)
        gate = jax.nn.sigmoid(jnp.dot(h, wg_ref[:, n * D_MODEL:(n + 1) * D_MODEL],
                                      preferred_element_type=F32))
        merged = gate * br if merged is None else merged + gate * br
    out = x + jnp.dot(merged.astype(BF16), wo_ref[...], preferred_element_type=F32)
    if last:
        ms = jnp.mean(out * out, axis=-1, keepdims=True)
        out = out * lax.rsqrt(ms + RMS_EPS) * fg_ref[...]
    out_ref[...] = out


def _merge_call(x2d, oa, ob, oc, g, wz, wg, wb, wo, fg, *, last):
    n_tok = x2d.shape[0]
    tm = TM_MERGE
    const2 = lambda i: (0, 0)
    tok = pl.BlockSpec((tm, BRANCH_WIDTH), lambda i: (i, 0))
    return pl.pallas_call(
        functools.partial(_merge_kernel, last=last),
        grid=(n_tok // tm,),
        in_specs=[
            pl.BlockSpec((tm, D_MODEL), lambda i: (i, 0)), tok, tok, tok,
            pl.BlockSpec((1, D_MODEL), const2),
            pl.BlockSpec((D_MODEL, 3 * BRANCH_WIDTH), const2),
            pl.BlockSpec((D_MODEL, 3 * D_MODEL), const2),
            pl.BlockSpec((3, BRANCH_WIDTH, D_MODEL), lambda i: (0, 0, 0)),
            pl.BlockSpec((D_MODEL, D_MODEL), const2),
            pl.BlockSpec((1, D_MODEL), const2),
        ],
        out_specs=pl.BlockSpec((tm, D_MODEL), lambda i: (i, 0)),
        out_shape=jax.ShapeDtypeStruct((n_tok, D_MODEL), F32),
        compiler_params=pltpu.CompilerParams(
            dimension_semantics=("arbitrary",), vmem_limit_bytes=VMEM_LIMIT),
        name="merge",
    )(x2d, oa.reshape(n_tok, BRANCH_WIDTH), ob.reshape(n_tok, BRANCH_WIDTH),
      oc.reshape(n_tok, BRANCH_WIDTH), g, wz, wg, wb, wo, fg)


def _alibi_slopes(n_heads):
    return [2.0 ** (-ALIBI_MAX_EXP * (h + 1) / n_heads) for h in range(n_heads)]


def _alibi_key_bias(seq_len, slopes_per_pair):
    assert seq_len <= 1 << 16
    lane_slope = np.zeros((1, BRANCH_WIDTH), np.float32)
    lane_is_high = np.zeros((1, BRANCH_WIDTH), bool)
    for p, pair in enumerate(slopes_per_pair):
        for c, slope in enumerate(pair):
            assert math.frexp(slope)[0] == 0.5, "ALiBi slopes are powers of two"
            lane_slope[0, p * LANES + 3 * c:p * LANES + 3 * c + 2] = slope
            lane_is_high[0, p * LANES + 3 * c] = True
    pos = jnp.arange(seq_len, dtype=jnp.int32)[:, None]
    piece = jnp.where(lane_is_high, pos - pos % 256, pos % 256).astype(F32)
    return (piece * lane_slope).astype(BF16)[None]


def _fox_bias_scatter():
    m = [[0.0] * BRANCH_WIDTH for _ in range(3 * LANES)]
    for x in range(3):
        for h in range(N_HEADS):
            m[x * LANES + h][(h // 2) * LANES + 3 * (h % 2) + x] = 1.0
    return jnp.asarray(m, dtype=BF16)


def kernel(x, norm_g, w_in, fox_fb, diff_lam, diff_norm_g, w_branch, w_out, final_g):
    batch, seq_len, _ = x.shape
    depth = norm_g.shape[0]
    n_tok = batch * seq_len
    scale = HEAD_DIM ** -0.5

    w = BRANCH_WIDTH
    off = {}
    pos = 0
    for name, width in (("diff_q", w), ("diff_k", w), ("diff_v", w), ("diff_z", w),
                        ("fox_q", w), ("fox_k", w), ("fox_v", w), ("fox_f", N_HEADS),
                        ("fox_z", w), ("dil_q", w), ("dil_k", w), ("dil_v", w),
                        ("dil_z", w), ("merge_g", 3 * D_MODEL)):
        off[name] = (pos, pos + width)
        pos += width

    def cols(wl, name, mult=1.0):
        a, b = off[name]
        return wl[:, a:b] * mult if mult != 1.0 else wl[:, a:b]

    ds = _alibi_slopes(DIFF_HEADS)
    cs = _alibi_slopes(N_HEADS)
    diff_kb = _alibi_key_bias(seq_len, [(ds[h], ds[h]) for h in range(DIFF_HEADS)])
    dil_kb = _alibi_key_bias(seq_len, [(cs[2 * p], cs[2 * p + 1]) for p in range(N_PAIRS)])
    pmat = _fox_bias_scatter()
    ones_g = jnp.ones((1, BRANCH_WIDTH), F32)
    zero_lam = jnp.zeros((4, LANES), F32)

    x2d = x.reshape(n_tok, D_MODEL)
    for l in range(depth):
        wl = w_in[l]
        w_qkv = jnp.concatenate(
            [cols(wl, "diff_q", scale), cols(wl, "diff_k"), cols(wl, "diff_v"),
             cols(wl, "fox_q", scale), cols(wl, "fox_k"), cols(wl, "fox_v"),
             cols(wl, "dil_q", scale), cols(wl, "dil_k"), cols(wl, "dil_v")],
            axis=1).astype(BF16)
        w_f = jnp.pad(cols(wl, "fox_f"), ((0, 0), (0, LANES - N_HEADS))).astype(BF16)
        fb = jnp.pad(fox_fb[l].astype(F32), (0, LANES - N_HEADS))[None]
        g = norm_g[l].astype(F32)[None]

        (dqT, dk, dvT, fqT, fk, fvT, fkb, cq, ck, cv) = _proj_call(
            x2d, g, w_qkv, w_f, fb, pmat, seq_len=seq_len)

        lam_init = 0.8 - 0.6 * math.exp(-0.3 * l)
        lam_pad = jnp.pad(diff_lam[l].astype(F32), ((0, 0), (0, LANES - HEAD_DIM)))
        oa = _pair_attn_call(dqT, dk, diff_kb, dvT, lam_pad,
                             diff_norm_g[l].astype(F32)[None],
                             batch=batch, seq_len=seq_len, diff=True, lam_init=lam_init)
        ob = _pair_attn_call(fqT, fk, fkb.reshape(batch, seq_len, BRANCH_WIDTH), fvT,
                             zero_lam, ones_g,
                             batch=batch, seq_len=seq_len, diff=False, lam_init=0.0)
        oc = _dil_call(cq, ck, cv, dil_kb, batch=batch, seq_len=seq_len)

        wz = jnp.concatenate([cols(wl, "diff_z"), cols(wl, "fox_z"), cols(wl, "dil_z")],
                             axis=1).astype(BF16)
        wg = cols(wl, "merge_g").astype(BF16)
        x2d = _merge_call(x2d, oa, ob, oc, g, wz, wg, w_branch[l].astype(BF16),
                          w_out[l].astype(BF16), final_g.astype(F32)[None],
                          last=(l == depth - 1))
    return x2d.reshape(batch, seq_len, D_MODEL)
```

```python
import functools
import math

import jax
import jax.numpy as jnp
import numpy as np
from jax import lax
from jax.experimental import pallas as pl
from jax.experimental.pallas import tpu as pltpu

F32 = jnp.float32
BF16 = jnp.bfloat16

D_MODEL = 1024
HEAD_DIM = 64
BRANCH_WIDTH = 512
LANES = 128
N_PAIRS = BRANCH_WIDTH // LANES
DIFF_HEADS = 4
N_HEADS = 8
DIL_PATTERNS = ((128, 1), (512, 4), (2048, 16))
DIL_BLOCK = 128
DIL_TQ = 256
DIL_GROUP = 2
RMS_EPS = 1e-6
LOG2E = math.log2(math.e)
ALIBI_MAX_EXP = 8.0
N_QKV_GROUPS = 9
TQ = 256
TK = 256
KV_SUBS = 4
TM_PROJ = 512
TM_MERGE = 512
VMEM_LIMIT = 56 * 1024 * 1024


def _split3(a):
    a1 = a.astype(BF16)
    r1 = a - a1.astype(F32)
    a2 = r1.astype(BF16)
    a3 = (r1 - a2.astype(F32)).astype(BF16)
    return a1, a2, a3


def _proj_kernel(x_ref, g_ref, w_ref, wf_ref, fb_ref, pmat_ref,
                 dqT_ref, dk_ref, dvT_ref, fqT_ref, fk_ref, fvT_ref, fkb_ref,
                 cq_ref, ck_ref, cv_ref, carry_ref, *, tiles_per_seq):
    i = pl.program_id(0)
    tm = x_ref.shape[0]
    x = x_ref[...]
    ms = jnp.mean(x * x, axis=-1, keepdims=True)
    h = (x * lax.rsqrt(ms + RMS_EPS) * g_ref[...]).astype(BF16)

    def proj(g):
        return jnp.dot(h, w_ref[:, g * BRANCH_WIDTH:(g + 1) * BRANCH_WIDTH],
                       preferred_element_type=F32)

    def put_t(ref, val):
        for t in range(tm // TQ):
            ref[t] = val[t * TQ:(t + 1) * TQ, :].T.astype(BF16)

    put_t(dqT_ref, proj(0))
    dk_ref[...] = proj(1).astype(BF16)
    put_t(dvT_ref, proj(2))
    put_t(fqT_ref, proj(3))
    fk_ref[...] = proj(4).astype(BF16)
    put_t(fvT_ref, proj(5))
    cq_ref[...] = proj(6).astype(BF16)
    ck_ref[...] = proj(7).astype(BF16)
    cv_ref[...] = proj(8).astype(BF16)

    f = jnp.dot(h, wf_ref[...], preferred_element_type=F32) + fb_ref[...]
    logf = jnp.minimum(f, 0.0) - jnp.log1p(jnp.exp(-jnp.abs(f)))

    @pl.when(i % tiles_per_seq == 0)
    def _():
        carry_ref[...] = jnp.zeros_like(carry_ref)

    row = lax.broadcasted_iota(jnp.int32, (tm, tm), 0)
    col = lax.broadcasted_iota(jnp.int32, (tm, tm), 1)
    tri = jnp.where(row >= col, 1.0, 0.0).astype(BF16)
    a1, a2, a3 = _split3(logf)
    c = (jnp.dot(tri, a1, preferred_element_type=F32)
         + jnp.dot(tri, a2, preferred_element_type=F32)
         + jnp.dot(tri, a3, preferred_element_type=F32)) + carry_ref[0:1, :]
    carry_ref[0:1, :] = c[tm - 1:tm, :]
    b1, b2, b3 = _split3(c * (-LOG2E))
    kb = jnp.dot(jnp.concatenate([b1, b2, b3], axis=1), pmat_ref[...],
                 preferred_element_type=F32)
    fkb_ref[...] = kb.astype(BF16)


def _proj_call(x2d, g, w_qkv, w_f, fb, pmat, *, seq_len):
    n_tok = x2d.shape[0]
    tm = TM_PROJ
    nt = n_tok // tm
    const = lambda i: (0, 0)
    tok = pl.BlockSpec((tm, BRANCH_WIDTH), lambda i: (i, 0))
    tr = pl.BlockSpec((tm // TQ, BRANCH_WIDTH, TQ), lambda i: (i, 0, 0))
    nat_shape = jax.ShapeDtypeStruct((n_tok, BRANCH_WIDTH), BF16)
    tr_shape = jax.ShapeDtypeStruct((n_tok // TQ, BRANCH_WIDTH, TQ), BF16)
    return pl.pallas_call(
        functools.partial(_proj_kernel, tiles_per_seq=seq_len // tm),
        grid=(nt,),
        in_specs=[
            pl.BlockSpec((tm, D_MODEL), lambda i: (i, 0)),
            pl.BlockSpec((1, D_MODEL), const),
            pl.BlockSpec((D_MODEL, N_QKV_GROUPS * BRANCH_WIDTH), const),
            pl.BlockSpec((D_MODEL, LANES), const),
            pl.BlockSpec((1, LANES), const),
            pl.BlockSpec((3 * LANES, BRANCH_WIDTH), const),
        ],
        out_specs=[tr, tok, tr, tr, tok, tr, tok, tok, tok, tok],
        out_shape=[tr_shape, nat_shape, tr_shape, tr_shape, nat_shape, tr_shape,
                   nat_shape, nat_shape, nat_shape, nat_shape],
        scratch_shapes=[pltpu.VMEM((8, LANES), F32)],
        compiler_params=pltpu.CompilerParams(
            dimension_semantics=("arbitrary",), vmem_limit_bytes=VMEM_LIMIT),
        name="proj",
    )(x2d, g, w_qkv, w_f, fb, pmat)


def _pair_attn_kernel(qT_ref, qTn_ref, k_ref, kb_ref, vT_ref, lam_ref, g_ref, o_ref,
                      qa_sc, m_sc, l_sc, acc_sc, s_sc, t0_sc, smax_sc, t0max_sc,
                      *, diff, lam_init, max_full):
    i = pl.program_id(2)
    par = i % 2
    row = lax.broadcasted_iota(jnp.int32, (LANES, TQ), 0)

    def build_qa(q_ref, slot):
        qT = q_ref[0, 0].astype(F32)
        qa_sc[slot, :, 0:TQ] = jnp.concatenate(
            [jnp.where(row < HEAD_DIM, qT, 0.0), jnp.where(row < 3, 1.0, 0.0)],
            axis=0).astype(BF16)
        qa_sc[slot, :, TQ:2 * TQ] = jnp.concatenate(
            [jnp.where(row >= HEAD_DIM, qT, 0.0),
             jnp.where((row >= 3) & (row < 6), 1.0, 0.0)], axis=0).astype(BF16)

    tile_keys = KV_SUBS * TK

    def score_tile(t, qa_slot, dst, dst_max, dst_slot):
        base = pl.multiple_of(t * tile_keys, tile_keys)
        ka = jnp.concatenate([k_ref[0, pl.ds(base, tile_keys), :],
                              kb_ref[0, pl.ds(base, tile_keys), :]], axis=1)
        s = jnp.dot(ka, qa_sc[qa_slot], preferred_element_type=F32)
        dst[dst_slot] = s
        dst_max[dst_slot] = jnp.max(s, axis=0, keepdims=True)

    @pl.when(i == 0)
    def _():
        build_qa(qT_ref, 0)
        score_tile(0, 0, t0_sc, t0max_sc, 0)

    build_qa(qTn_ref, 1 - par)
    m_sc[...] = jnp.full(m_sc.shape, -jnp.inf, F32)
    l_sc[...] = jnp.zeros(l_sc.shape, F32)
    acc_sc[...] = jnp.zeros(acc_sc.shape, F32)

    def consume(s, sub0, n_sub, tile_max):
        rows = n_sub * TK
        if tile_max is None:
            kr = lax.broadcasted_iota(jnp.int32, (TK, 2 * TQ), 0)
            qc = lax.broadcasted_iota(jnp.int32, (TK, 2 * TQ), 1) % TQ
            tail = jnp.where(kr <= qc, s[rows - TK:], -jnp.inf)
            s = tail if n_sub == 1 else jnp.concatenate([s[:rows - TK], tail], axis=0)
            tile_max = jnp.max(s, axis=0, keepdims=True)
        m_old = m_sc[...]
        m_new = jnp.maximum(m_old, tile_max)
        alpha = jnp.exp2(m_old - m_new)
        p = jnp.exp2(s - m_new)
        l_sc[...] = alpha * l_sc[...] + jnp.sum(p, axis=0, keepdims=True)
        pb = p.astype(BF16)
        pv = jnp.dot(vT_ref[0, sub0], pb[0:TK], preferred_element_type=F32)
        for t in range(1, n_sub):
            pv = pv + jnp.dot(vT_ref[0, sub0 + t], pb[t * TK:(t + 1) * TK],
                              preferred_element_type=F32)
        acc_sc[...] = alpha * acc_sc[...] + pv
        m_sc[...] = m_new

    def full_block(cur, cur_max, cur_slot, sub0, t_next):
        m_old = m_sc[...]
        m_new = jnp.maximum(m_old, cur_max[cur_slot])
        alpha = jnp.exp2(m_old - m_new)
        nslot = t_next % 2
        lsum = pv = nmax = None
        for t in range(KV_SUBS):
            kbase = pl.multiple_of(t_next * tile_keys + t * TK, TK)
            ka = jnp.concatenate([k_ref[0, pl.ds(kbase, TK), :],
                                  kb_ref[0, pl.ds(kbase, TK), :]], axis=1)
            sn = jnp.dot(ka, qa_sc[par], preferred_element_type=F32)
            s_sc[nslot, t * TK:(t + 1) * TK, :] = sn
            cm = jnp.max(sn, axis=0, keepdims=True)
            nmax = cm if nmax is None else jnp.maximum(nmax, cm)
            p = jnp.exp2(cur[cur_slot, t * TK:(t + 1) * TK, :] - m_new)
            ls = jnp.sum(p, axis=0, keepdims=True)
            lsum = ls if lsum is None else lsum + ls
            d = jnp.dot(vT_ref[0, sub0 + t], p.astype(BF16), preferred_element_type=F32)
            pv = d if pv is None else pv + d
        smax_sc[nslot] = nmax
        l_sc[...] = alpha * l_sc[...] + lsum
        acc_sc[...] = alpha * acc_sc[...] + pv
        m_sc[...] = m_new

    n_full = i // KV_SUBS
    rem = i % KV_SUBS
    for j in range(max_full):
        @pl.when(j < n_full)
        def _(j=j):
            if j == 0:
                full_block(t0_sc, t0max_sc, par, 0, j + 1)
            else:
                full_block(s_sc, smax_sc, j % 2, j * KV_SUBS, j + 1)
    for r in range(KV_SUBS):
        for diag_is_tile0 in (True, False):
            @pl.when((rem == r) & ((n_full == 0) == diag_is_tile0))
            def _(r=r, diag_is_tile0=diag_is_tile0):
                score_tile(0, 1 - par, t0_sc, t0max_sc, (r + 1) % 2)
                rows = (r + 1) * TK
                s = t0_sc[r % 2, 0:rows, :] if diag_is_tile0 else s_sc[n_full % 2, 0:rows, :]
                consume(s, n_full * KV_SUBS, r + 1, None)

    a = acc_sc[...] / l_sc[...]
    a0 = a[:, 0:TQ]
    a1 = a[:, TQ:2 * TQ]
    if diff:
        dl = lam_ref[...]
        lam = (jnp.exp(jnp.sum(dl[0:1] * dl[1:2], axis=1, keepdims=True))
               - jnp.exp(jnp.sum(dl[2:3] * dl[3:4], axis=1, keepdims=True)) + lam_init)
        oT = a0 - lam * a1
        ms = jnp.mean(oT * oT, axis=0, keepdims=True)
        oT = oT * lax.rsqrt(ms + RMS_EPS)
        o_ref[0] = (oT.T * g_ref[...]) * (1.0 - lam_init)
    else:
        o_ref[0] = jnp.where(row < HEAD_DIM, a0, a1).T


def _pair_attn_call(qT, k, kb, vT, lam_pad, g, *, batch, seq_len, diff, lam_init):
    nq = seq_len // TQ
    nk = seq_len // TK
    qT = qT.reshape(batch, nq, BRANCH_WIDTH, TQ)
    vT = vT.reshape(batch, nk, BRANCH_WIDTH, TK)
    k = k.reshape(batch, seq_len, BRANCH_WIDTH)
    kb_batched = kb.shape[0] != 1
    kb_map = (lambda b, p, i: (b, 0, p)) if kb_batched else (lambda b, p, i: (0, 0, p))
    return pl.pallas_call(
        functools.partial(_pair_attn_kernel, diff=diff, lam_init=lam_init,
                          max_full=seq_len // (KV_SUBS * TK) - 1),
        grid=(batch, N_PAIRS, nq),
        in_specs=[
            pl.BlockSpec((1, 1, LANES, TQ), lambda b, p, i: (b, i, p, 0)),
            pl.BlockSpec((1, 1, LANES, TQ),
                         lambda b, p, i: (b, jnp.minimum(i + 1, nq - 1), p, 0)),
            pl.BlockSpec((1, seq_len, LANES), lambda b, p, i: (b, 0, p)),
            pl.BlockSpec((1, seq_len, LANES), kb_map),
            pl.BlockSpec((1, nk, LANES, TK), lambda b, p, i: (b, 0, p, 0)),
            pl.BlockSpec((4, LANES), lambda b, p, i: (0, 0)),
            pl.BlockSpec((1, LANES), lambda b, p, i: (0, p)),
        ],
        out_specs=pl.BlockSpec((1, TQ, LANES), lambda b, p, i: (b, i, p)),
        out_shape=jax.ShapeDtypeStruct((batch, seq_len, BRANCH_WIDTH), F32),
        scratch_shapes=[
            pltpu.VMEM((2, 2 * LANES, 2 * TQ), BF16),
            pltpu.VMEM((1, 2 * TQ), F32),
            pltpu.VMEM((1, 2 * TQ), F32),
            pltpu.VMEM((LANES, 2 * TQ), F32),
            pltpu.VMEM((2, KV_SUBS * TK, 2 * TQ), F32),
            pltpu.VMEM((2, KV_SUBS * TK, 2 * TQ), F32),
            pltpu.VMEM((2, 1, 2 * TQ), F32),
            pltpu.VMEM((2, 1, 2 * TQ), F32),
        ],
        compiler_params=pltpu.CompilerParams(
            dimension_semantics=("arbitrary", "arbitrary", "arbitrary"),
            vmem_limit_bytes=VMEM_LIMIT),
        name="diff_attn" if diff else "fox_attn",
    )(qT, qT, k, kb, vT, lam_pad, g)


def _dil_kernel(q_ref, k_ref, v_ref, kb_ref, o_ref,
                qf, kf, vf, kbf, acc, m_sc, l_sc, mask_sc, *, seq_len):
    L = DIL_BLOCK
    T = DIL_TQ
    KEYS = T + L
    qf[...] = q_ref[0].astype(F32)
    kf[...] = k_ref[0].astype(F32)
    vf[...] = v_ref[0].astype(F32)
    kbf[...] = kb_ref[0].astype(F32)
    acc[...] = jnp.zeros(acc.shape, F32)
    m_sc[...] = jnp.full(m_sc.shape, -jnp.inf, F32)
    l_sc[...] = jnp.zeros(l_sc.shape, F32)

    kr = lax.broadcasted_iota(jnp.int32, (2 * L, 2 * L), 0)
    qc = lax.broadcasted_iota(jnp.int32, (2 * L, 2 * L), 1) % L
    band = (kr >= qc) & (kr <= qc + L)
    mask_sc[0] = jnp.where(band & (kr >= L), 0.0, -jnp.inf)
    mask_sc[1] = jnp.where(band, 0.0, -jnp.inf)

    lane = lax.broadcasted_iota(jnp.int32, (T, LANES), 1)
    first = lane < HEAD_DIM
    sel0 = jnp.where(lane < 3, 1.0, 0.0)
    sel1 = jnp.where((lane >= 3) & (lane < 6), 1.0, 0.0)
    frow = lax.broadcasted_iota(jnp.int32, (LANES, L), 0) < HEAD_DIM
    n_qb = T // L

    for _, d in DIL_PATTERNS:
        nt = seq_len // (T * d)

        def group(u, carry, d=d, nt=nt):
            rows, prows, slot = [], [], []
            for g in range(DIL_GROUP):
                t = u * DIL_GROUP + g
                r = t // nt
                it = t % nt
                start = r + d * T * it
                prev = jnp.where(it == 0, start, start - d * L)
                rows.append(pl.ds(start, T, stride=d))
                prows.append(pl.ds(prev, L, stride=d))
                slot.append(jnp.minimum(it, 1))
            m_old = [m_sc[rows[g], :] for g in range(DIL_GROUP)]
            l_old = [l_sc[rows[g], :] for g in range(DIL_GROUP)]
            a_old = [acc[rows[g], :] for g in range(DIL_GROUP)]
            s, vT = [], []
            for g in range(DIL_GROUP):
                qb = qf[rows[g], :]
                q0 = jnp.concatenate([jnp.where(first, qb, 0.0), sel0], axis=1).astype(BF16)
                q1 = jnp.concatenate([jnp.where(first, 0.0, qb), sel1], axis=1).astype(BF16)
                ka = jnp.concatenate(
                    [jnp.concatenate([kf[prows[g], :], kbf[prows[g], :]], axis=1),
                     jnp.concatenate([kf[rows[g], :], kbf[rows[g], :]], axis=1)],
                    axis=0).astype(BF16)
                sg = []
                for b in range(n_qb):
                    qa = jnp.concatenate([q0[b * L:(b + 1) * L], q1[b * L:(b + 1) * L]], axis=0)
                    sb = lax.dot_general(ka[b * L:(b + 2) * L], qa, (((1,), (1,)), ((), ())),
                                         preferred_element_type=F32)
                    sg.append(sb + mask_sc[slot[g] if b == 0 else 1])
                s.append(sg)
                vv = jnp.concatenate([vf[prows[g], :], vf[rows[g], :]], axis=0)
                vT.append(vv.T.astype(BF16))
            num, m_blk, l_blk = [], [], []
            for g in range(DIL_GROUP):
                nb_, mb_, lb_ = [], [], []
                for b in range(n_qb):
                    mg = jnp.max(s[g][b], axis=0, keepdims=True)
                    p = jnp.exp2(s[g][b] - mg)
                    lg = jnp.sum(p, axis=0, keepdims=True)
                    oT = jnp.dot(vT[g][:, b * L:(b + 2) * L], p.astype(BF16),
                                 preferred_element_type=F32)
                    nb_.append(jnp.where(frow, oT[:, :L], oT[:, L:]))
                    mb_.append(jnp.where(frow, jnp.broadcast_to(mg[:, :L], (LANES, L)),
                                         jnp.broadcast_to(mg[:, L:], (LANES, L))))
                    lb_.append(jnp.where(frow, jnp.broadcast_to(lg[:, :L], (LANES, L)),
                                         jnp.broadcast_to(lg[:, L:], (LANES, L))))
                num.append(jnp.concatenate(nb_, axis=1).T)
                m_blk.append(jnp.concatenate(mb_, axis=1).T)
                l_blk.append(jnp.concatenate(lb_, axis=1).T)
            for g in range(DIL_GROUP):
                m_new = jnp.maximum(m_old[g], m_blk[g])
                w_old = jnp.exp2(m_old[g] - m_new)
                w_blk = jnp.exp2(m_blk[g] - m_new)
                l_sc[rows[g], :] = w_old * l_old[g] + w_blk * l_blk[g]
                acc[rows[g], :] = w_old * a_old[g] + w_blk * num[g]
                m_sc[rows[g], :] = m_new
            return carry

        lax.fori_loop(0, seq_len // (T * DIL_GROUP), group, 0)

    o_ref[0] = acc[...] / l_sc[...]


def _dil_call(q, k, v, kb, *, batch, seq_len):
    q = q.reshape(batch, seq_len, BRANCH_WIDTH)
    k = k.reshape(batch, seq_len, BRANCH_WIDTH)
    v = v.reshape(batch, seq_len, BRANCH_WIDTH)
    blk = pl.BlockSpec((1, seq_len, LANES), lambda b, p: (b, 0, p))
    return pl.pallas_call(
        functools.partial(_dil_kernel, seq_len=seq_len),
        grid=(batch, N_PAIRS),
        in_specs=[blk, blk, blk, pl.BlockSpec((1, seq_len, LANES), lambda b, p: (0, 0, p))],
        out_specs=blk,
        out_shape=jax.ShapeDtypeStruct((batch, seq_len, BRANCH_WIDTH), F32),
        scratch_shapes=[pltpu.VMEM((seq_len, LANES), F32) for _ in range(7)]
        + [pltpu.VMEM((2, 2 * DIL_BLOCK, 2 * DIL_BLOCK), F32)],
        compiler_params=pltpu.CompilerParams(
            dimension_semantics=("arbitrary", "arbitrary"), vmem_limit_bytes=VMEM_LIMIT),
        name="dil_attn",
    )(q, k, v, kb)


def _merge_kernel(x_ref, oa_ref, ob_ref, oc_ref, g_ref, wz_ref, wg_ref, wb_ref, wo_ref,
                  fg_ref, out_ref, *, last):
    x = x_ref[...]
    ms = jnp.mean(x * x, axis=-1, keepdims=True)
    h = (x * lax.rsqrt(ms + RMS_EPS) * g_ref[...]).astype(BF16)
    merged = None
    for n, o_ref in enumerate((oa_ref, ob_ref, oc_ref)):
        z = jnp.dot(h, wz_ref[:, n * BRANCH_WIDTH:(n + 1) * BRANCH_WIDTH],
                    preferred_element_type=F32)
        y = (o_ref[...] * (z * jax.nn.sigmoid(z))).astype(BF16)
        br = jnp.dot(y, wb_ref[n], preferred_element_type=F32)
        gate = jax.nn.sigmoid(jnp.dot(h, wg_ref[:, n * D_MODEL:(n + 1) * D_MODEL],
                                      preferred_element_type=F32))
        merged = gate * br if merged is None else merged + gate * br
    out = x + jnp.dot(merged.astype(BF16), wo_ref[...], preferred_element_type=F32)
    if last:
        ms = jnp.mean(out * out, axis=-1, keepdims=True)
        out = out * lax.rsqrt(ms + RMS_EPS) * fg_ref[...]
    out_ref[...] = out


def _merge_call(x2d, oa, ob, oc, g, wz, wg, wb, wo, fg, *, last):
    n_tok = x2d.shape[0]
    tm = TM_MERGE
    const2 = lambda i: (0, 0)
    tok = pl.BlockSpec((tm, BRANCH_WIDTH), lambda i: (i, 0))
    return pl.pallas_call(
        functools.partial(_merge_kernel, last=last),
        grid=(n_tok // tm,),
        in_specs=[
            pl.BlockSpec((tm, D_MODEL), lambda i: (i, 0)), tok, tok, tok,
            pl.BlockSpec((1, D_MODEL), const2),
            pl.BlockSpec((D_MODEL, 3 * BRANCH_WIDTH), const2),
            pl.BlockSpec((D_MODEL, 3 * D_MODEL), const2),
            pl.BlockSpec((3, BRANCH_WIDTH, D_MODEL), lambda i: (0, 0, 0)),
            pl.BlockSpec((D_MODEL, D_MODEL), const2),
            pl.BlockSpec((1, D_MODEL), const2),
        ],
        out_specs=pl.BlockSpec((tm, D_MODEL), lambda i: (i, 0)),
        out_shape=jax.ShapeDtypeStruct((n_tok, D_MODEL), F32),
        compiler_params=pltpu.CompilerParams(
            dimension_semantics=("arbitrary",), vmem_limit_bytes=VMEM_LIMIT),
        name="merge",
    )(x2d, oa.reshape(n_tok, BRANCH_WIDTH), ob.reshape(n_tok, BRANCH_WIDTH),
      oc.reshape(n_tok, BRANCH_WIDTH), g, wz, wg, wb, wo, fg)


def _alibi_slopes(n_heads):
    return [2.0 ** (-ALIBI_MAX_EXP * (h + 1) / n_heads) for h in range(n_heads)]


def _alibi_key_bias(seq_len, slopes_per_pair):
    pos = np.arange(seq_len, dtype=np.float64)
    out = np.zeros((seq_len, BRANCH_WIDTH), dtype=BF16)
    for p, pair in enumerate(slopes_per_pair):
        for c, slope in enumerate(pair):
            rest = pos * (slope * LOG2E)
            for x in range(3):
                piece = rest.astype(BF16)
                out[:, p * LANES + 3 * c + x] = piece
                rest = rest - piece.astype(np.float64)
    return jnp.asarray(out)[None]


def _fox_bias_scatter():
    m = [[0.0] * BRANCH_WIDTH for _ in range(3 * LANES)]
    for x in range(3):
        for h in range(N_HEADS):
            m[x * LANES + h][(h // 2) * LANES + 3 * (h % 2) + x] = 1.0
    return jnp.asarray(m, dtype=BF16)


def kernel(x, norm_g, w_in, fox_fb, diff_lam, diff_norm_g, w_branch, w_out, final_g):
    batch, seq_len, _ = x.shape
    depth = norm_g.shape[0]
    n_tok = batch * seq_len
    scale = HEAD_DIM ** -0.5 * LOG2E

    w = BRANCH_WIDTH
    off = {}
    pos = 0
    for name, width in (("diff_q", w), ("diff_k", w), ("diff_v", w), ("diff_z", w),
                        ("fox_q", w), ("fox_k", w), ("fox_v", w), ("fox_f", N_HEADS),
                        ("fox_z", w), ("dil_q", w), ("dil_k", w), ("dil_v", w),
                        ("dil_z", w), ("merge_g", 3 * D_MODEL)):
        off[name] = (pos, pos + width)
        pos += width

    def cols(wl, name, mult=1.0):
        a, b = off[name]
        return wl[:, a:b] * mult if mult != 1.0 else wl[:, a:b]

    ds = _alibi_slopes(DIFF_HEADS)
    cs = _alibi_slopes(N_HEADS)
    diff_kb = _alibi_key_bias(seq_len, [(ds[h], ds[h]) for h in range(DIFF_HEADS)])
    dil_kb = _alibi_key_bias(seq_len, [(cs[2 * p], cs[2 * p + 1]) for p in range(N_PAIRS)])
    pmat = _fox_bias_scatter()
    ones_g = jnp.ones((1, BRANCH_WIDTH), F32)
    zero_lam = jnp.zeros((4, LANES), F32)

    x2d = x.reshape(n_tok, D_MODEL)
    for l in range(depth):
        wl = w_in[l]
        w_qkv = jnp.concatenate(
            [cols(wl, "diff_q", scale), cols(wl, "diff_k"), cols(wl, "diff_v"),
             cols(wl, "fox_q", scale), cols(wl, "fox_k"), cols(wl, "fox_v"),
             cols(wl, "dil_q", scale), cols(wl, "dil_k"), cols(wl, "dil_v")],
            axis=1).astype(BF16)
        w_f = jnp.pad(cols(wl, "fox_f"), ((0, 0), (0, LANES - N_HEADS))).astype(BF16)
        fb = jnp.pad(fox_fb[l].astype(F32), (0, LANES - N_HEADS))[None]
        g = norm_g[l].astype(F32)[None]

        (dqT, dk, dvT, fqT, fk, fvT, fkb, cq, ck, cv) = _proj_call(
            x2d, g, w_qkv, w_f, fb, pmat, seq_len=seq_len)

        lam_init = 0.8 - 0.6 * math.exp(-0.3 * l)
        lam_pad = jnp.pad(diff_lam[l].astype(F32), ((0, 0), (0, LANES - HEAD_DIM)))
        oa = _pair_attn_call(dqT, dk, diff_kb, dvT, lam_pad,
                             diff_norm_g[l].astype(F32)[None],
                             batch=batch, seq_len=seq_len, diff=True, lam_init=lam_init)
        ob = _pair_attn_call(fqT, fk, fkb.reshape(batch, seq_len, BRANCH_WIDTH), fvT,
                             zero_lam, ones_g,
                             batch=batch, seq_len=seq_len, diff=False, lam_init=0.0)
        oc = _dil_call(cq, ck, cv, dil_kb, batch=batch, seq_len=seq_len)

        wz = jnp.concatenate([cols(wl, "diff_z"), cols(wl, "fox_z"), cols(wl, "dil_z")],
                             axis=1).astype(BF16)
        wg = cols(wl, "merge_g").astype(BF16)
        x2d = _merge_call(x2d, oa, ob, oc, g, wz, wg, w_branch[l].astype(BF16),
                          w_out[l].astype(BF16), final_g.astype(F32)[None],
                          last=(l == depth - 1))
    return x2d.reshape(batch, seq_len, D_MODEL)
```

```python
import functools
import math

import jax
import jax.numpy as jnp
import numpy as np
from jax import lax
from jax.experimental import pallas as pl
from jax.experimental.pallas import tpu as pltpu

F32 = jnp.float32
BF16 = jnp.bfloat16

D_MODEL = 1024
HEAD_DIM = 64
BRANCH_WIDTH = 512
LANES = 128
N_PAIRS = BRANCH_WIDTH // LANES
DIFF_HEADS = 4
N_HEADS = 8
DIL_PATTERNS = ((128, 1), (512, 4), (2048, 16))
DIL_BLOCK = 128
DIL_TQ = 256
DIL_GROUP = 2
RMS_EPS = 1e-6
LOG2E = math.log2(math.e)
ALIBI_MAX_EXP = 8.0
N_QKV_GROUPS = 9
TQ = 512
TK = 512
KV_SUBS = 2
TM_PROJ = 512
TM_MERGE = 512
VMEM_LIMIT = 56 * 1024 * 1024


def _split3(a):
    a1 = a.astype(BF16)
    r1 = a - a1.astype(F32)
    a2 = r1.astype(BF16)
    a3 = (r1 - a2.astype(F32)).astype(BF16)
    return a1, a2, a3


def _proj_kernel(x_ref, g_ref, w_ref, wf_ref, fb_ref, pmat_ref,
                 dqT_ref, dk_ref, dvT_ref, fqT_ref, fk_ref, fvT_ref, fkb_ref,
                 cq_ref, ck_ref, cv_ref, carry_ref, *, tiles_per_seq):
    i = pl.program_id(0)
    tm = x_ref.shape[0]
    x = x_ref[...]
    ms = jnp.mean(x * x, axis=-1, keepdims=True)
    h = (x * lax.rsqrt(ms + RMS_EPS) * g_ref[...]).astype(BF16)

    def proj(g):
        return jnp.dot(h, w_ref[:, g * BRANCH_WIDTH:(g + 1) * BRANCH_WIDTH],
                       preferred_element_type=F32)

    def put_t(ref, val):
        for t in range(tm // TQ):
            ref[t] = val[t * TQ:(t + 1) * TQ, :].T.astype(BF16)

    put_t(dqT_ref, proj(0))
    dk_ref[...] = proj(1).astype(BF16)
    put_t(dvT_ref, proj(2))
    put_t(fqT_ref, proj(3))
    fk_ref[...] = proj(4).astype(BF16)
    put_t(fvT_ref, proj(5))
    cq_ref[...] = proj(6).astype(BF16)
    ck_ref[...] = proj(7).astype(BF16)
    cv_ref[...] = proj(8).astype(BF16)

    f = jnp.dot(h, wf_ref[...], preferred_element_type=F32) + fb_ref[...]
    logf = jnp.minimum(f, 0.0) - jnp.log1p(jnp.exp(-jnp.abs(f)))

    @pl.when(i % tiles_per_seq == 0)
    def _():
        carry_ref[...] = jnp.zeros_like(carry_ref)

    row = lax.broadcasted_iota(jnp.int32, (tm, tm), 0)
    col = lax.broadcasted_iota(jnp.int32, (tm, tm), 1)
    tri = jnp.where(row >= col, 1.0, 0.0).astype(BF16)
    a1, a2, a3 = _split3(logf)
    c = (jnp.dot(tri, a1, preferred_element_type=F32)
         + jnp.dot(tri, a2, preferred_element_type=F32)
         + jnp.dot(tri, a3, preferred_element_type=F32)) + carry_ref[0:1, :]
    carry_ref[0:1, :] = c[tm - 1:tm, :]
    b1, b2, b3 = _split3(c * (-LOG2E))
    kb = jnp.dot(jnp.concatenate([b1, b2, b3], axis=1), pmat_ref[...],
                 preferred_element_type=F32)
    fkb_ref[...] = kb.astype(BF16)


def _proj_call(x2d, g, w_qkv, w_f, fb, pmat, *, seq_len):
    n_tok = x2d.shape[0]
    tm = TM_PROJ
    nt = n_tok // tm
    const = lambda i: (0, 0)
    tok = pl.BlockSpec((tm, BRANCH_WIDTH), lambda i: (i, 0))
    tr = pl.BlockSpec((tm // TQ, BRANCH_WIDTH, TQ), lambda i: (i, 0, 0))
    nat_shape = jax.ShapeDtypeStruct((n_tok, BRANCH_WIDTH), BF16)
    tr_shape = jax.ShapeDtypeStruct((n_tok // TQ, BRANCH_WIDTH, TQ), BF16)
    return pl.pallas_call(
        functools.partial(_proj_kernel, tiles_per_seq=seq_len // tm),
        grid=(nt,),
        in_specs=[
            pl.BlockSpec((tm, D_MODEL), lambda i: (i, 0)),
            pl.BlockSpec((1, D_MODEL), const),
            pl.BlockSpec((D_MODEL, N_QKV_GROUPS * BRANCH_WIDTH), const),
            pl.BlockSpec((D_MODEL, LANES), const),
            pl.BlockSpec((1, LANES), const),
            pl.BlockSpec((3 * LANES, BRANCH_WIDTH), const),
        ],
        out_specs=[tr, tok, tr, tr, tok, tr, tok, tok, tok, tok],
        out_shape=[tr_shape, nat_shape, tr_shape, tr_shape, nat_shape, tr_shape,
                   nat_shape, nat_shape, nat_shape, nat_shape],
        scratch_shapes=[pltpu.VMEM((8, LANES), F32)],
        compiler_params=pltpu.CompilerParams(
            dimension_semantics=("arbitrary",), vmem_limit_bytes=VMEM_LIMIT),
        name="proj",
    )(x2d, g, w_qkv, w_f, fb, pmat)


def _pair_attn_kernel(qT_ref, qTn_ref, k_ref, kb_ref, vT_ref, lam_ref, g_ref, o_ref,
                      qa_sc, m_sc, l_sc, acc_sc, s_sc, t0_sc, smax_sc, t0max_sc,
                      *, diff, lam_init, max_full):
    i = pl.program_id(2)
    par = i % 2
    row = lax.broadcasted_iota(jnp.int32, (LANES, TQ), 0)

    def build_qa(q_ref, slot):
        qT = q_ref[0, 0].astype(F32)
        qa_sc[slot, :, 0:TQ] = jnp.concatenate(
            [jnp.where(row < HEAD_DIM, qT, 0.0), jnp.where(row < 3, 1.0, 0.0)],
            axis=0).astype(BF16)
        qa_sc[slot, :, TQ:2 * TQ] = jnp.concatenate(
            [jnp.where(row >= HEAD_DIM, qT, 0.0),
             jnp.where((row >= 3) & (row < 6), 1.0, 0.0)], axis=0).astype(BF16)

    tile_keys = KV_SUBS * TK

    def score_tile(t, qa_slot, dst, dst_max, dst_slot):
        base = pl.multiple_of(t * tile_keys, tile_keys)
        ka = jnp.concatenate([k_ref[0, pl.ds(base, tile_keys), :],
                              kb_ref[0, pl.ds(base, tile_keys), :]], axis=1)
        s = jnp.dot(ka, qa_sc[qa_slot], preferred_element_type=F32)
        dst[dst_slot] = s
        dst_max[dst_slot] = jnp.max(s, axis=0, keepdims=True)

    @pl.when(i == 0)
    def _():
        build_qa(qT_ref, 0)
        score_tile(0, 0, t0_sc, t0max_sc, 0)

    build_qa(qTn_ref, 1 - par)
    m_sc[...] = jnp.full(m_sc.shape, -jnp.inf, F32)
    l_sc[...] = jnp.zeros(l_sc.shape, F32)
    acc_sc[...] = jnp.zeros(acc_sc.shape, F32)

    def consume(s, sub0, n_sub, tile_max):
        rows = n_sub * TK
        if tile_max is None:
            kr = lax.broadcasted_iota(jnp.int32, (TK, 2 * TQ), 0)
            qc = lax.broadcasted_iota(jnp.int32, (TK, 2 * TQ), 1) % TQ
            tail = jnp.where(kr <= qc, s[rows - TK:], -jnp.inf)
            s = tail if n_sub == 1 else jnp.concatenate([s[:rows - TK], tail], axis=0)
            tile_max = jnp.max(s, axis=0, keepdims=True)
        m_old = m_sc[...]
        m_new = jnp.maximum(m_old, tile_max)
        alpha = jnp.exp2(m_old - m_new)
        p = jnp.exp2(s - m_new)
        l_sc[...] = alpha * l_sc[...] + jnp.sum(p, axis=0, keepdims=True)
        pb = p.astype(BF16)
        pv = jnp.dot(vT_ref[0, sub0], pb[0:TK], preferred_element_type=F32)
        for t in range(1, n_sub):
            pv = pv + jnp.dot(vT_ref[0, sub0 + t], pb[t * TK:(t + 1) * TK],
                              preferred_element_type=F32)
        acc_sc[...] = alpha * acc_sc[...] + pv
        m_sc[...] = m_new

    def full_block(cur, cur_max, cur_slot, sub0, t_next):
        m_old = m_sc[...]
        m_new = jnp.maximum(m_old, cur_max[cur_slot])
        alpha = jnp.exp2(m_old - m_new)
        nslot = t_next % 2
        lsum = pv = nmax = None
        for t in range(KV_SUBS):
            kbase = pl.multiple_of(t_next * tile_keys + t * TK, TK)
            ka = jnp.concatenate([k_ref[0, pl.ds(kbase, TK), :],
                                  kb_ref[0, pl.ds(kbase, TK), :]], axis=1)
            sn = jnp.dot(ka, qa_sc[par], preferred_element_type=F32)
            s_sc[nslot, t * TK:(t + 1) * TK, :] = sn
            cm = jnp.max(sn, axis=0, keepdims=True)
            nmax = cm if nmax is None else jnp.maximum(nmax, cm)
            p = jnp.exp2(cur[cur_slot, t * TK:(t + 1) * TK, :] - m_new)
            ls = jnp.sum(p, axis=0, keepdims=True)
            lsum = ls if lsum is None else lsum + ls
            d = jnp.dot(vT_ref[0, sub0 + t], p.astype(BF16), preferred_element_type=F32)
            pv = d if pv is None else pv + d
        smax_sc[nslot] = nmax
        l_sc[...] = alpha * l_sc[...] + lsum
        acc_sc[...] = alpha * acc_sc[...] + pv
        m_sc[...] = m_new

    n_full = i // KV_SUBS
    rem = i % KV_SUBS
    for j in range(max_full):
        @pl.when(j < n_full)
        def _(j=j):
            if j == 0:
                full_block(t0_sc, t0max_sc, par, 0, j + 1)
            else:
                full_block(s_sc, smax_sc, j % 2, j * KV_SUBS, j + 1)
    for r in range(KV_SUBS):
        for diag_is_tile0 in (True, False):
            @pl.when((rem == r) & ((n_full == 0) == diag_is_tile0))
            def _(r=r, diag_is_tile0=diag_is_tile0):
                score_tile(0, 1 - par, t0_sc, t0max_sc, (r + 1) % 2)
                rows = (r + 1) * TK
                s = t0_sc[r % 2, 0:rows, :] if diag_is_tile0 else s_sc[n_full % 2, 0:rows, :]
                consume(s, n_full * KV_SUBS, r + 1, None)

    a = acc_sc[...] / l_sc[...]
    a0 = a[:, 0:TQ]
    a1 = a[:, TQ:2 * TQ]
    if diff:
        dl = lam_ref[...]
        lam = (jnp.exp(jnp.sum(dl[0:1] * dl[1:2], axis=1, keepdims=True))
               - jnp.exp(jnp.sum(dl[2:3] * dl[3:4], axis=1, keepdims=True)) + lam_init)
        oT = a0 - lam * a1
        ms = jnp.mean(oT * oT, axis=0, keepdims=True)
        oT = oT * lax.rsqrt(ms + RMS_EPS)
        o_ref[0] = (oT.T * g_ref[...]) * (1.0 - lam_init)
    else:
        o_ref[0] = jnp.where(row < HEAD_DIM, a0, a1).T


def _pair_attn_call(qT, k, kb, vT, lam_pad, g, *, batch, seq_len, diff, lam_init):
    nq = seq_len // TQ
    nk = seq_len // TK
    qT = qT.reshape(batch, nq, BRANCH_WIDTH, TQ)
    vT = vT.reshape(batch, nk, BRANCH_WIDTH, TK)
    k = k.reshape(batch, seq_len, BRANCH_WIDTH)
    kb_batched = kb.shape[0] != 1
    kb_map = (lambda b, p, i: (b, 0, p)) if kb_batched else (lambda b, p, i: (0, 0, p))
    return pl.pallas_call(
        functools.partial(_pair_attn_kernel, diff=diff, lam_init=lam_init,
                          max_full=seq_len // (KV_SUBS * TK) - 1),
        grid=(batch, N_PAIRS, nq),
        in_specs=[
            pl.BlockSpec((1, 1, LANES, TQ), lambda b, p, i: (b, i, p, 0)),
            pl.BlockSpec((1, 1, LANES, TQ),
                         lambda b, p, i: (b, jnp.minimum(i + 1, nq - 1), p, 0)),
            pl.BlockSpec((1, seq_len, LANES), lambda b, p, i: (b, 0, p)),
            pl.BlockSpec((1, seq_len, LANES), kb_map),
            pl.BlockSpec((1, nk, LANES, TK), lambda b, p, i: (b, 0, p, 0)),
            pl.BlockSpec((4, LANES), lambda b, p, i: (0, 0)),
            pl.BlockSpec((1, LANES), lambda b, p, i: (0, p)),
        ],
        out_specs=pl.BlockSpec((1, TQ, LANES), lambda b, p, i: (b, i, p)),
        out_shape=jax.ShapeDtypeStruct((batch, seq_len, BRANCH_WIDTH), F32),
        scratch_shapes=[
            pltpu.VMEM((2, 2 * LANES, 2 * TQ), BF16),
            pltpu.VMEM((1, 2 * TQ), F32),
            pltpu.VMEM((1, 2 * TQ), F32),
            pltpu.VMEM((LANES, 2 * TQ), F32),
            pltpu.VMEM((2, KV_SUBS * TK, 2 * TQ), F32),
            pltpu.VMEM((2, KV_SUBS * TK, 2 * TQ), F32),
            pltpu.VMEM((2, 1, 2 * TQ), F32),
            pltpu.VMEM((2, 1, 2 * TQ), F32),
        ],
        compiler_params=pltpu.CompilerParams(
            dimension_semantics=("arbitrary", "arbitrary", "arbitrary"),
            vmem_limit_bytes=VMEM_LIMIT),
        name="diff_attn" if diff else "fox_attn",
    )(qT, qT, k, kb, vT, lam_pad, g)


def _dil_kernel(q_ref, k_ref, v_ref, kb_ref, o_ref,
                qf, kf, vf, kbf, acc, m_sc, l_sc, mask_sc, *, seq_len):
    L = DIL_BLOCK
    T = DIL_TQ
    KEYS = T + L
    qf[...] = q_ref[0].astype(F32)
    kf[...] = k_ref[0].astype(F32)
    vf[...] = v_ref[0].astype(F32)
    kbf[...] = kb_ref[0].astype(F32)
    acc[...] = jnp.zeros(acc.shape, F32)
    m_sc[...] = jnp.full(m_sc.shape, -jnp.inf, F32)
    l_sc[...] = jnp.zeros(l_sc.shape, F32)

    kr = lax.broadcasted_iota(jnp.int32, (2 * L, 2 * L), 0)
    qc = lax.broadcasted_iota(jnp.int32, (2 * L, 2 * L), 1) % L
    band = (kr >= qc) & (kr <= qc + L)
    mask_sc[0] = jnp.where(band & (kr >= L), 0.0, -jnp.inf)
    mask_sc[1] = jnp.where(band, 0.0, -jnp.inf)

    lane = lax.broadcasted_iota(jnp.int32, (T, LANES), 1)
    first = lane < HEAD_DIM
    sel0 = jnp.where(lane < 3, 1.0, 0.0)
    sel1 = jnp.where((lane >= 3) & (lane < 6), 1.0, 0.0)
    frow = lax.broadcasted_iota(jnp.int32, (LANES, L), 0) < HEAD_DIM
    n_qb = T // L

    for _, d in DIL_PATTERNS:
        nt = seq_len // (T * d)

        def group(u, carry, d=d, nt=nt):
            rows, prows, slot = [], [], []
            for g in range(DIL_GROUP):
                t = u * DIL_GROUP + g
                r = t // nt
                it = t % nt
                start = r + d * T * it
                prev = jnp.where(it == 0, start, start - d * L)
                rows.append(pl.ds(start, T, stride=d))
                prows.append(pl.ds(prev, L, stride=d))
                slot.append(jnp.minimum(it, 1))
            m_old = [m_sc[rows[g], :] for g in range(DIL_GROUP)]
            l_old = [l_sc[rows[g], :] for g in range(DIL_GROUP)]
            a_old = [acc[rows[g], :] for g in range(DIL_GROUP)]
            s, vT = [], []
            for g in range(DIL_GROUP):
                qb = qf[rows[g], :]
                q0 = jnp.concatenate([jnp.where(first, qb, 0.0), sel0], axis=1).astype(BF16)
                q1 = jnp.concatenate([jnp.where(first, 0.0, qb), sel1], axis=1).astype(BF16)
                ka = jnp.concatenate(
                    [jnp.concatenate([kf[prows[g], :], kbf[prows[g], :]], axis=1),
                     jnp.concatenate([kf[rows[g], :], kbf[rows[g], :]], axis=1)],
                    axis=0).astype(BF16)
                sg = []
                for b in range(n_qb):
                    qa = jnp.concatenate([q0[b * L:(b + 1) * L], q1[b * L:(b + 1) * L]], axis=0)
                    sb = lax.dot_general(ka[b * L:(b + 2) * L], qa, (((1,), (1,)), ((), ())),
                                         preferred_element_type=F32)
                    sg.append(sb + mask_sc[slot[g] if b == 0 else 1])
                s.append(sg)
                vv = jnp.concatenate([vf[prows[g], :], vf[rows[g], :]], axis=0)
                vT.append(vv.T.astype(BF16))
            num, m_blk, l_blk = [], [], []
            for g in range(DIL_GROUP):
                nb_, mb_, lb_ = [], [], []
                for b in range(n_qb):
                    mg = jnp.max(s[g][b], axis=0, keepdims=True)
                    p = jnp.exp2(s[g][b] - mg)
                    lg = jnp.sum(p, axis=0, keepdims=True)
                    oT = jnp.dot(vT[g][:, b * L:(b + 2) * L], p.astype(BF16),
                                 preferred_element_type=F32)
                    nb_.append(jnp.where(frow, oT[:, :L], oT[:, L:]))
                    mb_.append(jnp.where(frow, jnp.broadcast_to(mg[:, :L], (LANES, L)),
                                         jnp.broadcast_to(mg[:, L:], (LANES, L))))
                    lb_.append(jnp.where(frow, jnp.broadcast_to(lg[:, :L], (LANES, L)),
                                         jnp.broadcast_to(lg[:, L:], (LANES, L))))
                num.append(jnp.concatenate(nb_, axis=1).T)
                m_blk.append(jnp.concatenate(mb_, axis=1).T)
                l_blk.append(jnp.concatenate(lb_, axis=1).T)
            for g in range(DIL_GROUP):
                m_new = jnp.maximum(m_old[g], m_blk[g])
                w_old = jnp.exp2(m_old[g] - m_new)
                w_blk = jnp.exp2(m_blk[g] - m_new)
                l_sc[rows[g], :] = w_old * l_old[g] + w_blk * l_blk[g]
                acc[rows[g], :] = w_old * a_old[g] + w_blk * num[g]
                m_sc[rows[g], :] = m_new
            return carry

        lax.fori_loop(0, seq_len // (T * DIL_GROUP), group, 0)

    o_ref[0] = acc[...] / l_sc[...]


def _dil_call(q, k, v, kb, *, batch, seq_len):
    q = q.reshape(batch, seq_len, BRANCH_WIDTH)
    k = k.reshape(batch, seq_len, BRANCH_WIDTH)
    v = v.reshape(batch, seq_len, BRANCH_WIDTH)
    blk = pl.BlockSpec((1, seq_len, LANES), lambda b, p: (b, 0, p))
    return pl.pallas_call(
        functools.partial(_dil_kernel, seq_len=seq_len),
        grid=(batch, N_PAIRS),
        in_specs=[blk, blk, blk, pl.BlockSpec((1, seq_len, LANES), lambda b, p: (0, 0, p))],
        out_specs=blk,
        out_shape=jax.ShapeDtypeStruct((batch, seq_len, BRANCH_WIDTH), F32),
        scratch_shapes=[pltpu.VMEM((seq_len, LANES), F32) for _ in range(7)]
        + [pltpu.VMEM((2, 2 * DIL_BLOCK, 2 * DIL_BLOCK), F32)],
        compiler_params=pltpu.CompilerParams(
            dimension_semantics=("arbitrary", "arbitrary"), vmem_limit_bytes=VMEM_LIMIT),
        name="dil_attn",
    )(q, k, v, kb)


def _merge_kernel(x_ref, oa_ref, ob_ref, oc_ref, g_ref, wz_ref, wg_ref, wb_ref, wo_ref,
                  fg_ref, out_ref, *, last):
    x = x_ref[...]
    ms = jnp.mean(x * x, axis=-1, keepdims=True)
    h = (x * lax.rsqrt(ms + RMS_EPS) * g_ref[...]).astype(BF16)
    merged = None
    for n, o_ref in enumerate((oa_ref, ob_ref, oc_ref)):
        z = jnp.dot(h, wz_ref[:, n * BRANCH_WIDTH:(n + 1) * BRANCH_WIDTH],
                    preferred_element_type=F32)
        y = (o_ref[...] * (z * jax.nn.sigmoid(z))).astype(BF16)
        br = jnp.dot(y, wb_ref[n], preferred_element_type=F32)
        gate = jax.nn.sigmoid(jnp.dot(h, wg_ref[:, n * D_MODEL:(n + 1) * D_MODEL],
                                      preferred_element_type=F32))
        merged = gate * br if merged is None else merged + gate * br
    out = x + jnp.dot(merged.astype(BF16), wo_ref[...], preferred_element_type=F32)
    if last:
        ms = jnp.mean(out * out, axis=-1, keepdims=True)
        out = out * lax.rsqrt(ms + RMS_EPS) * fg_ref[...]
    out_ref[...] = out


def _merge_call(x2d, oa, ob, oc, g, wz, wg, wb, wo, fg, *, last):
    n_tok = x2d.shape[0]
    tm = TM_MERGE
    const2 = lambda i: (0, 0)
    tok = pl.BlockSpec((tm, BRANCH_WIDTH), lambda i: (i, 0))
    return pl.pallas_call(
        functools.partial(_merge_kernel, last=last),
        grid=(n_tok // tm,),
        in_specs=[
            pl.BlockSpec((tm, D_MODEL), lambda i: (i, 0)), tok, tok, tok,
            pl.BlockSpec((1, D_MODEL), const2),
            pl.BlockSpec((D_MODEL, 3 * BRANCH_WIDTH), const2),
            pl.BlockSpec((D_MODEL, 3 * D_MODEL), const2),
            pl.BlockSpec((3, BRANCH_WIDTH, D_MODEL), lambda i: (0, 0, 0)),
            pl.BlockSpec((D_MODEL, D_MODEL), const2),
            pl.BlockSpec((1, D_MODEL), const2),
        ],
        out_specs=pl.BlockSpec((tm, D_MODEL), lambda i: (i, 0)),
        out_shape=jax.ShapeDtypeStruct((n_tok, D_MODEL), F32),
        compiler_params=pltpu.CompilerParams(
            dimension_semantics=("arbitrary",), vmem_limit_bytes=VMEM_LIMIT),
        name="merge",
    )(x2d, oa.reshape(n_tok, BRANCH_WIDTH), ob.reshape(n_tok, BRANCH_WIDTH),
      oc.reshape(n_tok, BRANCH_WIDTH), g, wz, wg, wb, wo, fg)


def _alibi_slopes(n_heads):
    return [2.0 ** (-ALIBI_MAX_EXP * (h + 1) / n_heads) for h in range(n_heads)]


def _alibi_key_bias(seq_len, slopes_per_pair):
    pos = np.arange(seq_len, dtype=np.float64)
    out = np.zeros((seq_len, BRANCH_WIDTH), dtype=BF16)
    for p, pair in enumerate(slopes_per_pair):
        for c, slope in enumerate(pair):
            rest = pos * (slope * LOG2E)
            for x in range(3):
                piece = rest.astype(BF16)
                out[:, p * LANES + 3 * c + x] = piece
                rest = rest - piece.astype(np.float64)
    return jnp.asarray(out)[None]


def _fox_bias_scatter():
    m = [[0.0] * BRANCH_WIDTH for _ in range(3 * LANES)]
    for x in range(3):
        for h in range(N_HEADS):
            m[x * LANES + h][(h // 2) * LANES + 3 * (h % 2) + x] = 1.0
    return jnp.asarray(m, dtype=BF16)


def kernel(x, norm_g, w_in, fox_fb, diff_lam, diff_norm_g, w_branch, w_out, final_g):
    batch, seq_len, _ = x.shape
    depth = norm_g.shape[0]
    n_tok = batch * seq_len
    scale = HEAD_DIM ** -0.5 * LOG2E

    w = BRANCH_WIDTH
    off = {}
    pos = 0
    for name, width in (("diff_q", w), ("diff_k", w), ("diff_v", w), ("diff_z", w),
                        ("fox_q", w), ("fox_k", w), ("fox_v", w), ("fox_f", N_HEADS),
                        ("fox_z", w), ("dil_q", w), ("dil_k", w), ("dil_v", w),
                        ("dil_z", w), ("merge_g", 3 * D_MODEL)):
        off[name] = (pos, pos + width)
        pos += width

    def cols(wl, name, mult=1.0):
        a, b = off[name]
        return wl[:, a:b] * mult if mult != 1.0 else wl[:, a:b]

    ds = _alibi_slopes(DIFF_HEADS)
    cs = _alibi_slopes(N_HEADS)
    diff_kb = _alibi_key_bias(seq_len, [(ds[h], ds[h]) for h in range(DIFF_HEADS)])
    dil_kb = _alibi_key_bias(seq_len, [(cs[2 * p], cs[2 * p + 1]) for p in range(N_PAIRS)])
    pmat = _fox_bias_scatter()
    ones_g = jnp.ones((1, BRANCH_WIDTH), F32)
    zero_lam = jnp.zeros((4, LANES), F32)

    x2d = x.reshape(n_tok, D_MODEL)
    for l in range(depth):
        wl = w_in[l]
        w_qkv = jnp.concatenate(
            [cols(wl, "diff_q", scale), cols(wl, "diff_k"), cols(wl, "diff_v"),
             cols(wl, "fox_q", scale), cols(wl, "fox_k"), cols(wl, "fox_v"),
             cols(wl, "dil_q", scale), cols(wl, "dil_k"), cols(wl, "dil_v")],
            axis=1).astype(BF16)
        w_f = jnp.pad(cols(wl, "fox_f"), ((0, 0), (0, LANES - N_HEADS))).astype(BF16)
        fb = jnp.pad(fox_fb[l].astype(F32), (0, LANES - N_HEADS))[None]
        g = norm_g[l].astype(F32)[None]

        (dqT, dk, dvT, fqT, fk, fvT, fkb, cq, ck, cv) = _proj_call(
            x2d, g, w_qkv, w_f, fb, pmat, seq_len=seq_len)

        lam_init = 0.8 - 0.6 * math.exp(-0.3 * l)
        lam_pad = jnp.pad(diff_lam[l].astype(F32), ((0, 0), (0, LANES - HEAD_DIM)))
        oa = _pair_attn_call(dqT, dk, diff_kb, dvT, lam_pad,
                             diff_norm_g[l].astype(F32)[None],
                             batch=batch, seq_len=seq_len, diff=True, lam_init=lam_init)
        ob = _pair_attn_call(fqT, fk, fkb.reshape(batch, seq_len, BRANCH_WIDTH), fvT,
                             zero_lam, ones_g,
                             batch=batch, seq_len=seq_len, diff=False, lam_init=0.0)
        oc = _dil_call(cq, ck, cv, dil_kb, batch=batch, seq_len=seq_len)

        wz = jnp.concatenate([cols(wl, "diff_z"), cols(wl, "fox_z"), cols(wl, "dil_z")],
                             axis=1).astype(BF16)
        wg = cols(wl, "merge_g").astype(BF16)
        x2d = _merge_call(x2d, oa, ob, oc, g, wz, wg, w_branch[l].astype(BF16),
                          w_out[l].astype(BF16), final_g.astype(F32)[None],
                          last=(l == depth - 1))
    return x2d.reshape(batch, seq_len, D_MODEL)
```

```python
import functools
import math

import jax
import jax.numpy as jnp
import numpy as np
from jax import lax
from jax.experimental import pallas as pl
from jax.experimental.pallas import tpu as pltpu

F32 = jnp.float32
BF16 = jnp.bfloat16

D_MODEL = 1024
HEAD_DIM = 64
BRANCH_WIDTH = 512
LANES = 128
N_PAIRS = BRANCH_WIDTH // LANES
DIFF_HEADS = 4
N_HEADS = 8
DIL_PATTERNS = ((128, 1), (512, 4), (2048, 16))
DIL_BLOCK = 128
DIL_TQ = 256
DIL_GROUP = 2
RMS_EPS = 1e-6
LOG2E = math.log2(math.e)
ALIBI_MAX_EXP = 8.0
N_QKV_GROUPS = 9
TQ = 512
TK = 512
KV_SUBS = 2
TM_PROJ = 512
TM_MERGE = 512
VMEM_LIMIT = 56 * 1024 * 1024


def _split3(a):
    a1 = a.astype(BF16)
    r1 = a - a1.astype(F32)
    a2 = r1.astype(BF16)
    a3 = (r1 - a2.astype(F32)).astype(BF16)
    return a1, a2, a3


def _proj_kernel(x_ref, g_ref, w_ref, wf_ref, fb_ref, pmat_ref,
                 dqT_ref, dk_ref, dvT_ref, fqT_ref, fk_ref, fvT_ref, fkb_ref,
                 cq_ref, ck_ref, cv_ref, carry_ref, *, tiles_per_seq):
    i = pl.program_id(0)
    tm = x_ref.shape[0]
    x = x_ref[...]
    ms = jnp.mean(x * x, axis=-1, keepdims=True)
    h = (x * lax.rsqrt(ms + RMS_EPS) * g_ref[...]).astype(BF16)

    def proj(g):
        return jnp.dot(h, w_ref[:, g * BRANCH_WIDTH:(g + 1) * BRANCH_WIDTH],
                       preferred_element_type=F32)

    def put_t(ref, val):
        for t in range(tm // TQ):
            ref[t] = val[t * TQ:(t + 1) * TQ, :].T.astype(BF16)

    put_t(dqT_ref, proj(0))
    dk_ref[...] = proj(1).astype(BF16)
    put_t(dvT_ref, proj(2))
    put_t(fqT_ref, proj(3))
    fk_ref[...] = proj(4).astype(BF16)
    put_t(fvT_ref, proj(5))
    cq_ref[...] = proj(6).astype(BF16)
    ck_ref[...] = proj(7).astype(BF16)
    cv_ref[...] = proj(8).astype(BF16)

    f = jnp.dot(h, wf_ref[...], preferred_element_type=F32) + fb_ref[...]
    logf = jnp.minimum(f, 0.0) - jnp.log1p(jnp.exp(-jnp.abs(f)))

    @pl.when(i % tiles_per_seq == 0)
    def _():
        carry_ref[...] = jnp.zeros_like(carry_ref)

    row = lax.broadcasted_iota(jnp.int32, (tm, tm), 0)
    col = lax.broadcasted_iota(jnp.int32, (tm, tm), 1)
    tri = jnp.where(row >= col, 1.0, 0.0).astype(BF16)
    a1, a2, a3 = _split3(logf)
    c = (jnp.dot(tri, a1, preferred_element_type=F32)
         + jnp.dot(tri, a2, preferred_element_type=F32)
         + jnp.dot(tri, a3, preferred_element_type=F32)) + carry_ref[0:1, :]
    carry_ref[0:1, :] = c[tm - 1:tm, :]
    b1, b2, b3 = _split3(c * (-LOG2E))
    kb = jnp.dot(jnp.concatenate([b1, b2, b3], axis=1), pmat_ref[...],
                 preferred_element_type=F32)
    fkb_ref[...] = kb.astype(BF16)


def _proj_call(x2d, g, w_qkv, w_f, fb, pmat, *, seq_len):
    n_tok = x2d.shape[0]
    tm = TM_PROJ
    nt = n_tok // tm
    const = lambda i: (0, 0)
    tok = pl.BlockSpec((tm, BRANCH_WIDTH), lambda i: (i, 0))
    tr = pl.BlockSpec((tm // TQ, BRANCH_WIDTH, TQ), lambda i: (i, 0, 0))
    nat_shape = jax.ShapeDtypeStruct((n_tok, BRANCH_WIDTH), BF16)
    tr_shape = jax.ShapeDtypeStruct((n_tok // TQ, BRANCH_WIDTH, TQ), BF16)
    return pl.pallas_call(
        functools.partial(_proj_kernel, tiles_per_seq=seq_len // tm),
        grid=(nt,),
        in_specs=[
            pl.BlockSpec((tm, D_MODEL), lambda i: (i, 0)),
            pl.BlockSpec((1, D_MODEL), const),
            pl.BlockSpec((D_MODEL, N_QKV_GROUPS * BRANCH_WIDTH), const),
            pl.BlockSpec((D_MODEL, LANES), const),
            pl.BlockSpec((1, LANES), const),
            pl.BlockSpec((3 * LANES, BRANCH_WIDTH), const),
        ],
        out_specs=[tr, tok, tr, tr, tok, tr, tok, tok, tok, tok],
        out_shape=[tr_shape, nat_shape, tr_shape, tr_shape, nat_shape, tr_shape,
                   nat_shape, nat_shape, nat_shape, nat_shape],
        scratch_shapes=[pltpu.VMEM((8, LANES), F32)],
        compiler_params=pltpu.CompilerParams(
            dimension_semantics=("arbitrary",), vmem_limit_bytes=VMEM_LIMIT),
        name="proj",
    )(x2d, g, w_qkv, w_f, fb, pmat)


def _pair_attn_kernel(qT_ref, qTn_ref, k_ref, kb_ref, vT_ref, lam_ref, g_ref, o_ref,
                      qa_sc, m_sc, l_sc, acc_sc, s_sc, t0_sc, smax_sc, t0max_sc,
                      *, diff, lam_init, max_full):
    i = pl.program_id(2)
    par = i % 2
    row = lax.broadcasted_iota(jnp.int32, (LANES, TQ), 0)

    def build_qa(q_ref, slot):
        qT = q_ref[0, 0].astype(F32)
        qa_sc[slot, :, 0:TQ] = jnp.concatenate(
            [jnp.where(row < HEAD_DIM, qT, 0.0), jnp.where(row < 3, 1.0, 0.0)],
            axis=0).astype(BF16)
        qa_sc[slot, :, TQ:2 * TQ] = jnp.concatenate(
            [jnp.where(row >= HEAD_DIM, qT, 0.0),
             jnp.where((row >= 3) & (row < 6), 1.0, 0.0)], axis=0).astype(BF16)

    tile_keys = KV_SUBS * TK

    def score_tile(t, qa_slot, dst, dst_max, dst_slot):
        base = pl.multiple_of(t * tile_keys, tile_keys)
        ka = jnp.concatenate([k_ref[0, pl.ds(base, tile_keys), :],
                              kb_ref[0, pl.ds(base, tile_keys), :]], axis=1)
        s = jnp.dot(ka, qa_sc[qa_slot], preferred_element_type=F32)
        dst[dst_slot] = s
        dst_max[dst_slot] = jnp.max(s, axis=0, keepdims=True)

    @pl.when(i == 0)
    def _():
        build_qa(qT_ref, 0)
        score_tile(0, 0, t0_sc, t0max_sc, 0)

    build_qa(qTn_ref, 1 - par)
    m_sc[...] = jnp.full(m_sc.shape, -jnp.inf, F32)
    l_sc[...] = jnp.zeros(l_sc.shape, F32)
    acc_sc[...] = jnp.zeros(acc_sc.shape, F32)

    def consume(s, sub0, n_sub, tile_max):
        rows = n_sub * TK
        if tile_max is None:
            kr = lax.broadcasted_iota(jnp.int32, (TK, 2 * TQ), 0)
            qc = lax.broadcasted_iota(jnp.int32, (TK, 2 * TQ), 1) % TQ
            tail = jnp.where(kr <= qc, s[rows - TK:], -jnp.inf)
            s = tail if n_sub == 1 else jnp.concatenate([s[:rows - TK], tail], axis=0)
            tile_max = jnp.max(s, axis=0, keepdims=True)
        m_old = m_sc[...]
        m_new = jnp.maximum(m_old, tile_max)
        alpha = jnp.exp2(m_old - m_new)
        p = jnp.exp2(s - m_new)
        l_sc[...] = alpha * l_sc[...] + jnp.sum(p, axis=0, keepdims=True)
        pb = p.astype(BF16)
        pv = jnp.dot(vT_ref[0, sub0], pb[0:TK], preferred_element_type=F32)
        for t in range(1, n_sub):
            pv = pv + jnp.dot(vT_ref[0, sub0 + t], pb[t * TK:(t + 1) * TK],
                              preferred_element_type=F32)
        acc_sc[...] = alpha * acc_sc[...] + pv
        m_sc[...] = m_new

    def full_block(cur, cur_max, cur_slot, sub0, t_next):
        m_old = m_sc[...]
        m_new = jnp.maximum(m_old, cur_max[cur_slot])
        alpha = jnp.exp2(m_old - m_new)
        nslot = t_next % 2
        lsum = pv = nmax = None
        for t in range(KV_SUBS):
            kbase = pl.multiple_of(t_next * tile_keys + t * TK, TK)
            ka = jnp.concatenate([k_ref[0, pl.ds(kbase, TK), :],
                                  kb_ref[0, pl.ds(kbase, TK), :]], axis=1)
            sn = jnp.dot(ka, qa_sc[par], preferred_element_type=F32)
            s_sc[nslot, t * TK:(t + 1) * TK, :] = sn
            cm = jnp.max(sn, axis=0, keepdims=True)
            nmax = cm if nmax is None else jnp.maximum(nmax, cm)
            p = jnp.exp2(cur[cur_slot, t * TK:(t + 1) * TK, :] - m_new)
            ls = jnp.sum(p, axis=0, keepdims=True)
            lsum = ls if lsum is None else lsum + ls
            d = jnp.dot(vT_ref[0, sub0 + t], p.astype(BF16), preferred_element_type=F32)
            pv = d if pv is None else pv + d
        smax_sc[nslot] = nmax
        l_sc[...] = alpha * l_sc[...] + lsum
        acc_sc[...] = alpha * acc_sc[...] + pv
        m_sc[...] = m_new

    n_full = i // KV_SUBS
    rem = i % KV_SUBS
    for j in range(max_full):
        @pl.when(j < n_full)
        def _(j=j):
            if j == 0:
                full_block(t0_sc, t0max_sc, par, 0, j + 1)
            else:
                full_block(s_sc, smax_sc, j % 2, j * KV_SUBS, j + 1)
    for r in range(KV_SUBS):
        for diag_is_tile0 in (True, False):
            @pl.when((rem == r) & ((n_full == 0) == diag_is_tile0))
            def _(r=r, diag_is_tile0=diag_is_tile0):
                score_tile(0, 1 - par, t0_sc, t0max_sc, (r + 1) % 2)
                rows = (r + 1) * TK
                s = t0_sc[r % 2, 0:rows, :] if diag_is_tile0 else s_sc[n_full % 2, 0:rows, :]
                consume(s, n_full * KV_SUBS, r + 1, None)

    a = acc_sc[...] / l_sc[...]
    a0 = a[:, 0:TQ]
    a1 = a[:, TQ:2 * TQ]
    if diff:
        dl = lam_ref[...]
        lam = (jnp.exp(jnp.sum(dl[0:1] * dl[1:2], axis=1, keepdims=True))
               - jnp.exp(jnp.sum(dl[2:3] * dl[3:4], axis=1, keepdims=True)) + lam_init)
        oT = a0 - lam * a1
        ms = jnp.mean(oT * oT, axis=0, keepdims=True)
        oT = oT * lax.rsqrt(ms + RMS_EPS)
        o_ref[0] = (oT.T * g_ref[...]) * (1.0 - lam_init)
    else:
        o_ref[0] = jnp.where(row < HEAD_DIM, a0, a1).T


def _pair_attn_call(qT, k, kb, vT, lam_pad, g, *, batch, seq_len, diff, lam_init):
    nq = seq_len // TQ
    nk = seq_len // TK
    qT = qT.reshape(batch, nq, BRANCH_WIDTH, TQ)
    vT = vT.reshape(batch, nk, BRANCH_WIDTH, TK)
    k = k.reshape(batch, seq_len, BRANCH_WIDTH)
    kb_batched = kb.shape[0] != 1
    kb_map = (lambda b, p, i: (b, 0, p)) if kb_batched else (lambda b, p, i: (0, 0, p))
    return pl.pallas_call(
        functools.partial(_pair_attn_kernel, diff=diff, lam_init=lam_init,
                          max_full=seq_len // (KV_SUBS * TK) - 1),
        grid=(batch, N_PAIRS, nq),
        in_specs=[
            pl.BlockSpec((1, 1, LANES, TQ), lambda b, p, i: (b, i, p, 0)),
            pl.BlockSpec((1, 1, LANES, TQ),
                         lambda b, p, i: (b, jnp.minimum(i + 1, nq - 1), p, 0)),
            pl.BlockSpec((1, seq_len, LANES), lambda b, p, i: (b, 0, p)),
            pl.BlockSpec((1, seq_len, LANES), kb_map),
            pl.BlockSpec((1, nk, LANES, TK), lambda b, p, i: (b, 0, p, 0)),
            pl.BlockSpec((4, LANES), lambda b, p, i: (0, 0)),
            pl.BlockSpec((1, LANES), lambda b, p, i: (0, p)),
        ],
        out_specs=pl.BlockSpec((1, TQ, LANES), lambda b, p, i: (b, i, p)),
        out_shape=jax.ShapeDtypeStruct((batch, seq_len, BRANCH_WIDTH), F32),
        scratch_shapes=[
            pltpu.VMEM((2, 2 * LANES, 2 * TQ), BF16),
            pltpu.VMEM((1, 2 * TQ), F32),
            pltpu.VMEM((1, 2 * TQ), F32),
            pltpu.VMEM((LANES, 2 * TQ), F32),
            pltpu.VMEM((2, KV_SUBS * TK, 2 * TQ), F32),
            pltpu.VMEM((2, KV_SUBS * TK, 2 * TQ), F32),
            pltpu.VMEM((2, 1, 2 * TQ), F32),
            pltpu.VMEM((2, 1, 2 * TQ), F32),
        ],
        compiler_params=pltpu.CompilerParams(
            dimension_semantics=("arbitrary", "arbitrary", "arbitrary"),
            vmem_limit_bytes=VMEM_LIMIT),
        name="diff_attn" if diff else "fox_attn",
    )(qT, qT, k, kb, vT, lam_pad, g)


def _dil_kernel(q_ref, k_ref, v_ref, kb_ref, o_ref,
                qf, kf, vf, kbf, acc, m_sc, l_sc, mask_sc, s_buf, vT_buf, smax_buf,
                *, seq_len):
    L = DIL_BLOCK
    T = DIL_TQ
    KEYS = T + L
    qf[...] = q_ref[0].astype(F32)
    kf[...] = k_ref[0].astype(F32)
    vf[...] = v_ref[0].astype(F32)
    kbf[...] = kb_ref[0].astype(F32)
    acc[...] = jnp.zeros(acc.shape, F32)
    m_sc[...] = jnp.full(m_sc.shape, -jnp.inf, F32)
    l_sc[...] = jnp.zeros(l_sc.shape, F32)

    kr = lax.broadcasted_iota(jnp.int32, (2 * L, 2 * L), 0)
    qc = lax.broadcasted_iota(jnp.int32, (2 * L, 2 * L), 1) % L
    band = (kr >= qc) & (kr <= qc + L)
    mask_sc[0] = jnp.where(band & (kr >= L), 0.0, -jnp.inf)
    mask_sc[1] = jnp.where(band, 0.0, -jnp.inf)

    lane = lax.broadcasted_iota(jnp.int32, (T, LANES), 1)
    first = lane < HEAD_DIM
    sel0 = jnp.where(lane < 3, 1.0, 0.0).astype(BF16)
    sel1 = jnp.where((lane >= 3) & (lane < 6), 1.0, 0.0).astype(BF16)
    head0 = jnp.where(first, 1.0, 0.0).astype(BF16)
    head1 = jnp.where(first, 0.0, 1.0).astype(BF16)
    frow = lax.broadcasted_iota(jnp.int32, (LANES, L), 0) < HEAD_DIM
    n_qb = T // L

    for _, d in DIL_PATTERNS:
        nt = seq_len // (T * d)

        n_groups = seq_len // (T * DIL_GROUP)

        def addr(u, g, d=d, nt=nt):
            t = u * DIL_GROUP + g
            r = t // nt
            it = t % nt
            start = r + d * T * it
            prev = jnp.where(it == 0, start, start - d * L)
            if d == 1:
                start = pl.multiple_of(start, L)
                prev = pl.multiple_of(prev, L)
            return pl.ds(start, T, stride=d), pl.ds(prev, L, stride=d), jnp.minimum(it, 1)

        def rows_bf16(ref, ref_f32, rows, d=d):
            return ref[0, rows, :] if d == 1 else ref_f32[rows, :].astype(BF16)

        def score_group(u, buf):
            for g in range(DIL_GROUP):
                rows, prows, slot = addr(u, g)
                qb = rows_bf16(q_ref, qf, rows)
                q0 = jnp.concatenate([qb * head0, sel0], axis=1)
                q1 = jnp.concatenate([qb * head1, sel1], axis=1)
                ka = jnp.concatenate(
                    [jnp.concatenate([rows_bf16(k_ref, kf, prows),
                                      rows_bf16(kb_ref, kbf, prows)], axis=1),
                     jnp.concatenate([rows_bf16(k_ref, kf, rows),
                                      rows_bf16(kb_ref, kbf, rows)], axis=1)], axis=0)
                for b in range(n_qb):
                    qa = jnp.concatenate([q0[b * L:(b + 1) * L], q1[b * L:(b + 1) * L]], axis=0)
                    sb = lax.dot_general(ka[b * L:(b + 2) * L], qa, (((1,), (1,)), ((), ())),
                                         preferred_element_type=F32)
                    sb = sb + mask_sc[slot if b == 0 else 1]
                    s_buf[buf, g, b] = sb
                    smax_buf[buf, g, b] = jnp.max(sb, axis=0, keepdims=True)
                vv = jnp.concatenate([vf[prows, :], vf[rows, :]], axis=0)
                vT_buf[buf, g] = vv.T.astype(BF16)

        def consume_group(u, buf):
            num, m_blk, l_blk = [], [], []
            for g in range(DIL_GROUP):
                nb_, mb_, lb_ = [], [], []
                for b in range(n_qb):
                    mg = smax_buf[buf, g, b]
                    p = jnp.exp2(s_buf[buf, g, b] - mg)
                    lg = jnp.sum(p, axis=0, keepdims=True)
                    oT = jnp.dot(vT_buf[buf, g, :, b * L:(b + 2) * L], p.astype(BF16),
                                 preferred_element_type=F32)
                    nb_.append(jnp.where(frow, oT[:, :L], oT[:, L:]))
                    mb_.append(jnp.where(frow, jnp.broadcast_to(mg[:, :L], (LANES, L)),
                                         jnp.broadcast_to(mg[:, L:], (LANES, L))))
                    lb_.append(jnp.where(frow, jnp.broadcast_to(lg[:, :L], (LANES, L)),
                                         jnp.broadcast_to(lg[:, L:], (LANES, L))))
                num.append(jnp.concatenate(nb_, axis=1).T)
                m_blk.append(jnp.concatenate(mb_, axis=1).T)
                l_blk.append(jnp.concatenate(lb_, axis=1).T)
            for g in range(DIL_GROUP):
                rows, _, _ = addr(u, g)
                m_old = m_sc[rows, :]
                m_new = jnp.maximum(m_old, m_blk[g])
                w_old = jnp.exp2(m_old - m_new)
                w_blk = jnp.exp2(m_blk[g] - m_new)
                l_sc[rows, :] = w_old * l_sc[rows, :] + w_blk * l_blk[g]
                acc[rows, :] = w_old * acc[rows, :] + w_blk * num[g]
                m_sc[rows, :] = m_new

        score_group(0, 0)

        def trip(v, carry):
            score_group(2 * v + 1, 1)
            consume_group(2 * v, 0)
            score_group(jnp.minimum(2 * v + 2, n_groups - 1), 0)
            consume_group(2 * v + 1, 1)
            return carry

        lax.fori_loop(0, n_groups // 2, trip, 0)

    o_ref[0] = acc[...] / l_sc[...]


def _dil_call(q, k, v, kb, *, batch, seq_len):
    q = q.reshape(batch, seq_len, BRANCH_WIDTH)
    k = k.reshape(batch, seq_len, BRANCH_WIDTH)
    v = v.reshape(batch, seq_len, BRANCH_WIDTH)
    blk = pl.BlockSpec((1, seq_len, LANES), lambda b, p: (b, 0, p))
    return pl.pallas_call(
        functools.partial(_dil_kernel, seq_len=seq_len),
        grid=(batch, N_PAIRS),
        in_specs=[blk, blk, blk, pl.BlockSpec((1, seq_len, LANES), lambda b, p: (0, 0, p))],
        out_specs=blk,
        out_shape=jax.ShapeDtypeStruct((batch, seq_len, BRANCH_WIDTH), F32),
        scratch_shapes=[pltpu.VMEM((seq_len, LANES), F32) for _ in range(7)]
        + [pltpu.VMEM((2, 2 * DIL_BLOCK, 2 * DIL_BLOCK), F32),
           pltpu.VMEM((2, DIL_GROUP, DIL_TQ // DIL_BLOCK, 2 * DIL_BLOCK, 2 * DIL_BLOCK), F32),
           pltpu.VMEM((2, DIL_GROUP, LANES, DIL_TQ + DIL_BLOCK), BF16),
           pltpu.VMEM((2, DIL_GROUP, DIL_TQ // DIL_BLOCK, 1, 2 * DIL_BLOCK), F32)],
        compiler_params=pltpu.CompilerParams(
            dimension_semantics=("arbitrary", "arbitrary"), vmem_limit_bytes=VMEM_LIMIT),
        name="dil_attn",
    )(q, k, v, kb)


def _merge_kernel(x_ref, oa_ref, ob_ref, oc_ref, g_ref, wz_ref, wg_ref, wb_ref, wo_ref,
                  fg_ref, out_ref, *, last):
    x = x_ref[...]
    ms = jnp.mean(x * x, axis=-1, keepdims=True)
    h = (x * lax.rsqrt(ms + RMS_EPS) * g_ref[...]).astype(BF16)
    merged = None
    for n, o_ref in enumerate((oa_ref, ob_ref, oc_ref)):
        z = jnp.dot(h, wz_ref[:, n * BRANCH_WIDTH:(n + 1) * BRANCH_WIDTH],
                    preferred_element_type=F32)
        y = (o_ref[...] * (z * jax.nn.sigmoid(z))).astype(BF16)
        br = jnp.dot(y, wb_ref[n], preferred_element_type=F32)
        gate = jax.nn.sigmoid(jnp.dot(h, wg_ref[:, n * D_MODEL:(n + 1) * D_MODEL],
                                      preferred_element_type=F32))
        merged = gate * br if merged is None else merged + gate * br
    out = x + jnp.dot(merged.astype(BF16), wo_ref[...], preferred_element_type=F32)
    if last:
        ms = jnp.mean(out * out, axis=-1, keepdims=True)
        out = out * lax.rsqrt(ms + RMS_EPS) * fg_ref[...]
    out_ref[...] = out


def _merge_call(x2d, oa, ob, oc, g, wz, wg, wb, wo, fg, *, last):
    n_tok = x2d.shape[0]
    tm = TM_MERGE
    const2 = lambda i: (0, 0)
    tok = pl.BlockSpec((tm, BRANCH_WIDTH), lambda i: (i, 0))
    return pl.pallas_call(
        functools.partial(_merge_kernel, last=last),
        grid=(n_tok // tm,),
        in_specs=[
            pl.BlockSpec((tm, D_MODEL), lambda i: (i, 0)), tok, tok, tok,
            pl.BlockSpec((1, D_MODEL), const2),
            pl.BlockSpec((D_MODEL, 3 * BRANCH_WIDTH), const2),
            pl.BlockSpec((D_MODEL, 3 * D_MODEL), const2),
            pl.BlockSpec((3, BRANCH_WIDTH, D_MODEL), lambda i: (0, 0, 0)),
            pl.BlockSpec((D_MODEL, D_MODEL), const2),
            pl.BlockSpec((1, D_MODEL), const2),
        ],
        out_specs=pl.BlockSpec((tm, D_MODEL), lambda i: (i, 0)),
        out_shape=jax.ShapeDtypeStruct((n_tok, D_MODEL), F32),
        compiler_params=pltpu.CompilerParams(
            dimension_semantics=("arbitrary",), vmem_limit_bytes=VMEM_LIMIT),
        name="merge",
    )(x2d, oa.reshape(n_tok, BRANCH_WIDTH), ob.reshape(n_tok, BRANCH_WIDTH),
      oc.reshape(n_tok, BRANCH_WIDTH), g, wz, wg, wb, wo, fg)


def _alibi_slopes(n_heads):
    return [2.0 ** (-ALIBI_MAX_EXP * (h + 1) / n_heads) for h in range(n_heads)]


def _alibi_key_bias(seq_len, slopes_per_pair):
    pos = np.arange(seq_len, dtype=np.float64)
    out = np.zeros((seq_len, BRANCH_WIDTH), dtype=BF16)
    for p, pair in enumerate(slopes_per_pair):
        for c, slope in enumerate(pair):
            rest = pos * (slope * LOG2E)
            for x in range(3):
                piece = rest.astype(BF16)
                out[:, p * LANES + 3 * c + x] = piece
                rest = rest - piece.astype(np.float64)
    return jnp.asarray(out)[None]


def _fox_bias_scatter():
    m = [[0.0] * BRANCH_WIDTH for _ in range(3 * LANES)]
    for x in range(3):
        for h in range(N_HEADS):
            m[x * LANES + h][(h // 2) * LANES + 3 * (h % 2) + x] = 1.0
    return jnp.asarray(m, dtype=BF16)


def kernel(x, norm_g, w_in, fox_fb, diff_lam, diff_norm_g, w_branch, w_out, final_g):
    batch, seq_len, _ = x.shape
    depth = norm_g.shape[0]
    n_tok = batch * seq_len
    scale = HEAD_DIM ** -0.5 * LOG2E

    w = BRANCH_WIDTH
    off = {}
    pos = 0
    for name, width in (("diff_q", w), ("diff_k", w), ("diff_v", w), ("diff_z", w),
                        ("fox_q", w), ("fox_k", w), ("fox_v", w), ("fox_f", N_HEADS),
                        ("fox_z", w), ("dil_q", w), ("dil_k", w), ("dil_v", w),
                        ("dil_z", w), ("merge_g", 3 * D_MODEL)):
        off[name] = (pos, pos + width)
        pos += width

    def cols(wl, name, mult=1.0):
        a, b = off[name]
        return wl[:, a:b] * mult if mult != 1.0 else wl[:, a:b]

    ds = _alibi_slopes(DIFF_HEADS)
    cs = _alibi_slopes(N_HEADS)
    diff_kb = _alibi_key_bias(seq_len, [(ds[h], ds[h]) for h in range(DIFF_HEADS)])
    dil_kb = _alibi_key_bias(seq_len, [(cs[2 * p], cs[2 * p + 1]) for p in range(N_PAIRS)])
    pmat = _fox_bias_scatter()
    ones_g = jnp.ones((1, BRANCH_WIDTH), F32)
    zero_lam = jnp.zeros((4, LANES), F32)

    x2d = x.reshape(n_tok, D_MODEL)
    for l in range(depth):
        wl = w_in[l]
        w_qkv = jnp.concatenate(
            [cols(wl, "diff_q", scale), cols(wl, "diff_k"), cols(wl, "diff_v"),
             cols(wl, "fox_q", scale), cols(wl, "fox_k"), cols(wl, "fox_v"),
             cols(wl, "dil_q", scale), cols(wl, "dil_k"), cols(wl, "dil_v")],
            axis=1).astype(BF16)
        w_f = jnp.pad(cols(wl, "fox_f"), ((0, 0), (0, LANES - N_HEADS))).astype(BF16)
        fb = jnp.pad(fox_fb[l].astype(F32), (0, LANES - N_HEADS))[None]
        g = norm_g[l].astype(F32)[None]

        (dqT, dk, dvT, fqT, fk, fvT, fkb, cq, ck, cv) = _proj_call(
            x2d, g, w_qkv, w_f, fb, pmat, seq_len=seq_len)

        lam_init = 0.8 - 0.6 * math.exp(-0.3 * l)
        lam_pad = jnp.pad(diff_lam[l].astype(F32), ((0, 0), (0, LANES - HEAD_DIM)))
        oa = _pair_attn_call(dqT, dk, diff_kb, dvT, lam_pad,
                             diff_norm_g[l].astype(F32)[None],
                             batch=batch, seq_len=seq_len, diff=True, lam_init=lam_init)
        ob = _pair_attn_call(fqT, fk, fkb.reshape(batch, seq_len, BRANCH_WIDTH), fvT,
                             zero_lam, ones_g,
                             batch=batch, seq_len=seq_len, diff=False, lam_init=0.0)
        oc = _dil_call(cq, ck, cv, dil_kb, batch=batch, seq_len=seq_len)

        wz = jnp.concatenate([cols(wl, "diff_z"), cols(wl, "fox_z"), cols(wl, "dil_z")],
                             axis=1).astype(BF16)
        wg = cols(wl, "merge_g").astype(BF16)
        x2d = _merge_call(x2d, oa, ob, oc, g, wz, wg, w_branch[l].astype(BF16),
                          w_out[l].astype(BF16), final_g.astype(F32)[None],
                          last=(l == depth - 1))
    return x2d.reshape(batch, seq_len, D_MODEL)
```

```python
import functools
import math

import jax
import jax.numpy as jnp
import numpy as np
from jax import lax
from jax.experimental import pallas as pl
from jax.experimental.pallas import tpu as pltpu

F32 = jnp.float32
BF16 = jnp.bfloat16

D_MODEL = 1024
HEAD_DIM = 64
BRANCH_WIDTH = 512
LANES = 128
N_PAIRS = BRANCH_WIDTH // LANES
DIFF_HEADS = 4
N_HEADS = 8
DIL_PATTERNS = ((128, 1), (512, 4), (2048, 16))
DIL_BLOCK = 128
DIL_TQ = 256
DIL_GROUP = 2
RMS_EPS = 1e-6
LOG2E = math.log2(math.e)
ALIBI_MAX_EXP = 8.0
N_QKV_GROUPS = 9
TQ = 512
TK = 512
KV_SUBS = 2
TM_PROJ = 512
TM_MERGE = 512
VMEM_LIMIT = 56 * 1024 * 1024


def _split3(a):
    a1 = a.astype(BF16)
    r1 = a - a1.astype(F32)
    a2 = r1.astype(BF16)
    a3 = (r1 - a2.astype(F32)).astype(BF16)
    return a1, a2, a3


def _proj_kernel(x_ref, g_ref, w_ref, wf_ref, fb_ref, pmat_ref,
                 dqT_ref, dk_ref, dvT_ref, fqT_ref, fk_ref, fvT_ref, fkb_ref,
                 cq_ref, ck_ref, cv_ref, cq4_ref, ck4_ref, cv4_ref, cq16_ref, ck16_ref, cv16_ref,
                 carry_ref, stage_ref, *, tiles_per_seq):
    i = pl.program_id(0)
    tm = x_ref.shape[0]
    x = x_ref[...]
    ms = jnp.mean(x * x, axis=-1, keepdims=True)
    h = (x * lax.rsqrt(ms + RMS_EPS) * g_ref[...]).astype(BF16)

    def proj(g):
        return jnp.dot(h, w_ref[:, g * BRANCH_WIDTH:(g + 1) * BRANCH_WIDTH],
                       preferred_element_type=F32)

    def put_t(ref, val):
        for t in range(tm // TQ):
            ref[t] = val[t * TQ:(t + 1) * TQ, :].T.astype(BF16)

    put_t(dqT_ref, proj(0))
    dk_ref[...] = proj(1).astype(BF16)
    put_t(dvT_ref, proj(2))
    put_t(fqT_ref, proj(3))
    fk_ref[...] = proj(4).astype(BF16)
    put_t(fvT_ref, proj(5))
    def put_dil(nat_ref, subs, val):
        nat_ref[...] = val.astype(BF16)
        n_chunks = BRANCH_WIDTH // LANES
        for c in range(n_chunks):
            stage_ref[c] = val[:, c * LANES:(c + 1) * LANES]
        for d, ref in subs:
            for r in range(d):
                ref[0, r] = jnp.concatenate(
                    [stage_ref[c, pl.ds(r, tm // d, stride=d), :] for c in range(n_chunks)],
                    axis=1).astype(BF16)

    put_dil(cq_ref, ((4, cq4_ref), (16, cq16_ref)), proj(6))
    put_dil(ck_ref, ((4, ck4_ref), (16, ck16_ref)), proj(7))
    put_dil(cv_ref, ((4, cv4_ref), (16, cv16_ref)), proj(8))

    f = jnp.dot(h, wf_ref[...], preferred_element_type=F32) + fb_ref[...]
    logf = jnp.minimum(f, 0.0) - jnp.log1p(jnp.exp(-jnp.abs(f)))

    @pl.when(i % tiles_per_seq == 0)
    def _():
        carry_ref[...] = jnp.zeros_like(carry_ref)

    row = lax.broadcasted_iota(jnp.int32, (tm, tm), 0)
    col = lax.broadcasted_iota(jnp.int32, (tm, tm), 1)
    tri = jnp.where(row >= col, 1.0, 0.0).astype(BF16)
    a1, a2, a3 = _split3(logf)
    c = (jnp.dot(tri, a1, preferred_element_type=F32)
         + jnp.dot(tri, a2, preferred_element_type=F32)
         + jnp.dot(tri, a3, preferred_element_type=F32)) + carry_ref[0:1, :]
    carry_ref[0:1, :] = c[tm - 1:tm, :]
    b1, b2, b3 = _split3(c * (-LOG2E))
    kb = jnp.dot(jnp.concatenate([b1, b2, b3], axis=1), pmat_ref[...],
                 preferred_element_type=F32)
    fkb_ref[...] = kb.astype(BF16)


def _proj_call(x2d, g, w_qkv, w_f, fb, pmat, *, seq_len):
    n_tok = x2d.shape[0]
    tm = TM_PROJ
    nt = n_tok // tm
    const = lambda i: (0, 0)
    tok = pl.BlockSpec((tm, BRANCH_WIDTH), lambda i: (i, 0))
    tr = pl.BlockSpec((tm // TQ, BRANCH_WIDTH, TQ), lambda i: (i, 0, 0))
    nat_shape = jax.ShapeDtypeStruct((n_tok, BRANCH_WIDTH), BF16)
    tr_shape = jax.ShapeDtypeStruct((n_tok // TQ, BRANCH_WIDTH, TQ), BF16)
    tps = seq_len // tm
    batch = n_tok // seq_len

    def sub(d):
        spec = pl.BlockSpec((1, d, tm // d, BRANCH_WIDTH), lambda i: (i // tps, 0, i % tps, 0))
        shape = jax.ShapeDtypeStruct((batch, d, seq_len // d, BRANCH_WIDTH), BF16)
        return spec, shape

    (s4, s4_shape), (s16, s16_shape) = sub(4), sub(16)
    return pl.pallas_call(
        functools.partial(_proj_kernel, tiles_per_seq=seq_len // tm),
        grid=(nt,),
        in_specs=[
            pl.BlockSpec((tm, D_MODEL), lambda i: (i, 0)),
            pl.BlockSpec((1, D_MODEL), const),
            pl.BlockSpec((D_MODEL, N_QKV_GROUPS * BRANCH_WIDTH), const),
            pl.BlockSpec((D_MODEL, LANES), const),
            pl.BlockSpec((1, LANES), const),
            pl.BlockSpec((3 * LANES, BRANCH_WIDTH), const),
        ],
        out_specs=[tr, tok, tr, tr, tok, tr, tok, tok, tok, tok, s4, s4, s4, s16, s16, s16],
        out_shape=[tr_shape, nat_shape, tr_shape, tr_shape, nat_shape, tr_shape,
                   nat_shape, nat_shape, nat_shape, nat_shape,
                   s4_shape, s4_shape, s4_shape, s16_shape, s16_shape, s16_shape],
        scratch_shapes=[pltpu.VMEM((8, LANES), F32),
                        pltpu.VMEM((BRANCH_WIDTH // LANES, tm, LANES), F32)],
        compiler_params=pltpu.CompilerParams(
            dimension_semantics=("arbitrary",), vmem_limit_bytes=VMEM_LIMIT),
        name="proj",
    )(x2d, g, w_qkv, w_f, fb, pmat)


def _pair_attn_kernel(qT_ref, qTn_ref, k_ref, kb_ref, vT_ref, lam_ref, g_ref, o_ref,
                      qa_sc, m_sc, l_sc, acc_sc, s_sc, t0_sc, smax_sc, t0max_sc,
                      *, diff, lam_init, max_full):
    i = pl.program_id(2)
    par = i % 2
    row = lax.broadcasted_iota(jnp.int32, (LANES, TQ), 0)

    def build_qa(q_ref, slot):
        qT = q_ref[0, 0].astype(F32)
        qa_sc[slot, :, 0:TQ] = jnp.concatenate(
            [jnp.where(row < HEAD_DIM, qT, 0.0), jnp.where(row < 3, 1.0, 0.0)],
            axis=0).astype(BF16)
        qa_sc[slot, :, TQ:2 * TQ] = jnp.concatenate(
            [jnp.where(row >= HEAD_DIM, qT, 0.0),
             jnp.where((row >= 3) & (row < 6), 1.0, 0.0)], axis=0).astype(BF16)

    tile_keys = KV_SUBS * TK

    def score_tile(t, qa_slot, dst, dst_max, dst_slot):
        base = pl.multiple_of(t * tile_keys, tile_keys)
        ka = jnp.concatenate([k_ref[0, pl.ds(base, tile_keys), :],
                              kb_ref[0, pl.ds(base, tile_keys), :]], axis=1)
        s = jnp.dot(ka, qa_sc[qa_slot], preferred_element_type=F32)
        dst[dst_slot] = s
        dst_max[dst_slot] = jnp.max(s, axis=0, keepdims=True)

    @pl.when(i == 0)
    def _():
        build_qa(qT_ref, 0)
        score_tile(0, 0, t0_sc, t0max_sc, 0)

    build_qa(qTn_ref, 1 - par)
    m_sc[...] = jnp.full(m_sc.shape, -jnp.inf, F32)
    l_sc[...] = jnp.zeros(l_sc.shape, F32)
    acc_sc[...] = jnp.zeros(acc_sc.shape, F32)

    def consume(s, sub0, n_sub, tile_max):
        rows = n_sub * TK
        if tile_max is None:
            kr = lax.broadcasted_iota(jnp.int32, (TK, 2 * TQ), 0)
            qc = lax.broadcasted_iota(jnp.int32, (TK, 2 * TQ), 1) % TQ
            tail = jnp.where(kr <= qc, s[rows - TK:], -jnp.inf)
            s = tail if n_sub == 1 else jnp.concatenate([s[:rows - TK], tail], axis=0)
            tile_max = jnp.max(s, axis=0, keepdims=True)
        m_old = m_sc[...]
        m_new = jnp.maximum(m_old, tile_max)
        alpha = jnp.exp2(m_old - m_new)
        p = jnp.exp2(s - m_new)
        l_sc[...] = alpha * l_sc[...] + jnp.sum(p, axis=0, keepdims=True)
        pb = p.astype(BF16)
        pv = jnp.dot(vT_ref[0, sub0], pb[0:TK], preferred_element_type=F32)
        for t in range(1, n_sub):
            pv = pv + jnp.dot(vT_ref[0, sub0 + t], pb[t * TK:(t + 1) * TK],
                              preferred_element_type=F32)
        acc_sc[...] = alpha * acc_sc[...] + pv
        m_sc[...] = m_new

    def full_block(cur, cur_max, cur_slot, sub0, t_next):
        m_old = m_sc[...]
        m_new = jnp.maximum(m_old, cur_max[cur_slot])
        alpha = jnp.exp2(m_old - m_new)
        nslot = t_next % 2
        lsum = pv = nmax = None
        for t in range(KV_SUBS):
            kbase = pl.multiple_of(t_next * tile_keys + t * TK, TK)
            ka = jnp.concatenate([k_ref[0, pl.ds(kbase, TK), :],
                                  kb_ref[0, pl.ds(kbase, TK), :]], axis=1)
            sn = jnp.dot(ka, qa_sc[par], preferred_element_type=F32)
            s_sc[nslot, t * TK:(t + 1) * TK, :] = sn
            cm = jnp.max(sn, axis=0, keepdims=True)
            nmax = cm if nmax is None else jnp.maximum(nmax, cm)
            p = jnp.exp2(cur[cur_slot, t * TK:(t + 1) * TK, :] - m_new)
            ls = jnp.sum(p, axis=0, keepdims=True)
            lsum = ls if lsum is None else lsum + ls
            d = jnp.dot(vT_ref[0, sub0 + t], p.astype(BF16), preferred_element_type=F32)
            pv = d if pv is None else pv + d
        smax_sc[nslot] = nmax
        l_sc[...] = alpha * l_sc[...] + lsum
        acc_sc[...] = alpha * acc_sc[...] + pv
        m_sc[...] = m_new

    n_full = i // KV_SUBS
    rem = i % KV_SUBS
    for j in range(max_full):
        @pl.when(j < n_full)
        def _(j=j):
            if j == 0:
                full_block(t0_sc, t0max_sc, par, 0, j + 1)
            else:
                full_block(s_sc, smax_sc, j % 2, j * KV_SUBS, j + 1)
    for r in range(KV_SUBS):
        for diag_is_tile0 in (True, False):
            @pl.when((rem == r) & ((n_full == 0) == diag_is_tile0))
            def _(r=r, diag_is_tile0=diag_is_tile0):
                score_tile(0, 1 - par, t0_sc, t0max_sc, (r + 1) % 2)
                rows = (r + 1) * TK
                s = t0_sc[r % 2, 0:rows, :] if diag_is_tile0 else s_sc[n_full % 2, 0:rows, :]
                consume(s, n_full * KV_SUBS, r + 1, None)

    a = acc_sc[...] / l_sc[...]
    a0 = a[:, 0:TQ]
    a1 = a[:, TQ:2 * TQ]
    if diff:
        dl = lam_ref[...]
        lam = (jnp.exp(jnp.sum(dl[0:1] * dl[1:2], axis=1, keepdims=True))
               - jnp.exp(jnp.sum(dl[2:3] * dl[3:4], axis=1, keepdims=True)) + lam_init)
        oT = a0 - lam * a1
        ms = jnp.mean(oT * oT, axis=0, keepdims=True)
        oT = oT * lax.rsqrt(ms + RMS_EPS)
        o_ref[0] = (oT.T * g_ref[...]) * (1.0 - lam_init)
    else:
        o_ref[0] = jnp.where(row < HEAD_DIM, a0, a1).T


def _pair_attn_call(qT, k, kb, vT, lam_pad, g, *, batch, seq_len, diff, lam_init):
    nq = seq_len // TQ
    nk = seq_len // TK
    qT = qT.reshape(batch, nq, BRANCH_WIDTH, TQ)
    vT = vT.reshape(batch, nk, BRANCH_WIDTH, TK)
    k = k.reshape(batch, seq_len, BRANCH_WIDTH)
    kb_batched = kb.shape[0] != 1
    kb_map = (lambda b, p, i: (b, 0, p)) if kb_batched else (lambda b, p, i: (0, 0, p))
    return pl.pallas_call(
        functools.partial(_pair_attn_kernel, diff=diff, lam_init=lam_init,
                          max_full=seq_len // (KV_SUBS * TK) - 1),
        grid=(batch, N_PAIRS, nq),
        in_specs=[
            pl.BlockSpec((1, 1, LANES, TQ), lambda b, p, i: (b, i, p, 0)),
            pl.BlockSpec((1, 1, LANES, TQ),
                         lambda b, p, i: (b, jnp.minimum(i + 1, nq - 1), p, 0)),
            pl.BlockSpec((1, seq_len, LANES), lambda b, p, i: (b, 0, p)),
            pl.BlockSpec((1, seq_len, LANES), kb_map),
            pl.BlockSpec((1, nk, LANES, TK), lambda b, p, i: (b, 0, p, 0)),
            pl.BlockSpec((4, LANES), lambda b, p, i: (0, 0)),
            pl.BlockSpec((1, LANES), lambda b, p, i: (0, p)),
        ],
        out_specs=pl.BlockSpec((1, TQ, LANES), lambda b, p, i: (b, i, p)),
        out_shape=jax.ShapeDtypeStruct((batch, seq_len, BRANCH_WIDTH), F32),
        scratch_shapes=[
            pltpu.VMEM((2, 2 * LANES, 2 * TQ), BF16),
            pltpu.VMEM((1, 2 * TQ), F32),
            pltpu.VMEM((1, 2 * TQ), F32),
            pltpu.VMEM((LANES, 2 * TQ), F32),
            pltpu.VMEM((2, KV_SUBS * TK, 2 * TQ), F32),
            pltpu.VMEM((2, KV_SUBS * TK, 2 * TQ), F32),
            pltpu.VMEM((2, 1, 2 * TQ), F32),
            pltpu.VMEM((2, 1, 2 * TQ), F32),
        ],
        compiler_params=pltpu.CompilerParams(
            dimension_semantics=("arbitrary", "arbitrary", "arbitrary"),
            vmem_limit_bytes=VMEM_LIMIT),
        name="diff_attn" if diff else "fox_attn",
    )(qT, qT, k, kb, vT, lam_pad, g)


def _dil_kernel(q1_ref, k1_ref, v1_ref, kb1_ref, q4_ref, k4_ref, v4_ref, kb4_ref,
                q16_ref, k16_ref, v16_ref, kb16_ref, o_ref,
                acc, m_sc, l_sc, mask_sc, s_buf, vT_buf, smax_buf, *, seq_len):
    L = DIL_BLOCK
    T = DIL_TQ
    acc[...] = jnp.zeros(acc.shape, F32)
    m_sc[...] = jnp.full(m_sc.shape, -jnp.inf, F32)
    l_sc[...] = jnp.zeros(l_sc.shape, F32)

    kr = lax.broadcasted_iota(jnp.int32, (2 * L, 2 * L), 0)
    qc = lax.broadcasted_iota(jnp.int32, (2 * L, 2 * L), 1) % L
    band = (kr >= qc) & (kr <= qc + L)
    mask_sc[0] = jnp.where(band & (kr >= L), 0.0, -jnp.inf)
    mask_sc[1] = jnp.where(band, 0.0, -jnp.inf)

    lane = lax.broadcasted_iota(jnp.int32, (T, LANES), 1)
    first = lane < HEAD_DIM
    sel0 = jnp.where(lane < 3, 1.0, 0.0).astype(BF16)
    sel1 = jnp.where((lane >= 3) & (lane < 6), 1.0, 0.0).astype(BF16)
    head0 = jnp.where(first, 1.0, 0.0).astype(BF16)
    head1 = jnp.where(first, 0.0, 1.0).astype(BF16)
    frow = lax.broadcasted_iota(jnp.int32, (LANES, L), 0) < HEAD_DIM
    n_qb = T // L

    operands = ((q1_ref, k1_ref, v1_ref, kb1_ref), (q4_ref, k4_ref, v4_ref, kb4_ref),
                (q16_ref, k16_ref, v16_ref, kb16_ref))
    for (_, d), (q_ref, k_ref, v_ref, kb_ref) in zip(DIL_PATTERNS, operands):
        nt = seq_len // (T * d)

        n_groups = seq_len // (T * DIL_GROUP)

        def addr(u, g, d=d, nt=nt):
            t = u * DIL_GROUP + g
            r = t // nt
            it = t % nt
            start = pl.multiple_of(r * (seq_len // d) + T * it, L)
            prev = pl.multiple_of(jnp.where(it == 0, start, start - L), L)
            state_rows = pl.ds(r + d * T * it, T, stride=d)
            return pl.ds(start, T), pl.ds(prev, L), jnp.minimum(it, 1), state_rows

        def score_group(u, buf):
            for g in range(DIL_GROUP):
                rows, prows, slot, _ = addr(u, g)
                qb = q_ref[0, rows, :]
                q0 = jnp.concatenate([qb * head0, sel0], axis=1)
                q1 = jnp.concatenate([qb * head1, sel1], axis=1)
                ka = jnp.concatenate(
                    [jnp.concatenate([k_ref[0, prows, :], kb_ref[0, prows, :]], axis=1),
                     jnp.concatenate([k_ref[0, rows, :], kb_ref[0, rows, :]], axis=1)],
                    axis=0)
                for b in range(n_qb):
                    qa = jnp.concatenate([q0[b * L:(b + 1) * L], q1[b * L:(b + 1) * L]], axis=0)
                    sb = lax.dot_general(ka[b * L:(b + 2) * L], qa, (((1,), (1,)), ((), ())),
                                         preferred_element_type=F32)
                    sb = sb + mask_sc[slot if b == 0 else 1]
                    s_buf[buf, g, b] = sb
                    smax_buf[buf, g, b] = jnp.max(sb, axis=0, keepdims=True)
                vv = jnp.concatenate([v_ref[0, prows, :], v_ref[0, rows, :]], axis=0)
                vT_buf[buf, g] = vv.astype(F32).T.astype(BF16)

        def consume_group(u, buf):
            num, m_blk, l_blk = [], [], []
            for g in range(DIL_GROUP):
                nb_, mb_, lb_ = [], [], []
                for b in range(n_qb):
                    mg = smax_buf[buf, g, b]
                    p = jnp.exp2(s_buf[buf, g, b] - mg)
                    lg = jnp.sum(p, axis=0, keepdims=True)
                    oT = jnp.dot(vT_buf[buf, g, :, b * L:(b + 2) * L], p.astype(BF16),
                                 preferred_element_type=F32)
                    nb_.append(jnp.where(frow, oT[:, :L], oT[:, L:]))
                    mb_.append(jnp.where(frow, jnp.broadcast_to(mg[:, :L], (LANES, L)),
                                         jnp.broadcast_to(mg[:, L:], (LANES, L))))
                    lb_.append(jnp.where(frow, jnp.broadcast_to(lg[:, :L], (LANES, L)),
                                         jnp.broadcast_to(lg[:, L:], (LANES, L))))
                num.append(jnp.concatenate(nb_, axis=1).T)
                m_blk.append(jnp.concatenate(mb_, axis=1).T)
                l_blk.append(jnp.concatenate(lb_, axis=1).T)
            for g in range(DIL_GROUP):
                _, _, _, rows = addr(u, g)
                m_old = m_sc[rows, :]
                m_new = jnp.maximum(m_old, m_blk[g])
                w_old = jnp.exp2(m_old - m_new)
                w_blk = jnp.exp2(m_blk[g] - m_new)
                l_sc[rows, :] = w_old * l_sc[rows, :] + w_blk * l_blk[g]
                acc[rows, :] = w_old * acc[rows, :] + w_blk * num[g]
                m_sc[rows, :] = m_new

        score_group(0, 0)

        def trip(v, carry):
            score_group(2 * v + 1, 1)
            consume_group(2 * v, 0)
            score_group(jnp.minimum(2 * v + 2, n_groups - 1), 0)
            consume_group(2 * v + 1, 1)
            return carry

        lax.fori_loop(0, n_groups // 2, trip, 0)

    o_ref[0] = acc[...] / l_sc[...]


def _dil_call(operands, *, batch, seq_len):
    assert seq_len % (DIL_TQ * max(d for _, d in DIL_PATTERNS)) == 0
    assert seq_len % (DIL_TQ * DIL_GROUP * 2) == 0
    blk = pl.BlockSpec((1, seq_len, LANES), lambda b, p: (b, 0, p))
    kb_blk = pl.BlockSpec((1, seq_len, LANES), lambda b, p: (0, 0, p))
    args, specs = [], []
    for q, k, v, kb in operands:
        args += [q.reshape(batch, seq_len, BRANCH_WIDTH), k.reshape(batch, seq_len, BRANCH_WIDTH),
                 v.reshape(batch, seq_len, BRANCH_WIDTH), kb]
        specs += [blk, blk, blk, kb_blk]
    n_qb = DIL_TQ // DIL_BLOCK
    return pl.pallas_call(
        functools.partial(_dil_kernel, seq_len=seq_len),
        grid=(batch, N_PAIRS),
        in_specs=specs,
        out_specs=blk,
        out_shape=jax.ShapeDtypeStruct((batch, seq_len, BRANCH_WIDTH), F32),
        scratch_shapes=[pltpu.VMEM((seq_len, LANES), F32) for _ in range(3)]
        + [pltpu.VMEM((2, 2 * DIL_BLOCK, 2 * DIL_BLOCK), F32),
           pltpu.VMEM((2, DIL_GROUP, n_qb, 2 * DIL_BLOCK, 2 * DIL_BLOCK), F32),
           pltpu.VMEM((2, DIL_GROUP, LANES, DIL_TQ + DIL_BLOCK), BF16),
           pltpu.VMEM((2, DIL_GROUP, n_qb, 1, 2 * DIL_BLOCK), F32)],
        compiler_params=pltpu.CompilerParams(
            dimension_semantics=("arbitrary", "arbitrary"), vmem_limit_bytes=VMEM_LIMIT),
        name="dil_attn",
    )(*args)


def _merge_kernel(x_ref, oa_ref, ob_ref, oc_ref, g_ref, wz_ref, wg_ref, wb_ref, wo_ref,
                  fg_ref, out_ref, *, last):
    x = x_ref[...]
    ms = jnp.mean(x * x, axis=-1, keepdims=True)
    h = (x * lax.rsqrt(ms + RMS_EPS) * g_ref[...]).astype(BF16)
    merged = None
    for n, o_ref in enumerate((oa_ref, ob_ref, oc_ref)):
        z = jnp.dot(h, wz_ref[:, n * BRANCH_WIDTH:(n + 1) * BRANCH_WIDTH],
                    preferred_element_type=F32)
        y = (o_ref[...] * (z * jax.nn.sigmoid(z))).astype(BF16)
        br = jnp.dot(y, wb_ref[n], preferred_element_type=F32)
        gate = jax.nn.sigmoid(jnp.dot(h, wg_ref[:, n * D_MODEL:(n + 1) * D_MODEL],
                                      preferred_element_type=F32))
        merged = gate * br if merged is None else merged + gate * br
    out = x + jnp.dot(merged.astype(BF16), wo_ref[...], preferred_element_type=F32)
    if last:
        ms = jnp.mean(out * out, axis=-1, keepdims=True)
        out = out * lax.rsqrt(ms + RMS_EPS) * fg_ref[...]
    out_ref[...] = out


def _merge_call(x2d, oa, ob, oc, g, wz, wg, wb, wo, fg, *, last):
    n_tok = x2d.shape[0]
    tm = TM_MERGE
    const2 = lambda i: (0, 0)
    tok = pl.BlockSpec((tm, BRANCH_WIDTH), lambda i: (i, 0))
    return pl.pallas_call(
        functools.partial(_merge_kernel, last=last),
        grid=(n_tok // tm,),
        in_specs=[
            pl.BlockSpec((tm, D_MODEL), lambda i: (i, 0)), tok, tok, tok,
            pl.BlockSpec((1, D_MODEL), const2),
            pl.BlockSpec((D_MODEL, 3 * BRANCH_WIDTH), const2),
            pl.BlockSpec((D_MODEL, 3 * D_MODEL), const2),
            pl.BlockSpec((3, BRANCH_WIDTH, D_MODEL), lambda i: (0, 0, 0)),
            pl.BlockSpec((D_MODEL, D_MODEL), const2),
            pl.BlockSpec((1, D_MODEL), const2),
        ],
        out_specs=pl.BlockSpec((tm, D_MODEL), lambda i: (i, 0)),
        out_shape=jax.ShapeDtypeStruct((n_tok, D_MODEL), F32),
        compiler_params=pltpu.CompilerParams(
            dimension_semantics=("arbitrary",), vmem_limit_bytes=VMEM_LIMIT),
        name="merge",
    )(x2d, oa.reshape(n_tok, BRANCH_WIDTH), ob.reshape(n_tok, BRANCH_WIDTH),
      oc.reshape(n_tok, BRANCH_WIDTH), g, wz, wg, wb, wo, fg)


def _alibi_slopes(n_heads):
    return [2.0 ** (-ALIBI_MAX_EXP * (h + 1) / n_heads) for h in range(n_heads)]


def _alibi_key_bias(seq_len, slopes_per_pair, dilation=1):
    pos = np.arange(seq_len, dtype=np.float64).reshape(-1, dilation).T.reshape(-1)
    out = np.zeros((seq_len, BRANCH_WIDTH), dtype=BF16)
    for p, pair in enumerate(slopes_per_pair):
        for c, slope in enumerate(pair):
            rest = pos * (slope * LOG2E)
            for x in range(3):
                piece = rest.astype(BF16)
                out[:, p * LANES + 3 * c + x] = piece
                rest = rest - piece.astype(np.float64)
    return jnp.asarray(out)[None]


def _fox_bias_scatter():
    m = [[0.0] * BRANCH_WIDTH for _ in range(3 * LANES)]
    for x in range(3):
        for h in range(N_HEADS):
            m[x * LANES + h][(h // 2) * LANES + 3 * (h % 2) + x] = 1.0
    return jnp.asarray(m, dtype=BF16)


def kernel(x, norm_g, w_in, fox_fb, diff_lam, diff_norm_g, w_branch, w_out, final_g):
    batch, seq_len, _ = x.shape
    depth = norm_g.shape[0]
    n_tok = batch * seq_len
    scale = HEAD_DIM ** -0.5 * LOG2E

    w = BRANCH_WIDTH
    off = {}
    pos = 0
    for name, width in (("diff_q", w), ("diff_k", w), ("diff_v", w), ("diff_z", w),
                        ("fox_q", w), ("fox_k", w), ("fox_v", w), ("fox_f", N_HEADS),
                        ("fox_z", w), ("dil_q", w), ("dil_k", w), ("dil_v", w),
                        ("dil_z", w), ("merge_g", 3 * D_MODEL)):
        off[name] = (pos, pos + width)
        pos += width

    def cols(wl, name, mult=1.0):
        a, b = off[name]
        return wl[:, a:b] * mult if mult != 1.0 else wl[:, a:b]

    ds = _alibi_slopes(DIFF_HEADS)
    cs = _alibi_slopes(N_HEADS)
    diff_kb = _alibi_key_bias(seq_len, [(ds[h], ds[h]) for h in range(DIFF_HEADS)])
    dil_pairs = [(cs[2 * p], cs[2 * p + 1]) for p in range(N_PAIRS)]
    dil_kb = [_alibi_key_bias(seq_len, dil_pairs, d) for _, d in DIL_PATTERNS]
    pmat = _fox_bias_scatter()
    ones_g = jnp.ones((1, BRANCH_WIDTH), F32)
    zero_lam = jnp.zeros((4, LANES), F32)

    x2d = x.reshape(n_tok, D_MODEL)
    for l in range(depth):
        wl = w_in[l]
        w_qkv = jnp.concatenate(
            [cols(wl, "diff_q", scale), cols(wl, "diff_k"), cols(wl, "diff_v"),
             cols(wl, "fox_q", scale), cols(wl, "fox_k"), cols(wl, "fox_v"),
             cols(wl, "dil_q", scale), cols(wl, "dil_k"), cols(wl, "dil_v")],
            axis=1).astype(BF16)
        w_f = jnp.pad(cols(wl, "fox_f"), ((0, 0), (0, LANES - N_HEADS))).astype(BF16)
        fb = jnp.pad(fox_fb[l].astype(F32), (0, LANES - N_HEADS))[None]
        g = norm_g[l].astype(F32)[None]

        (dqT, dk, dvT, fqT, fk, fvT, fkb, cq, ck, cv, cq4, ck4, cv4, cq16, ck16, cv16) = _proj_call(
            x2d, g, w_qkv, w_f, fb, pmat, seq_len=seq_len)

        lam_init = 0.8 - 0.6 * math.exp(-0.3 * l)
        lam_pad = jnp.pad(diff_lam[l].astype(F32), ((0, 0), (0, LANES - HEAD_DIM)))
        oa = _pair_attn_call(dqT, dk, diff_kb, dvT, lam_pad,
                             diff_norm_g[l].astype(F32)[None],
                             batch=batch, seq_len=seq_len, diff=True, lam_init=lam_init)
        ob = _pair_attn_call(fqT, fk, fkb.reshape(batch, seq_len, BRANCH_WIDTH), fvT,
                             zero_lam, ones_g,
                             batch=batch, seq_len=seq_len, diff=False, lam_init=0.0)
        oc = _dil_call(((cq, ck, cv, dil_kb[0]), (cq4, ck4, cv4, dil_kb[1]),
                        (cq16, ck16, cv16, dil_kb[2])), batch=batch, seq_len=seq_len)

        wz = jnp.concatenate([cols(wl, "diff_z"), cols(wl, "fox_z"), cols(wl, "dil_z")],
                             axis=1).astype(BF16)
        wg = cols(wl, "merge_g").astype(BF16)
        x2d = _merge_call(x2d, oa, ob, oc, g, wz, wg, w_branch[l].astype(BF16),
                          w_out[l].astype(BF16), final_g.astype(F32)[None],
                          last=(l == depth - 1))
    return x2d.reshape(batch, seq_len, D_MODEL)
```

```python
import functools
import math

import jax
import jax.numpy as jnp
import numpy as np
from jax import lax
from jax.experimental import pallas as pl
from jax.experimental.pallas import tpu as pltpu

F32 = jnp.float32
BF16 = jnp.bfloat16

D_MODEL = 1024
HEAD_DIM = 64
BRANCH_WIDTH = 512
LANES = 128
N_PAIRS = BRANCH_WIDTH // LANES
DIFF_HEADS = 4
N_HEADS = 8
DIL_PATTERNS = ((128, 1), (512, 4), (2048, 16))
DIL_BLOCK = 128
DIL_TQ = 256
DIL_GROUP = 2
RMS_EPS = 1e-6
LOG2E = math.log2(math.e)
ALIBI_MAX_EXP = 8.0
N_QKV_GROUPS = 9
TQ = 512
TK = 512
KV_SUBS = 2
TM_PROJ = 512
TM_MERGE = 512
VMEM_LIMIT = 56 * 1024 * 1024


def _split3(a):
    a1 = a.astype(BF16)
    r1 = a - a1.astype(F32)
    a2 = r1.astype(BF16)
    a3 = (r1 - a2.astype(F32)).astype(BF16)
    return a1, a2, a3


def _proj_kernel(x_ref, g_ref, w_ref, wf_ref, fb_ref, pmat_ref,
                 dqT_ref, dk_ref, dvT_ref, fqT_ref, fk_ref, fvT_ref, fkb_ref,
                 cq_ref, ck_ref, cv_ref, carry_ref, *, tiles_per_seq):
    i = pl.program_id(0)
    tm = x_ref.shape[0]
    x = x_ref[...]
    ms = jnp.mean(x * x, axis=-1, keepdims=True)
    h = (x * lax.rsqrt(ms + RMS_EPS) * g_ref[...]).astype(BF16)

    def proj(g):
        return jnp.dot(h, w_ref[:, g * BRANCH_WIDTH:(g + 1) * BRANCH_WIDTH],
                       preferred_element_type=F32)

    def put_t(ref, val):
        for t in range(tm // TQ):
            ref[t] = val[t * TQ:(t + 1) * TQ, :].T.astype(BF16)

    put_t(dqT_ref, proj(0))
    dk_ref[...] = proj(1).astype(BF16)
    put_t(dvT_ref, proj(2))
    put_t(fqT_ref, proj(3))
    fk_ref[...] = proj(4).astype(BF16)
    put_t(fvT_ref, proj(5))
    cq_ref[...] = proj(6).astype(BF16)
    ck_ref[...] = proj(7).astype(BF16)
    cv_ref[...] = proj(8).astype(BF16)

    f = jnp.dot(h, wf_ref[...], preferred_element_type=F32) + fb_ref[...]
    logf = jnp.minimum(f, 0.0) - jnp.log1p(jnp.exp(-jnp.abs(f)))

    @pl.when(i % tiles_per_seq == 0)
    def _():
        carry_ref[...] = jnp.zeros_like(carry_ref)

    row = lax.broadcasted_iota(jnp.int32, (tm, tm), 0)
    col = lax.broadcasted_iota(jnp.int32, (tm, tm), 1)
    tri = jnp.where(row >= col, 1.0, 0.0).astype(BF16)
    a1, a2, a3 = _split3(logf)
    c = (jnp.dot(tri, a1, preferred_element_type=F32)
         + jnp.dot(tri, a2, preferred_element_type=F32)
         + jnp.dot(tri, a3, preferred_element_type=F32)) + carry_ref[0:1, :]
    carry_ref[0:1, :] = c[tm - 1:tm, :]
    b1, b2, b3 = _split3(c * (-LOG2E))
    kb = jnp.dot(jnp.concatenate([b1, b2, b3], axis=1), pmat_ref[...],
                 preferred_element_type=F32)
    fkb_ref[...] = kb.astype(BF16)


def _proj_call(x2d, g, w_qkv, w_f, fb, pmat, *, seq_len):
    n_tok = x2d.shape[0]
    tm = TM_PROJ
    nt = n_tok // tm
    const = lambda i: (0, 0)
    tok = pl.BlockSpec((tm, BRANCH_WIDTH), lambda i: (i, 0))
    tr = pl.BlockSpec((tm // TQ, BRANCH_WIDTH, TQ), lambda i: (i, 0, 0))
    nat_shape = jax.ShapeDtypeStruct((n_tok, BRANCH_WIDTH), BF16)
    tr_shape = jax.ShapeDtypeStruct((n_tok // TQ, BRANCH_WIDTH, TQ), BF16)
    return pl.pallas_call(
        functools.partial(_proj_kernel, tiles_per_seq=seq_len // tm),
        grid=(nt,),
        in_specs=[
            pl.BlockSpec((tm, D_MODEL), lambda i: (i, 0)),
            pl.BlockSpec((1, D_MODEL), const),
            pl.BlockSpec((D_MODEL, N_QKV_GROUPS * BRANCH_WIDTH), const),
            pl.BlockSpec((D_MODEL, LANES), const),
            pl.BlockSpec((1, LANES), const),
            pl.BlockSpec((3 * LANES, BRANCH_WIDTH), const),
        ],
        out_specs=[tr, tok, tr, tr, tok, tr, tok, tok, tok, tok],
        out_shape=[tr_shape, nat_shape, tr_shape, tr_shape, nat_shape, tr_shape,
                   nat_shape, nat_shape, nat_shape, nat_shape],
        scratch_shapes=[pltpu.VMEM((8, LANES), F32)],
        compiler_params=pltpu.CompilerParams(
            dimension_semantics=("arbitrary",), vmem_limit_bytes=VMEM_LIMIT),
        name="proj",
    )(x2d, g, w_qkv, w_f, fb, pmat)


def _pair_attn_kernel(qT_ref, qTn_ref, k_ref, kb_ref, vT_ref, lam_ref, g_ref, o_ref,
                      qa_sc, m_sc, l_sc, acc_sc, s_sc, t0_sc, smax_sc, t0max_sc,
                      *, diff, lam_init, max_full):
    i = pl.program_id(2)
    par = i % 2
    row = lax.broadcasted_iota(jnp.int32, (LANES, TQ), 0)

    def build_qa(q_ref, slot):
        qT = q_ref[0, 0].astype(F32)
        qa_sc[slot, :, 0:TQ] = jnp.concatenate(
            [jnp.where(row < HEAD_DIM, qT, 0.0), jnp.where(row < 3, 1.0, 0.0)],
            axis=0).astype(BF16)
        qa_sc[slot, :, TQ:2 * TQ] = jnp.concatenate(
            [jnp.where(row >= HEAD_DIM, qT, 0.0),
             jnp.where((row >= 3) & (row < 6), 1.0, 0.0)], axis=0).astype(BF16)

    tile_keys = KV_SUBS * TK

    def score_rows(row0, n_rows, qa_slot, dst, dst_slot):
        ka = jnp.concatenate([k_ref[0, row0:row0 + n_rows, :],
                              kb_ref[0, row0:row0 + n_rows, :]], axis=1)
        s = jnp.dot(ka, qa_sc[qa_slot], preferred_element_type=F32)
        dst[dst_slot, row0:row0 + n_rows, :] = s
        return jnp.max(s, axis=0, keepdims=True)

    @pl.when(i == 0)
    def _():
        build_qa(qT_ref, 0)
        t0max_sc[0] = score_rows(0, tile_keys, 0, t0_sc, 0)

    build_qa(qTn_ref, 1 - par)
    m_sc[...] = jnp.full(m_sc.shape, -jnp.inf, F32)
    l_sc[...] = jnp.zeros(l_sc.shape, F32)
    acc_sc[...] = jnp.zeros(acc_sc.shape, F32)

    def consume(s, sub0, n_sub, tile_max):
        rows = n_sub * TK
        if tile_max is None:
            kr = lax.broadcasted_iota(jnp.int32, (TK, 2 * TQ), 0)
            qc = lax.broadcasted_iota(jnp.int32, (TK, 2 * TQ), 1) % TQ
            tail = jnp.where(kr <= qc, s[rows - TK:], -jnp.inf)
            s = tail if n_sub == 1 else jnp.concatenate([s[:rows - TK], tail], axis=0)
            tile_max = jnp.max(s, axis=0, keepdims=True)
        m_old = m_sc[...]
        m_new = jnp.maximum(m_old, tile_max)
        alpha = jnp.exp2(m_old - m_new)
        p = jnp.exp2(s - m_new)
        l_sc[...] = alpha * l_sc[...] + jnp.sum(p, axis=0, keepdims=True)
        pb = p.astype(BF16)
        pv = jnp.dot(vT_ref[0, sub0], pb[0:TK], preferred_element_type=F32)
        for t in range(1, n_sub):
            pv = pv + jnp.dot(vT_ref[0, sub0 + t], pb[t * TK:(t + 1) * TK],
                              preferred_element_type=F32)
        acc_sc[...] = alpha * acc_sc[...] + pv
        m_sc[...] = m_new

    def full_block(cur, cur_max, cur_slot, sub0, t_next):
        m_old = m_sc[...]
        m_new = jnp.maximum(m_old, cur_max[cur_slot])
        alpha = jnp.exp2(m_old - m_new)
        nslot = t_next % 2
        lsum = pv = nmax = None
        for t in range(KV_SUBS):
            kbase = pl.multiple_of(t_next * tile_keys + t * TK, TK)
            ka = jnp.concatenate([k_ref[0, pl.ds(kbase, TK), :],
                                  kb_ref[0, pl.ds(kbase, TK), :]], axis=1)
            sn = jnp.dot(ka, qa_sc[par], preferred_element_type=F32)
            s_sc[nslot, t * TK:(t + 1) * TK, :] = sn
            cm = jnp.max(sn, axis=0, keepdims=True)
            nmax = cm if nmax is None else jnp.maximum(nmax, cm)
            p = jnp.exp2(cur[cur_slot, t * TK:(t + 1) * TK, :] - m_new)
            ls = jnp.sum(p, axis=0, keepdims=True)
            lsum = ls if lsum is None else lsum + ls
            d = jnp.dot(vT_ref[0, sub0 + t], p.astype(BF16), preferred_element_type=F32)
            pv = d if pv is None else pv + d
        smax_sc[nslot] = nmax
        l_sc[...] = alpha * l_sc[...] + lsum
        acc_sc[...] = alpha * acc_sc[...] + pv
        m_sc[...] = m_new

    def write_output():
        a = acc_sc[...] / l_sc[...]
        a0 = a[:, 0:TQ]
        a1 = a[:, TQ:2 * TQ]
        if diff:
            dl = lam_ref[...]
            lam = (jnp.exp(jnp.sum(dl[0:1] * dl[1:2], axis=1, keepdims=True))
                   - jnp.exp(jnp.sum(dl[2:3] * dl[3:4], axis=1, keepdims=True)) + lam_init)
            oT = a0 - lam * a1
            ms = jnp.mean(oT * oT, axis=0, keepdims=True)
            oT = oT * lax.rsqrt(ms + RMS_EPS)
            o_ref[0] = (oT.T * g_ref[...]) * (1.0 - lam_init)
        else:
            o_ref[0] = jnp.where(row < HEAD_DIM, a0, a1).T

    n_full = i // KV_SUBS
    rem = i % KV_SUBS
    for j in range(max_full):
        @pl.when(j < n_full)
        def _(j=j):
            if j == 0:
                full_block(t0_sc, t0max_sc, par, 0, j + 1)
            else:
                full_block(s_sc, smax_sc, j % 2, j * KV_SUBS, j + 1)
    for r in range(KV_SUBS):
        for diag_is_tile0 in (True, False):
            @pl.when((rem == r) & ((n_full == 0) == diag_is_tile0))
            def _(r=r, diag_is_tile0=diag_is_tile0):
                nslot = (r + 1) % 2
                half = tile_keys // 2
                max_a = score_rows(0, half, 1 - par, t0_sc, nslot)
                rows = (r + 1) * TK
                s = t0_sc[r % 2, 0:rows, :] if diag_is_tile0 else s_sc[n_full % 2, 0:rows, :]
                consume(s, n_full * KV_SUBS, r + 1, None)
                max_b = score_rows(half, half, 1 - par, t0_sc, nslot)
                t0max_sc[nslot] = jnp.maximum(max_a, max_b)
                write_output()


def _pair_attn_call(qT, k, kb, vT, lam_pad, g, *, batch, seq_len, diff, lam_init):
    nq = seq_len // TQ
    nk = seq_len // TK
    qT = qT.reshape(batch, nq, BRANCH_WIDTH, TQ)
    vT = vT.reshape(batch, nk, BRANCH_WIDTH, TK)
    k = k.reshape(batch, seq_len, BRANCH_WIDTH)
    kb_batched = kb.shape[0] != 1
    kb_map = (lambda b, p, i: (b, 0, p)) if kb_batched else (lambda b, p, i: (0, 0, p))
    return pl.pallas_call(
        functools.partial(_pair_attn_kernel, diff=diff, lam_init=lam_init,
                          max_full=seq_len // (KV_SUBS * TK) - 1),
        grid=(batch, N_PAIRS, nq),
        in_specs=[
            pl.BlockSpec((1, 1, LANES, TQ), lambda b, p, i: (b, i, p, 0)),
            pl.BlockSpec((1, 1, LANES, TQ),
                         lambda b, p, i: (b, jnp.minimum(i + 1, nq - 1), p, 0)),
            pl.BlockSpec((1, seq_len, LANES), lambda b, p, i: (b, 0, p)),
            pl.BlockSpec((1, seq_len, LANES), kb_map),
            pl.BlockSpec((1, nk, LANES, TK), lambda b, p, i: (b, 0, p, 0)),
            pl.BlockSpec((4, LANES), lambda b, p, i: (0, 0)),
            pl.BlockSpec((1, LANES), lambda b, p, i: (0, p)),
        ],
        out_specs=pl.BlockSpec((1, TQ, LANES), lambda b, p, i: (b, i, p)),
        out_shape=jax.ShapeDtypeStruct((batch, seq_len, BRANCH_WIDTH), F32),
        scratch_shapes=[
            pltpu.VMEM((2, 2 * LANES, 2 * TQ), BF16),
            pltpu.VMEM((1, 2 * TQ), F32),
            pltpu.VMEM((1, 2 * TQ), F32),
            pltpu.VMEM((LANES, 2 * TQ), F32),
            pltpu.VMEM((2, KV_SUBS * TK, 2 * TQ), F32),
            pltpu.VMEM((2, KV_SUBS * TK, 2 * TQ), F32),
            pltpu.VMEM((2, 1, 2 * TQ), F32),
            pltpu.VMEM((2, 1, 2 * TQ), F32),
        ],
        compiler_params=pltpu.CompilerParams(
            dimension_semantics=("arbitrary", "arbitrary", "arbitrary"),
            vmem_limit_bytes=VMEM_LIMIT),
        name="diff_attn" if diff else "fox_attn",
    )(qT, qT, k, kb, vT, lam_pad, g)


def _dil_kernel(q_ref, k_ref, v_ref, kb_ref, o_ref,
                qf, kf, vf, kbf, acc, m_sc, l_sc, mask_sc, s_buf, vT_buf, smax_buf,
                *, seq_len):
    L = DIL_BLOCK
    T = DIL_TQ
    qf[...] = q_ref[0].astype(F32)
    kf[...] = k_ref[0].astype(F32)
    vf[...] = v_ref[0].astype(F32)
    kbf[...] = kb_ref[0].astype(F32)
    acc[...] = jnp.zeros(acc.shape, F32)
    m_sc[...] = jnp.full(m_sc.shape, -jnp.inf, F32)
    l_sc[...] = jnp.zeros(l_sc.shape, F32)

    kr = lax.broadcasted_iota(jnp.int32, (2 * L, 2 * L), 0)
    qc = lax.broadcasted_iota(jnp.int32, (2 * L, 2 * L), 1) % L
    band = (kr >= qc) & (kr <= qc + L)
    mask_sc[0] = jnp.where(band & (kr >= L), 0.0, -jnp.inf)
    mask_sc[1] = jnp.where(band, 0.0, -jnp.inf)

    lane = lax.broadcasted_iota(jnp.int32, (T, LANES), 1)
    first = lane < HEAD_DIM
    sel0 = jnp.where(lane < 3, 1.0, 0.0).astype(BF16)
    sel1 = jnp.where((lane >= 3) & (lane < 6), 1.0, 0.0).astype(BF16)
    head0 = jnp.where(first, 1.0, 0.0).astype(BF16)
    head1 = jnp.where(first, 0.0, 1.0).astype(BF16)
    frow = lax.broadcasted_iota(jnp.int32, (LANES, L), 0) < HEAD_DIM
    n_qb = T // L

    for _, d in DIL_PATTERNS:
        nt = seq_len // (T * d)

        n_groups = seq_len // (T * DIL_GROUP)

        def addr(u, g, d=d, nt=nt):
            t = u * DIL_GROUP + g
            r = t // nt
            it = t % nt
            start = r + d * T * it
            prev = jnp.where(it == 0, start, start - d * L)
            if d == 1:
                start = pl.multiple_of(start, L)
                prev = pl.multiple_of(prev, L)
            return pl.ds(start, T, stride=d), pl.ds(prev, L, stride=d), jnp.minimum(it, 1)

        def rows_bf16(ref, ref_f32, rows, d=d):
            return ref[0, rows, :] if d == 1 else ref_f32[rows, :].astype(BF16)

        def score_group(u, buf):
            for g in range(DIL_GROUP):
                rows, prows, slot = addr(u, g)
                qb = rows_bf16(q_ref, qf, rows)
                q0 = jnp.concatenate([qb * head0, sel0], axis=1)
                q1 = jnp.concatenate([qb * head1, sel1], axis=1)
                ka = jnp.concatenate(
                    [jnp.concatenate([rows_bf16(k_ref, kf, prows),
                                      rows_bf16(kb_ref, kbf, prows)], axis=1),
                     jnp.concatenate([rows_bf16(k_ref, kf, rows),
                                      rows_bf16(kb_ref, kbf, rows)], axis=1)], axis=0)
                for b in range(n_qb):
                    qa = jnp.concatenate([q0[b * L:(b + 1) * L], q1[b * L:(b + 1) * L]], axis=0)
                    sb = lax.dot_general(ka[b * L:(b + 2) * L], qa, (((1,), (1,)), ((), ())),
                                         preferred_element_type=F32)
                    sb = sb + mask_sc[slot if b == 0 else 1]
                    s_buf[buf, g, b] = sb
                    smax_buf[buf, g, b] = jnp.max(sb, axis=0, keepdims=True)
                vv = jnp.concatenate([vf[prows, :], vf[rows, :]], axis=0)
                vT_buf[buf, g] = vv.T.astype(BF16)

        def consume_group(u, buf):
            num, m_blk, l_blk = [], [], []
            for g in range(DIL_GROUP):
                nb_, mb_, lb_ = [], [], []
                for b in range(n_qb):
                    mg = smax_buf[buf, g, b]
                    p = jnp.exp2(s_buf[buf, g, b] - mg)
                    lg = jnp.sum(p, axis=0, keepdims=True)
                    oT = jnp.dot(vT_buf[buf, g, :, b * L:(b + 2) * L], p.astype(BF16),
                                 preferred_element_type=F32)
                    nb_.append(jnp.where(frow, oT[:, :L], oT[:, L:]))
                    mb_.append(jnp.where(frow, jnp.broadcast_to(mg[:, :L], (LANES, L)),
                                         jnp.broadcast_to(mg[:, L:], (LANES, L))))
                    lb_.append(jnp.where(frow, jnp.broadcast_to(lg[:, :L], (LANES, L)),
                                         jnp.broadcast_to(lg[:, L:], (LANES, L))))
                num.append(jnp.concatenate(nb_, axis=1).T)
                m_blk.append(jnp.concatenate(mb_, axis=1).T)
                l_blk.append(jnp.concatenate(lb_, axis=1).T)
            for g in range(DIL_GROUP):
                rows, _, _ = addr(u, g)
                m_old = m_sc[rows, :]
                m_new = jnp.maximum(m_old, m_blk[g])
                w_old = jnp.exp2(m_old - m_new)
                w_blk = jnp.exp2(m_blk[g] - m_new)
                l_sc[rows, :] = w_old * l_sc[rows, :] + w_blk * l_blk[g]
                acc[rows, :] = w_old * acc[rows, :] + w_blk * num[g]
                m_sc[rows, :] = m_new

        score_group(0, 0)

        def trip(v, carry):
            score_group(2 * v + 1, 1)
            consume_group(2 * v, 0)
            score_group(jnp.minimum(2 * v + 2, n_groups - 1), 0)
            consume_group(2 * v + 1, 1)
            return carry

        lax.fori_loop(0, n_groups // 2, trip, 0)

    o_ref[0] = acc[...] / l_sc[...]


def _dil_call(q, k, v, kb, *, batch, seq_len):
    assert seq_len % (DIL_TQ * max(d for _, d in DIL_PATTERNS)) == 0
    assert seq_len % (DIL_TQ * DIL_GROUP * 2) == 0
    q = q.reshape(batch, seq_len, BRANCH_WIDTH)
    k = k.reshape(batch, seq_len, BRANCH_WIDTH)
    v = v.reshape(batch, seq_len, BRANCH_WIDTH)
    blk = pl.BlockSpec((1, seq_len, LANES), lambda b, p: (b, 0, p))
    n_qb = DIL_TQ // DIL_BLOCK
    return pl.pallas_call(
        functools.partial(_dil_kernel, seq_len=seq_len),
        grid=(batch, N_PAIRS),
        in_specs=[blk, blk, blk, pl.BlockSpec((1, seq_len, LANES), lambda b, p: (0, 0, p))],
        out_specs=blk,
        out_shape=jax.ShapeDtypeStruct((batch, seq_len, BRANCH_WIDTH), F32),
        scratch_shapes=[pltpu.VMEM((seq_len, LANES), F32) for _ in range(7)]
        + [pltpu.VMEM((2, 2 * DIL_BLOCK, 2 * DIL_BLOCK), F32),
           pltpu.VMEM((2, DIL_GROUP, n_qb, 2 * DIL_BLOCK, 2 * DIL_BLOCK), F32),
           pltpu.VMEM((2, DIL_GROUP, LANES, DIL_TQ + DIL_BLOCK), BF16),
           pltpu.VMEM((2, DIL_GROUP, n_qb, 1, 2 * DIL_BLOCK), F32)],
        compiler_params=pltpu.CompilerParams(
            dimension_semantics=("arbitrary", "arbitrary"), vmem_limit_bytes=VMEM_LIMIT),
        name="dil_attn",
    )(q, k, v, kb)


def _merge_kernel(x_ref, oa_ref, ob_ref, oc_ref, g_ref, wz_ref, wg_ref, wb_ref, wo_ref,
                  fg_ref, out_ref, *, last):
    x = x_ref[...]
    ms = jnp.mean(x * x, axis=-1, keepdims=True)
    h = (x * lax.rsqrt(ms + RMS_EPS) * g_ref[...]).astype(BF16)
    merged = None
    for n, o_ref in enumerate((oa_ref, ob_ref, oc_ref)):
        z = jnp.dot(h, wz_ref[:, n * BRANCH_WIDTH:(n + 1) * BRANCH_WIDTH],
                    preferred_element_type=F32)
        y = (o_ref[...] * (z * jax.nn.sigmoid(z))).astype(BF16)
        br = jnp.dot(y, wb_ref[n], preferred_element_type=F32)
        gate = jax.nn.sigmoid(jnp.dot(h, wg_ref[:, n * D_MODEL:(n + 1) * D_MODEL],
                                      preferred_element_type=F32))
        merged = gate * br if merged is None else merged + gate * br
    out = x + jnp.dot(merged.astype(BF16), wo_ref[...], preferred_element_type=F32)
    if last:
        ms = jnp.mean(out * out, axis=-1, keepdims=True)
        out = out * lax.rsqrt(ms + RMS_EPS) * fg_ref[...]
    out_ref[...] = out


def _merge_call(x2d, oa, ob, oc, g, wz, wg, wb, wo, fg, *, last):
    n_tok = x2d.shape[0]
    tm = TM_MERGE
    const2 = lambda i: (0, 0)
    tok = pl.BlockSpec((tm, BRANCH_WIDTH), lambda i: (i, 0))
    return pl.pallas_call(
        functools.partial(_merge_kernel, last=last),
        grid=(n_tok // tm,),
        in_specs=[
            pl.BlockSpec((tm, D_MODEL), lambda i: (i, 0)), tok, tok, tok,
            pl.BlockSpec((1, D_MODEL), const2),
            pl.BlockSpec((D_MODEL, 3 * BRANCH_WIDTH), const2),
            pl.BlockSpec((D_MODEL, 3 * D_MODEL), const2),
            pl.BlockSpec((3, BRANCH_WIDTH, D_MODEL), lambda i: (0, 0, 0)),
            pl.BlockSpec((D_MODEL, D_MODEL), const2),
            pl.BlockSpec((1, D_MODEL), const2),
        ],
        out_specs=pl.BlockSpec((tm, D_MODEL), lambda i: (i, 0)),
        out_shape=jax.ShapeDtypeStruct((n_tok, D_MODEL), F32),
        compiler_params=pltpu.CompilerParams(
            dimension_semantics=("arbitrary",), vmem_limit_bytes=VMEM_LIMIT),
        name="merge",
    )(x2d, oa.reshape(n_tok, BRANCH_WIDTH), ob.reshape(n_tok, BRANCH_WIDTH),
      oc.reshape(n_tok, BRANCH_WIDTH), g, wz, wg, wb, wo, fg)


def _alibi_slopes(n_heads):
    return [2.0 ** (-ALIBI_MAX_EXP * (h + 1) / n_heads) for h in range(n_heads)]


def _alibi_key_bias(seq_len, slopes_per_pair):
    pos = np.arange(seq_len, dtype=np.float64)
    out = np.zeros((seq_len, BRANCH_WIDTH), dtype=BF16)
    for p, pair in enumerate(slopes_per_pair):
        for c, slope in enumerate(pair):
            rest = pos * (slope * LOG2E)
            for x in range(3):
                piece = rest.astype(BF16)
                out[:, p * LANES + 3 * c + x] = piece
                rest = rest - piece.astype(np.float64)
    return jnp.asarray(out)[None]


def _fox_bias_scatter():
    m = [[0.0] * BRANCH_WIDTH for _ in range(3 * LANES)]
    for x in range(3):
        for h in range(N_HEADS):
            m[x * LANES + h][(h // 2) * LANES + 3 * (h % 2) + x] = 1.0
    return jnp.asarray(m, dtype=BF16)


def kernel(x, norm_g, w_in, fox_fb, diff_lam, diff_norm_g, w_branch, w_out, final_g):
    batch, seq_len, _ = x.shape
    depth = norm_g.shape[0]
    n_tok = batch * seq_len
    scale = HEAD_DIM ** -0.5 * LOG2E

    w = BRANCH_WIDTH
    off = {}
    pos = 0
    for name, width in (("diff_q", w), ("diff_k", w), ("diff_v", w), ("diff_z", w),
                        ("fox_q", w), ("fox_k", w), ("fox_v", w), ("fox_f", N_HEADS),
                        ("fox_z", w), ("dil_q", w), ("dil_k", w), ("dil_v", w),
                        ("dil_z", w), ("merge_g", 3 * D_MODEL)):
        off[name] = (pos, pos + width)
        pos += width

    def cols(wl, name, mult=1.0):
        a, b = off[name]
        return wl[:, a:b] * mult if mult != 1.0 else wl[:, a:b]

    ds = _alibi_slopes(DIFF_HEADS)
    cs = _alibi_slopes(N_HEADS)
    diff_kb = _alibi_key_bias(seq_len, [(ds[h], ds[h]) for h in range(DIFF_HEADS)])
    dil_kb = _alibi_key_bias(seq_len, [(cs[2 * p], cs[2 * p + 1]) for p in range(N_PAIRS)])
    pmat = _fox_bias_scatter()
    ones_g = jnp.ones((1, BRANCH_WIDTH), F32)
    zero_lam = jnp.zeros((4, LANES), F32)

    x2d = x.reshape(n_tok, D_MODEL)
    for l in range(depth):
        wl = w_in[l]
        w_qkv = jnp.concatenate(
            [cols(wl, "diff_q", scale), cols(wl, "diff_k"), cols(wl, "diff_v"),
             cols(wl, "fox_q", scale), cols(wl, "fox_k"), cols(wl, "fox_v"),
             cols(wl, "dil_q", scale), cols(wl, "dil_k"), cols(wl, "dil_v")],
            axis=1).astype(BF16)
        w_f = jnp.pad(cols(wl, "fox_f"), ((0, 0), (0, LANES - N_HEADS))).astype(BF16)
        fb = jnp.pad(fox_fb[l].astype(F32), (0, LANES - N_HEADS))[None]
        g = norm_g[l].astype(F32)[None]

        (dqT, dk, dvT, fqT, fk, fvT, fkb, cq, ck, cv) = _proj_call(
            x2d, g, w_qkv, w_f, fb, pmat, seq_len=seq_len)

        lam_init = 0.8 - 0.6 * math.exp(-0.3 * l)
        lam_pad = jnp.pad(diff_lam[l].astype(F32), ((0, 0), (0, LANES - HEAD_DIM)))
        oa = _pair_attn_call(dqT, dk, diff_kb, dvT, lam_pad,
                             diff_norm_g[l].astype(F32)[None],
                             batch=batch, seq_len=seq_len, diff=True, lam_init=lam_init)
        ob = _pair_attn_call(fqT, fk, fkb.reshape(batch, seq_len, BRANCH_WIDTH), fvT,
                             zero_lam, ones_g,
                             batch=batch, seq_len=seq_len, diff=False, lam_init=0.0)
        oc = _dil_call(cq, ck, cv, dil_kb, batch=batch, seq_len=seq_len)

        wz = jnp.concatenate([cols(wl, "diff_z"), cols(wl, "fox_z"), cols(wl, "dil_z")],
                             axis=1).astype(BF16)
        wg = cols(wl, "merge_g").astype(BF16)
        x2d = _merge_call(x2d, oa, ob, oc, g, wz, wg, w_branch[l].astype(BF16),
                          w_out[l].astype(BF16), final_g.astype(F32)[None],
                          last=(l == depth - 1))
    return x2d.reshape(batch, seq_len, D_MODEL)
```

```python
import functools
import math

import jax
import jax.numpy as jnp
import numpy as np
from jax import lax
from jax.experimental import pallas as pl
from jax.experimental.pallas import tpu as pltpu

F32 = jnp.float32
BF16 = jnp.bfloat16

D_MODEL = 1024
HEAD_DIM = 64
BRANCH_WIDTH = 512
LANES = 128
N_PAIRS = BRANCH_WIDTH // LANES
DIFF_HEADS = 4
N_HEADS = 8
DIL_PATTERNS = ((128, 1), (512, 4), (2048, 16))
DIL_BLOCK = 128
DIL_TQ = 256
DIL_GROUP = 2
RMS_EPS = 1e-6
LOG2E = math.log2(math.e)
ALIBI_MAX_EXP = 8.0
N_QKV_GROUPS = 9
TQ = 512
TK = 512
KV_SUBS = 2
TM_PROJ = 512
TM_MERGE = 512
VMEM_LIMIT = 56 * 1024 * 1024


def _split3(a):
    a1 = a.astype(BF16)
    r1 = a - a1.astype(F32)
    a2 = r1.astype(BF16)
    a3 = (r1 - a2.astype(F32)).astype(BF16)
    return a1, a2, a3


def _proj_kernel(x_ref, g_ref, w_ref, wf_ref, fb_ref, pmat_ref,
                 dqT_ref, dk_ref, dvT_ref, fqT_ref, fk_ref, fvT_ref, fkb_ref,
                 cq_ref, ck_ref, cv_ref, carry_ref, *, tiles_per_seq):
    i = pl.program_id(0)
    tm = x_ref.shape[0]
    x = x_ref[...]
    ms = jnp.mean(x * x, axis=-1, keepdims=True)
    h = (x * lax.rsqrt(ms + RMS_EPS) * g_ref[...]).astype(BF16)

    def proj(g):
        return jnp.dot(h, w_ref[:, g * BRANCH_WIDTH:(g + 1) * BRANCH_WIDTH],
                       preferred_element_type=F32)

    def put_t(ref, val):
        for t in range(tm // TQ):
            ref[t] = val[t * TQ:(t + 1) * TQ, :].T.astype(BF16)

    put_t(dqT_ref, proj(0))
    dk_ref[...] = proj(1).astype(BF16)
    put_t(dvT_ref, proj(2))
    put_t(fqT_ref, proj(3))
    fk_ref[...] = proj(4).astype(BF16)
    put_t(fvT_ref, proj(5))
    cq_ref[...] = proj(6).astype(BF16)
    ck_ref[...] = proj(7).astype(BF16)
    cv_ref[...] = proj(8).astype(BF16)

    f = jnp.dot(h, wf_ref[...], preferred_element_type=F32) + fb_ref[...]
    logf = jnp.minimum(f, 0.0) - jnp.log1p(jnp.exp(-jnp.abs(f)))

    @pl.when(i % tiles_per_seq == 0)
    def _():
        carry_ref[...] = jnp.zeros_like(carry_ref)

    row = lax.broadcasted_iota(jnp.int32, (tm, tm), 0)
    col = lax.broadcasted_iota(jnp.int32, (tm, tm), 1)
    tri = jnp.where(row >= col, 1.0, 0.0).astype(BF16)
    a1, a2, a3 = _split3(logf)
    c = (jnp.dot(tri, a1, preferred_element_type=F32)
         + jnp.dot(tri, a2, preferred_element_type=F32)
         + jnp.dot(tri, a3, preferred_element_type=F32)) + carry_ref[0:1, :]
    carry_ref[0:1, :] = c[tm - 1:tm, :]
    b1, b2, b3 = _split3(c * (-LOG2E))
    kb = jnp.dot(jnp.concatenate([b1, b2, b3], axis=1), pmat_ref[...],
                 preferred_element_type=F32)
    fkb_ref[...] = kb.astype(BF16)


def _proj_call(x2d, g, w_qkv, w_f, fb, pmat, *, seq_len):
    n_tok = x2d.shape[0]
    tm = TM_PROJ
    nt = n_tok // tm
    const = lambda i: (0, 0)
    tok = pl.BlockSpec((tm, BRANCH_WIDTH), lambda i: (i, 0))
    tr = pl.BlockSpec((tm // TQ, BRANCH_WIDTH, TQ), lambda i: (i, 0, 0))
    nat_shape = jax.ShapeDtypeStruct((n_tok, BRANCH_WIDTH), BF16)
    tr_shape = jax.ShapeDtypeStruct((n_tok // TQ, BRANCH_WIDTH, TQ), BF16)
    return pl.pallas_call(
        functools.partial(_proj_kernel, tiles_per_seq=seq_len // tm),
        grid=(nt,),
        in_specs=[
            pl.BlockSpec((tm, D_MODEL), lambda i: (i, 0)),
            pl.BlockSpec((1, D_MODEL), const),
            pl.BlockSpec((D_MODEL, N_QKV_GROUPS * BRANCH_WIDTH), const),
            pl.BlockSpec((D_MODEL, LANES), const),
            pl.BlockSpec((1, LANES), const),
            pl.BlockSpec((3 * LANES, BRANCH_WIDTH), const),
        ],
        out_specs=[tr, tok, tr, tr, tok, tr, tok, tok, tok, tok],
        out_shape=[tr_shape, nat_shape, tr_shape, tr_shape, nat_shape, tr_shape,
                   nat_shape, nat_shape, nat_shape, nat_shape],
        scratch_shapes=[pltpu.VMEM((8, LANES), F32)],
        compiler_params=pltpu.CompilerParams(
            dimension_semantics=("arbitrary",), vmem_limit_bytes=VMEM_LIMIT),
        name="proj",
    )(x2d, g, w_qkv, w_f, fb, pmat)


def _pair_attn_kernel(qT_ref, qTn_ref, k_ref, kb_ref, vT_ref, lam_ref, g_ref, o_ref,
                      qa_sc, m_sc, l_sc, acc_sc, s_sc, t0_sc, smax_sc, t0max_sc,
                      *, diff, lam_init, max_full):
    i = pl.program_id(2)
    par = i % 2
    row = lax.broadcasted_iota(jnp.int32, (LANES, TQ), 0)

    def build_qa(q_ref, slot):
        qT = q_ref[0, 0].astype(F32)
        qa_sc[slot, :, 0:TQ] = jnp.concatenate(
            [jnp.where(row < HEAD_DIM, qT, 0.0), jnp.where(row < 3, 1.0, 0.0)],
            axis=0).astype(BF16)
        qa_sc[slot, :, TQ:2 * TQ] = jnp.concatenate(
            [jnp.where(row >= HEAD_DIM, qT, 0.0),
             jnp.where((row >= 3) & (row < 6), 1.0, 0.0)], axis=0).astype(BF16)

    tile_keys = KV_SUBS * TK

    def score_rows(row0, n_rows, qa_slot, dst, dst_slot):
        ka = jnp.concatenate([k_ref[0, row0:row0 + n_rows, :],
                              kb_ref[0, row0:row0 + n_rows, :]], axis=1)
        s = jnp.dot(ka, qa_sc[qa_slot], preferred_element_type=F32)
        dst[dst_slot, row0:row0 + n_rows, :] = s
        return jnp.max(s, axis=0, keepdims=True)

    @pl.when(i == 0)
    def _():
        build_qa(qT_ref, 0)
        t0max_sc[0] = score_rows(0, tile_keys, 0, t0_sc, 0)

    build_qa(qTn_ref, 1 - par)
    m_sc[...] = jnp.full(m_sc.shape, -jnp.inf, F32)
    l_sc[...] = jnp.zeros(l_sc.shape, F32)
    acc_sc[...] = jnp.zeros(acc_sc.shape, F32)

    def consume(s, sub0, n_sub, tile_max):
        rows = n_sub * TK
        if tile_max is None:
            kr = lax.broadcasted_iota(jnp.int32, (TK, 2 * TQ), 0)
            qc = lax.broadcasted_iota(jnp.int32, (TK, 2 * TQ), 1) % TQ
            tail = jnp.where(kr <= qc, s[rows - TK:], -jnp.inf)
            s = tail if n_sub == 1 else jnp.concatenate([s[:rows - TK], tail], axis=0)
            tile_max = jnp.max(s, axis=0, keepdims=True)
        m_old = m_sc[...]
        m_new = jnp.maximum(m_old, tile_max)
        alpha = jnp.exp2(m_old - m_new)
        p = jnp.exp2(s - m_new)
        l_sc[...] = alpha * l_sc[...] + jnp.sum(p, axis=0, keepdims=True)
        pb = p.astype(BF16)
        pv = jnp.dot(vT_ref[0, sub0], pb[0:TK], preferred_element_type=F32)
        for t in range(1, n_sub):
            pv = pv + jnp.dot(vT_ref[0, sub0 + t], pb[t * TK:(t + 1) * TK],
                              preferred_element_type=F32)
        acc_sc[...] = alpha * acc_sc[...] + pv
        m_sc[...] = m_new

    def full_block(cur, cur_max, cur_slot, sub0, t_next):
        m_old = m_sc[...]
        m_new = jnp.maximum(m_old, cur_max[cur_slot])
        alpha = jnp.exp2(m_old - m_new)
        nslot = t_next % 2
        lsum = pv = nmax = None
        for t in range(KV_SUBS):
            kbase = pl.multiple_of(t_next * tile_keys + t * TK, TK)
            ka = jnp.concatenate([k_ref[0, pl.ds(kbase, TK), :],
                                  kb_ref[0, pl.ds(kbase, TK), :]], axis=1)
            sn = jnp.dot(ka, qa_sc[par], preferred_element_type=F32)
            s_sc[nslot, t * TK:(t + 1) * TK, :] = sn
            cm = jnp.max(sn, axis=0, keepdims=True)
            nmax = cm if nmax is None else jnp.maximum(nmax, cm)
            p = jnp.exp2(cur[cur_slot, t * TK:(t + 1) * TK, :] - m_new)
            ls = jnp.sum(p, axis=0, keepdims=True)
            lsum = ls if lsum is None else lsum + ls
            d = jnp.dot(vT_ref[0, sub0 + t], p.astype(BF16), preferred_element_type=F32)
            pv = d if pv is None else pv + d
        smax_sc[nslot] = nmax
        l_sc[...] = alpha * l_sc[...] + lsum
        acc_sc[...] = alpha * acc_sc[...] + pv
        m_sc[...] = m_new

    def write_output():
        a = acc_sc[...] / l_sc[...]
        a0 = a[:, 0:TQ]
        a1 = a[:, TQ:2 * TQ]
        if diff:
            dl = lam_ref[...]
            lam = (jnp.exp(jnp.sum(dl[0:1] * dl[1:2], axis=1, keepdims=True))
                   - jnp.exp(jnp.sum(dl[2:3] * dl[3:4], axis=1, keepdims=True)) + lam_init)
            oT = a0 - lam * a1
            ms = jnp.mean(oT * oT, axis=0, keepdims=True)
            oT = oT * lax.rsqrt(ms + RMS_EPS)
            o_ref[0] = (oT.T * g_ref[...]) * (1.0 - lam_init)
        else:
            o_ref[0] = jnp.where(row < HEAD_DIM, a0, a1).T

    n_full = i // KV_SUBS
    rem = i % KV_SUBS
    for j in range(max_full):
        @pl.when(j < n_full)
        def _(j=j):
            if j == 0:
                full_block(t0_sc, t0max_sc, par, 0, j + 1)
            else:
                full_block(s_sc, smax_sc, j % 2, j * KV_SUBS, j + 1)
    for r in range(KV_SUBS):
        for diag_is_tile0 in (True, False):
            @pl.when((rem == r) & ((n_full == 0) == diag_is_tile0))
            def _(r=r, diag_is_tile0=diag_is_tile0):
                nslot = (r + 1) % 2
                half = tile_keys // 2
                max_a = score_rows(0, half, 1 - par, t0_sc, nslot)
                rows = (r + 1) * TK
                s = t0_sc[r % 2, 0:rows, :] if diag_is_tile0 else s_sc[n_full % 2, 0:rows, :]
                consume(s, n_full * KV_SUBS, r + 1, None)
                max_b = score_rows(half, half, 1 - par, t0_sc, nslot)
                t0max_sc[nslot] = jnp.maximum(max_a, max_b)
                write_output()


def _pair_attn_call(qT, k, kb, vT, lam_pad, g, *, batch, seq_len, diff, lam_init):
    nq = seq_len // TQ
    nk = seq_len // TK
    qT = qT.reshape(batch, nq, BRANCH_WIDTH, TQ)
    vT = vT.reshape(batch, nk, BRANCH_WIDTH, TK)
    k = k.reshape(batch, seq_len, BRANCH_WIDTH)
    kb_batched = kb.shape[0] != 1
    kb_map = (lambda b, p, i: (b, 0, p)) if kb_batched else (lambda b, p, i: (0, 0, p))
    return pl.pallas_call(
        functools.partial(_pair_attn_kernel, diff=diff, lam_init=lam_init,
                          max_full=seq_len // (KV_SUBS * TK) - 1),
        grid=(batch, N_PAIRS, nq),
        in_specs=[
            pl.BlockSpec((1, 1, LANES, TQ), lambda b, p, i: (b, i, p, 0)),
            pl.BlockSpec((1, 1, LANES, TQ),
                         lambda b, p, i: (b, jnp.minimum(i + 1, nq - 1), p, 0)),
            pl.BlockSpec((1, seq_len, LANES), lambda b, p, i: (b, 0, p)),
            pl.BlockSpec((1, seq_len, LANES), kb_map),
            pl.BlockSpec((1, nk, LANES, TK), lambda b, p, i: (b, 0, p, 0)),
            pl.BlockSpec((4, LANES), lambda b, p, i: (0, 0)),
            pl.BlockSpec((1, LANES), lambda b, p, i: (0, p)),
        ],
        out_specs=pl.BlockSpec((1, TQ, LANES), lambda b, p, i: (b, i, p)),
        out_shape=jax.ShapeDtypeStruct((batch, seq_len, BRANCH_WIDTH), F32),
        scratch_shapes=[
            pltpu.VMEM((2, 2 * LANES, 2 * TQ), BF16),
            pltpu.VMEM((1, 2 * TQ), F32),
            pltpu.VMEM((1, 2 * TQ), F32),
            pltpu.VMEM((LANES, 2 * TQ), F32),
            pltpu.VMEM((2, KV_SUBS * TK, 2 * TQ), F32),
            pltpu.VMEM((2, KV_SUBS * TK, 2 * TQ), F32),
            pltpu.VMEM((2, 1, 2 * TQ), F32),
            pltpu.VMEM((2, 1, 2 * TQ), F32),
        ],
        compiler_params=pltpu.CompilerParams(
            dimension_semantics=("arbitrary", "arbitrary", "arbitrary"),
            vmem_limit_bytes=VMEM_LIMIT),
        name="diff_attn" if diff else "fox_attn",
    )(qT, qT, k, kb, vT, lam_pad, g)


def _dil_kernel(q_ref, k_ref, v_ref, kb1_ref, kb4_ref, kb16_ref, o_ref,
                qf, kf, vf, q4f, k4f, v4f, q_sub, k_sub, v_sub, acc, m_sc, l_sc, acc4, m4, l4,
                mask_sc, s_buf, vT_buf, smax_buf, *, seq_len):
    L = DIL_BLOCK
    T = DIL_TQ
    qf[...] = q_ref[0].astype(F32)
    kf[...] = k_ref[0].astype(F32)
    vf[...] = v_ref[0].astype(F32)

    kr = lax.broadcasted_iota(jnp.int32, (2 * L, 2 * L), 0)
    qc = lax.broadcasted_iota(jnp.int32, (2 * L, 2 * L), 1) % L
    band = (kr >= qc) & (kr <= qc + L)
    mask_sc[0] = jnp.where(band & (kr >= L), 0.0, -jnp.inf)
    mask_sc[1] = jnp.where(band, 0.0, -jnp.inf)

    lane = lax.broadcasted_iota(jnp.int32, (T, LANES), 1)
    first = lane < HEAD_DIM
    sel0 = jnp.where(lane < 3, 1.0, 0.0).astype(BF16)
    sel1 = jnp.where((lane >= 3) & (lane < 6), 1.0, 0.0).astype(BF16)
    head0 = jnp.where(first, 1.0, 0.0).astype(BF16)
    head1 = jnp.where(first, 0.0, 1.0).astype(BF16)
    frow = lax.broadcasted_iota(jnp.int32, (LANES, L), 0) < HEAD_DIM
    n_qb = T // L

    assert tuple(d for _, d in DIL_PATTERNS) == (1, 4, 16)
    n4 = seq_len // 4
    n16 = seq_len // 16
    kb_refs = (kb1_ref, kb4_ref, kb16_ref)
    for pattern, (_, d) in enumerate(DIL_PATTERNS):
        nt = seq_len // (T * d)
        n = seq_len // d
        kb_ref = kb_refs[pattern]
        if d == 1:
            q_rows = lambda rows: q_ref[0, rows, :]
            k_rows = lambda rows: k_ref[0, rows, :]
            v_rows = lambda rows: v_ref[0, rows, :]
            m_st, l_st, a_st = m_sc, l_sc, acc
        else:
            if d == 4:
                for r in range(4):
                    dst = slice(r * n4, (r + 1) * n4)
                    for src_f, dst_f, dst_b in ((qf, q4f, q_sub), (kf, k4f, k_sub), (vf, v4f, v_sub)):
                        val = src_f[pl.ds(r, n4, stride=4), :]
                        dst_f[dst, :] = val
                        dst_b[dst, :] = val.astype(BF16)
                    for src_s, dst_s in ((m_sc, m4), (l_sc, l4), (acc, acc4)):
                        dst_s[dst, :] = src_s[pl.ds(r, n4, stride=4), :]
            else:
                for r in range(16):
                    src0 = (r % 4) * n4 + r // 4
                    dst = slice(r * n16, (r + 1) * n16)
                    for src_f, dst_b in ((q4f, q_sub), (k4f, k_sub), (v4f, v_sub)):
                        dst_b[dst, :] = src_f[pl.ds(src0, n16, stride=4), :].astype(BF16)
            q_rows = lambda rows: q_sub[rows, :]
            k_rows = lambda rows: k_sub[rows, :]
            v_rows = lambda rows: v_sub[rows, :]
            m_st, l_st, a_st = m4, l4, acc4

        n_groups = seq_len // (T * DIL_GROUP)

        def addr(u, g, d=d, nt=nt, n=n):
            t = u * DIL_GROUP + g
            r = t // nt
            it = t % nt
            start = pl.multiple_of(r * n + T * it, L)
            prev = pl.multiple_of(jnp.where(it == 0, start, start - L), L)
            if d == 16:
                state_rows = pl.ds((r % 4) * n4 + 4 * T * it + r // 4, T, stride=4)
            else:
                state_rows = pl.ds(start, T)
            return pl.ds(start, T), pl.ds(prev, L), jnp.minimum(it, 1), state_rows

        def score_group(u, buf, q_rows=q_rows, k_rows=k_rows, v_rows=v_rows, kb_ref=kb_ref):
            for g in range(DIL_GROUP):
                rows, prows, slot, _ = addr(u, g)
                qb = q_rows(rows)
                q0 = jnp.concatenate([qb * head0, sel0], axis=1)
                q1 = jnp.concatenate([qb * head1, sel1], axis=1)
                ka = jnp.concatenate(
                    [jnp.concatenate([k_rows(prows), kb_ref[0, prows, :]], axis=1),
                     jnp.concatenate([k_rows(rows), kb_ref[0, rows, :]], axis=1)],
                    axis=0)
                for b in range(n_qb):
                    qa = jnp.concatenate([q0[b * L:(b + 1) * L], q1[b * L:(b + 1) * L]], axis=0)
                    sb = lax.dot_general(ka[b * L:(b + 2) * L], qa, (((1,), (1,)), ((), ())),
                                         preferred_element_type=F32)
                    sb = sb + mask_sc[slot if b == 0 else 1]
                    s_buf[buf, g, b] = sb
                    smax_buf[buf, g, b] = jnp.max(sb, axis=0, keepdims=True)
                vv = jnp.concatenate([v_rows(prows), v_rows(rows)], axis=0)
                vT_buf[buf, g] = vv.astype(F32).T.astype(BF16)

        def consume_group(u, buf, first_pattern=(pattern == 0), m_st=m_st, l_st=l_st, a_st=a_st):
            num, m_blk, l_blk = [], [], []
            for g in range(DIL_GROUP):
                nb_, mb_, lb_ = [], [], []
                for b in range(n_qb):
                    mg = smax_buf[buf, g, b]
                    p = jnp.exp2(s_buf[buf, g, b] - mg)
                    lg = jnp.sum(p, axis=0, keepdims=True)
                    oT = jnp.dot(vT_buf[buf, g, :, b * L:(b + 2) * L], p.astype(BF16),
                                 preferred_element_type=F32)
                    nb_.append(jnp.where(frow, oT[:, :L], oT[:, L:]))
                    mb_.append(jnp.where(frow, jnp.broadcast_to(mg[:, :L], (LANES, L)),
                                         jnp.broadcast_to(mg[:, L:], (LANES, L))))
                    lb_.append(jnp.where(frow, jnp.broadcast_to(lg[:, :L], (LANES, L)),
                                         jnp.broadcast_to(lg[:, L:], (LANES, L))))
                num.append(jnp.concatenate(nb_, axis=1).T)
                m_blk.append(jnp.concatenate(mb_, axis=1).T)
                l_blk.append(jnp.concatenate(lb_, axis=1).T)
            for g in range(DIL_GROUP):
                _, _, _, rows = addr(u, g)
                if first_pattern:
                    l_st[rows, :] = l_blk[g]
                    a_st[rows, :] = num[g]
                    m_st[rows, :] = m_blk[g]
                    continue
                m_old = m_st[rows, :]
                m_new = jnp.maximum(m_old, m_blk[g])
                w_old = jnp.exp2(m_old - m_new)
                w_blk = jnp.exp2(m_blk[g] - m_new)
                l_st[rows, :] = w_old * l_st[rows, :] + w_blk * l_blk[g]
                a_st[rows, :] = w_old * a_st[rows, :] + w_blk * num[g]
                m_st[rows, :] = m_new

        score_group(0, 0)

        def trip(v, carry):
            score_group(2 * v + 1, 1)
            consume_group(2 * v, 0)
            score_group(jnp.minimum(2 * v + 2, n_groups - 1), 0)
            consume_group(2 * v + 1, 1)
            return carry

        lax.fori_loop(0, n_groups // 2, trip, 0)

    for r in range(4):
        rows4 = slice(r * n4, (r + 1) * n4)
        o_ref[0, pl.ds(r, n4, stride=4), :] = acc4[rows4, :] / l4[rows4, :]


def _dil_call(q, k, v, kbs, *, batch, seq_len):
    assert seq_len % (DIL_TQ * max(d for _, d in DIL_PATTERNS)) == 0
    assert seq_len % (DIL_TQ * DIL_GROUP * 2) == 0
    q = q.reshape(batch, seq_len, BRANCH_WIDTH)
    k = k.reshape(batch, seq_len, BRANCH_WIDTH)
    v = v.reshape(batch, seq_len, BRANCH_WIDTH)
    blk = pl.BlockSpec((1, seq_len, LANES), lambda b, p: (b, 0, p))
    kb_blk = pl.BlockSpec((1, seq_len, LANES), lambda b, p: (0, 0, p))
    n_qb = DIL_TQ // DIL_BLOCK
    return pl.pallas_call(
        functools.partial(_dil_kernel, seq_len=seq_len),
        grid=(batch, N_PAIRS),
        in_specs=[blk, blk, blk, kb_blk, kb_blk, kb_blk],
        out_specs=blk,
        out_shape=jax.ShapeDtypeStruct((batch, seq_len, BRANCH_WIDTH), F32),
        scratch_shapes=[pltpu.VMEM((seq_len, LANES), F32) for _ in range(6)]
        + [pltpu.VMEM((seq_len, LANES), BF16) for _ in range(3)]
        + [pltpu.VMEM((seq_len, LANES), F32) for _ in range(6)]
        + [pltpu.VMEM((2, 2 * DIL_BLOCK, 2 * DIL_BLOCK), F32),
           pltpu.VMEM((2, DIL_GROUP, n_qb, 2 * DIL_BLOCK, 2 * DIL_BLOCK), F32),
           pltpu.VMEM((2, DIL_GROUP, LANES, DIL_TQ + DIL_BLOCK), BF16),
           pltpu.VMEM((2, DIL_GROUP, n_qb, 1, 2 * DIL_BLOCK), F32)],
        compiler_params=pltpu.CompilerParams(
            dimension_semantics=("arbitrary", "arbitrary"), vmem_limit_bytes=VMEM_LIMIT),
        name="dil_attn",
    )(q, k, v, *kbs)


def _merge_kernel(x_ref, oa_ref, ob_ref, oc_ref, g_ref, wz_ref, wg_ref, wb_ref, wo_ref,
                  fg_ref, out_ref, *, last):
    x = x_ref[...]
    ms = jnp.mean(x * x, axis=-1, keepdims=True)
    h = (x * lax.rsqrt(ms + RMS_EPS) * g_ref[...]).astype(BF16)
    merged = None
    for n, o_ref in enumerate((oa_ref, ob_ref, oc_ref)):
        z = jnp.dot(h, wz_ref[:, n * BRANCH_WIDTH:(n + 1) * BRANCH_WIDTH],
                    preferred_element_type=F32)
        y = (o_ref[...] * (z * jax.nn.sigmoid(z))).astype(BF16)
        br = jnp.dot(y, wb_ref[n], preferred_element_type=F32)
        gate = jax.nn.sigmoid(jnp.dot(h, wg_ref[:, n * D_MODEL:(n + 1) * D_MODEL],
                                      preferred_element_type=F32))
        merged = gate * br if merged is None else merged + gate * br
    out = x + jnp.dot(merged.astype(BF16), wo_ref[...], preferred_element_type=F32)
    if last:
        ms = jnp.mean(out * out, axis=-1, keepdims=True)
        out = out * lax.rsqrt(ms + RMS_EPS) * fg_ref[...]
    out_ref[...] = out


def _merge_call(x2d, oa, ob, oc, g, wz, wg, wb, wo, fg, *, last):
    n_tok = x2d.shape[0]
    tm = TM_MERGE
    const2 = lambda i: (0, 0)
    tok = pl.BlockSpec((tm, BRANCH_WIDTH), lambda i: (i, 0))
    return pl.pallas_call(
        functools.partial(_merge_kernel, last=last),
        grid=(n_tok // tm,),
        in_specs=[
            pl.BlockSpec((tm, D_MODEL), lambda i: (i, 0)), tok, tok, tok,
            pl.BlockSpec((1, D_MODEL), const2),
            pl.BlockSpec((D_MODEL, 3 * BRANCH_WIDTH), const2),
            pl.BlockSpec((D_MODEL, 3 * D_MODEL), const2),
            pl.BlockSpec((3, BRANCH_WIDTH, D_MODEL), lambda i: (0, 0, 0)),
            pl.BlockSpec((D_MODEL, D_MODEL), const2),
            pl.BlockSpec((1, D_MODEL), const2),
        ],
        out_specs=pl.BlockSpec((tm, D_MODEL), lambda i: (i, 0)),
        out_shape=jax.ShapeDtypeStruct((n_tok, D_MODEL), F32),
        compiler_params=pltpu.CompilerParams(
            dimension_semantics=("arbitrary",), vmem_limit_bytes=VMEM_LIMIT),
        name="merge",
    )(x2d, oa.reshape(n_tok, BRANCH_WIDTH), ob.reshape(n_tok, BRANCH_WIDTH),
      oc.reshape(n_tok, BRANCH_WIDTH), g, wz, wg, wb, wo, fg)


def _alibi_slopes(n_heads):
    return [2.0 ** (-ALIBI_MAX_EXP * (h + 1) / n_heads) for h in range(n_heads)]


def _alibi_key_bias(seq_len, slopes_per_pair, dilation=1):
    pos = np.arange(seq_len, dtype=np.float64).reshape(-1, dilation).T.reshape(-1)
    out = np.zeros((seq_len, BRANCH_WIDTH), dtype=BF16)
    for p, pair in enumerate(slopes_per_pair):
        for c, slope in enumerate(pair):
            rest = pos * (slope * LOG2E)
            for x in range(3):
                piece = rest.astype(BF16)
                out[:, p * LANES + 3 * c + x] = piece
                rest = rest - piece.astype(np.float64)
    return jnp.asarray(out)[None]


def _fox_bias_scatter():
    m = [[0.0] * BRANCH_WIDTH for _ in range(3 * LANES)]
    for x in range(3):
        for h in range(N_HEADS):
            m[x * LANES + h][(h // 2) * LANES + 3 * (h % 2) + x] = 1.0
    return jnp.asarray(m, dtype=BF16)


def kernel(x, norm_g, w_in, fox_fb, diff_lam, diff_norm_g, w_branch, w_out, final_g):
    batch, seq_len, _ = x.shape
    depth = norm_g.shape[0]
    n_tok = batch * seq_len
    scale = HEAD_DIM ** -0.5 * LOG2E

    w = BRANCH_WIDTH
    off = {}
    pos = 0
    for name, width in (("diff_q", w), ("diff_k", w), ("diff_v", w), ("diff_z", w),
                        ("fox_q", w), ("fox_k", w), ("fox_v", w), ("fox_f", N_HEADS),
                        ("fox_z", w), ("dil_q", w), ("dil_k", w), ("dil_v", w),
                        ("dil_z", w), ("merge_g", 3 * D_MODEL)):
        off[name] = (pos, pos + width)
        pos += width

    def cols(wl, name, mult=1.0):
        a, b = off[name]
        return wl[:, a:b] * mult if mult != 1.0 else wl[:, a:b]

    ds = _alibi_slopes(DIFF_HEADS)
    cs = _alibi_slopes(N_HEADS)
    diff_kb = _alibi_key_bias(seq_len, [(ds[h], ds[h]) for h in range(DIFF_HEADS)])
    dil_pairs = [(cs[2 * p], cs[2 * p + 1]) for p in range(N_PAIRS)]
    dil_kb = [_alibi_key_bias(seq_len, dil_pairs, d) for _, d in DIL_PATTERNS]
    pmat = _fox_bias_scatter()
    ones_g = jnp.ones((1, BRANCH_WIDTH), F32)
    zero_lam = jnp.zeros((4, LANES), F32)

    x2d = x.reshape(n_tok, D_MODEL)
    for l in range(depth):
        wl = w_in[l]
        w_qkv = jnp.concatenate(
            [cols(wl, "diff_q", scale), cols(wl, "diff_k"), cols(wl, "diff_v"),
             cols(wl, "fox_q", scale), cols(wl, "fox_k"), cols(wl, "fox_v"),
             cols(wl, "dil_q", scale), cols(wl, "dil_k"), cols(wl, "dil_v")],
            axis=1).astype(BF16)
        w_f = jnp.pad(cols(wl, "fox_f"), ((0, 0), (0, LANES - N_HEADS))).astype(BF16)
        fb = jnp.pad(fox_fb[l].astype(F32), (0, LANES - N_HEADS))[None]
        g = norm_g[l].astype(F32)[None]

        (dqT, dk, dvT, fqT, fk, fvT, fkb, cq, ck, cv) = _proj_call(
            x2d, g, w_qkv, w_f, fb, pmat, seq_len=seq_len)

        lam_init = 0.8 - 0.6 * math.exp(-0.3 * l)
        lam_pad = jnp.pad(diff_lam[l].astype(F32), ((0, 0), (0, LANES - HEAD_DIM)))
        oa = _pair_attn_call(dqT, dk, diff_kb, dvT, lam_pad,
                             diff_norm_g[l].astype(F32)[None],
                             batch=batch, seq_len=seq_len, diff=True, lam_init=lam_init)
        ob = _pair_attn_call(fqT, fk, fkb.reshape(batch, seq_len, BRANCH_WIDTH), fvT,
                             zero_lam, ones_g,
                             batch=batch, seq_len=seq_len, diff=False, lam_init=0.0)
        oc = _dil_call(cq, ck, cv, dil_kb, batch=batch, seq_len=seq_len)

        wz = jnp.concatenate([cols(wl, "diff_z"), cols(wl, "fox_z"), cols(wl, "dil_z")],
                             axis=1).astype(BF16)
        wg = cols(wl, "merge_g").astype(BF16)
        x2d = _merge_call(x2d, oa, ob, oc, g, wz, wg, w_branch[l].astype(BF16),
                          w_out[l].astype(BF16), final_g.astype(F32)[None],
                          last=(l == depth - 1))
    return x2d.reshape(batch, seq_len, D_MODEL)
```

```python
import functools
import math

import jax
import jax.numpy as jnp
import numpy as np
from jax import lax
from jax.experimental import pallas as pl
from jax.experimental.pallas import tpu as pltpu

F32 = jnp.float32
BF16 = jnp.bfloat16

D_MODEL = 1024
HEAD_DIM = 64
BRANCH_WIDTH = 512
LANES = 128
N_PAIRS = BRANCH_WIDTH // LANES
DIFF_HEADS = 4
N_HEADS = 8
DIL_PATTERNS = ((128, 1), (512, 4), (2048, 16))
DIL_BLOCK = 128
DIL_TQ = 256
DIL_GROUP = 2
RMS_EPS = 1e-6
LOG2E = math.log2(math.e)
ALIBI_MAX_EXP = 8.0
N_QKV_GROUPS = 9
TQ = 512
TK = 512
KV_SUBS = 2
TM_PROJ = 512
TM_MERGE = 512
VMEM_LIMIT = 56 * 1024 * 1024


def _split3(a):
    a1 = a.astype(BF16)
    r1 = a - a1.astype(F32)
    a2 = r1.astype(BF16)
    a3 = (r1 - a2.astype(F32)).astype(BF16)
    return a1, a2, a3


def _proj_kernel(x_ref, g_ref, w_ref, wf_ref, fb_ref, pmat_ref,
                 dqT_ref, dk_ref, dvT_ref, fqT_ref, fk_ref, fvT_ref, fkb_ref,
                 cq_ref, ck_ref, cv_ref, carry_ref, *, tiles_per_seq):
    i = pl.program_id(0)
    tm = x_ref.shape[0]
    x = x_ref[...]
    ms = jnp.mean(x * x, axis=-1, keepdims=True)
    h = (x * lax.rsqrt(ms + RMS_EPS) * g_ref[...]).astype(BF16)

    def proj(g):
        return jnp.dot(h, w_ref[:, g * BRANCH_WIDTH:(g + 1) * BRANCH_WIDTH],
                       preferred_element_type=F32)

    def put_t(ref, val):
        for t in range(tm // TQ):
            ref[t] = val[t * TQ:(t + 1) * TQ, :].T.astype(BF16)

    put_t(dqT_ref, proj(0))
    dk_ref[...] = proj(1).astype(BF16)
    put_t(dvT_ref, proj(2))
    put_t(fqT_ref, proj(3))
    fk_ref[...] = proj(4).astype(BF16)
    put_t(fvT_ref, proj(5))
    cq_ref[...] = proj(6).astype(BF16)
    ck_ref[...] = proj(7).astype(BF16)
    cv_ref[...] = proj(8).astype(BF16)

    f = jnp.dot(h, wf_ref[...], preferred_element_type=F32) + fb_ref[...]
    logf = jnp.minimum(f, 0.0) - jnp.log1p(jnp.exp(-jnp.abs(f)))

    @pl.when(i % tiles_per_seq == 0)
    def _():
        carry_ref[...] = jnp.zeros_like(carry_ref)

    row = lax.broadcasted_iota(jnp.int32, (tm, tm), 0)
    col = lax.broadcasted_iota(jnp.int32, (tm, tm), 1)
    tri = jnp.where(row >= col, 1.0, 0.0).astype(BF16)
    a1, a2, a3 = _split3(logf)
    c = (jnp.dot(tri, a1, preferred_element_type=F32)
         + jnp.dot(tri, a2, preferred_element_type=F32)
         + jnp.dot(tri, a3, preferred_element_type=F32)) + carry_ref[0:1, :]
    carry_ref[0:1, :] = c[tm - 1:tm, :]
    b1, b2, b3 = _split3(c * (-LOG2E))
    kb = jnp.dot(jnp.concatenate([b1, b2, b3], axis=1), pmat_ref[...],
                 preferred_element_type=F32)
    fkb_ref[...] = kb.astype(BF16)


def _proj_call(x2d, g, w_qkv, w_f, fb, pmat, *, seq_len):
    n_tok = x2d.shape[0]
    tm = TM_PROJ
    nt = n_tok // tm
    const = lambda i: (0, 0)
    tok = pl.BlockSpec((tm, BRANCH_WIDTH), lambda i: (i, 0))
    tr = pl.BlockSpec((tm // TQ, BRANCH_WIDTH, TQ), lambda i: (i, 0, 0))
    nat_shape = jax.ShapeDtypeStruct((n_tok, BRANCH_WIDTH), BF16)
    tr_shape = jax.ShapeDtypeStruct((n_tok // TQ, BRANCH_WIDTH, TQ), BF16)
    return pl.pallas_call(
        functools.partial(_proj_kernel, tiles_per_seq=seq_len // tm),
        grid=(nt,),
        in_specs=[
            pl.BlockSpec((tm, D_MODEL), lambda i: (i, 0)),
            pl.BlockSpec((1, D_MODEL), const),
            pl.BlockSpec((D_MODEL, N_QKV_GROUPS * BRANCH_WIDTH), const),
            pl.BlockSpec((D_MODEL, LANES), const),
            pl.BlockSpec((1, LANES), const),
            pl.BlockSpec((3 * LANES, BRANCH_WIDTH), const),
        ],
        out_specs=[tr, tok, tr, tr, tok, tr, tok, tok, tok, tok],
        out_shape=[tr_shape, nat_shape, tr_shape, tr_shape, nat_shape, tr_shape,
                   nat_shape, nat_shape, nat_shape, nat_shape],
        scratch_shapes=[pltpu.VMEM((8, LANES), F32)],
        compiler_params=pltpu.CompilerParams(
            dimension_semantics=("arbitrary",), vmem_limit_bytes=VMEM_LIMIT),
        name="proj",
    )(x2d, g, w_qkv, w_f, fb, pmat)


def _pair_attn_kernel(qT_ref, qTn_ref, k_ref, kb_ref, vT_ref, lam_ref, g_ref, o_ref,
                      qa_sc, m_sc, l_sc, acc_sc, s_sc, t0_sc, smax_sc, t0max_sc,
                      *, diff, lam_init, max_full):
    i = pl.program_id(2)
    par = i % 2
    row = lax.broadcasted_iota(jnp.int32, (LANES, TQ), 0)

    def build_qa(q_ref, slot):
        qT = q_ref[0, 0].astype(F32)
        qa_sc[slot, :, 0:TQ] = jnp.concatenate(
            [jnp.where(row < HEAD_DIM, qT, 0.0), jnp.where(row < 3, 1.0, 0.0)],
            axis=0).astype(BF16)
        qa_sc[slot, :, TQ:2 * TQ] = jnp.concatenate(
            [jnp.where(row >= HEAD_DIM, qT, 0.0),
             jnp.where((row >= 3) & (row < 6), 1.0, 0.0)], axis=0).astype(BF16)

    tile_keys = KV_SUBS * TK

    def score_rows(row0, n_rows, qa_slot, dst, dst_slot):
        ka = jnp.concatenate([k_ref[0, row0:row0 + n_rows, :],
                              kb_ref[0, row0:row0 + n_rows, :]], axis=1)
        s = jnp.dot(ka, qa_sc[qa_slot], preferred_element_type=F32)
        dst[dst_slot, row0:row0 + n_rows, :] = s
        return jnp.max(s, axis=0, keepdims=True)

    @pl.when(i == 0)
    def _():
        build_qa(qT_ref, 0)
        t0max_sc[0] = score_rows(0, TK, 0, t0_sc, 0)

    build_qa(qTn_ref, 1 - par)
    m_sc[...] = jnp.full(m_sc.shape, -jnp.inf, F32)
    l_sc[...] = jnp.zeros(l_sc.shape, F32)
    acc_sc[...] = jnp.zeros(acc_sc.shape, F32)

    def consume(s, sub0, n_sub, tile_max):
        rows = n_sub * TK
        if tile_max is None:
            kr = lax.broadcasted_iota(jnp.int32, (TK, 2 * TQ), 0)
            qc = lax.broadcasted_iota(jnp.int32, (TK, 2 * TQ), 1) % TQ
            tail = jnp.where(kr <= qc, s[rows - TK:], -jnp.inf)
            s = tail if n_sub == 1 else jnp.concatenate([s[:rows - TK], tail], axis=0)
            tile_max = jnp.max(s, axis=0, keepdims=True)
        m_old = m_sc[...]
        m_new = jnp.maximum(m_old, tile_max)
        alpha = jnp.exp2(m_old - m_new)
        p = jnp.exp2(s - m_new)
        l_sc[...] = alpha * l_sc[...] + jnp.sum(p, axis=0, keepdims=True)
        pb = p.astype(BF16)
        pv = jnp.dot(vT_ref[0, sub0], pb[0:TK], preferred_element_type=F32)
        for t in range(1, n_sub):
            pv = pv + jnp.dot(vT_ref[0, sub0 + t], pb[t * TK:(t + 1) * TK],
                              preferred_element_type=F32)
        acc_sc[...] = alpha * acc_sc[...] + pv
        m_sc[...] = m_new

    def full_block(cur, cur_max, cur_slot, sub0, t_next, n_host):
        m_old = m_sc[...]
        m_new = jnp.maximum(m_old, cur_max[cur_slot])
        alpha = jnp.exp2(m_old - m_new)
        nslot = t_next % 2
        lsum = pv = nmax = None
        for t in range(KV_SUBS):
            if t < n_host:
                kbase = pl.multiple_of(t_next * tile_keys + t * TK, TK)
                ka = jnp.concatenate([k_ref[0, pl.ds(kbase, TK), :],
                                      kb_ref[0, pl.ds(kbase, TK), :]], axis=1)
                sn = jnp.dot(ka, qa_sc[par], preferred_element_type=F32)
                s_sc[nslot, t * TK:(t + 1) * TK, :] = sn
                cm = jnp.max(sn, axis=0, keepdims=True)
                nmax = cm if nmax is None else jnp.maximum(nmax, cm)
            p = jnp.exp2(cur[cur_slot, t * TK:(t + 1) * TK, :] - m_new)
            ls = jnp.sum(p, axis=0, keepdims=True)
            lsum = ls if lsum is None else lsum + ls
            d = jnp.dot(vT_ref[0, sub0 + t], p.astype(BF16), preferred_element_type=F32)
            pv = d if pv is None else pv + d
        if n_host == KV_SUBS:
            smax_sc[nslot] = nmax
        l_sc[...] = alpha * l_sc[...] + lsum
        acc_sc[...] = alpha * acc_sc[...] + pv
        m_sc[...] = m_new

    def write_output():
        a = acc_sc[...] / l_sc[...]
        a0 = a[:, 0:TQ]
        a1 = a[:, TQ:2 * TQ]
        if diff:
            dl = lam_ref[...]
            lam = (jnp.exp(jnp.sum(dl[0:1] * dl[1:2], axis=1, keepdims=True))
                   - jnp.exp(jnp.sum(dl[2:3] * dl[3:4], axis=1, keepdims=True)) + lam_init)
            oT = a0 - lam * a1
            ms = jnp.mean(oT * oT, axis=0, keepdims=True)
            oT = oT * lax.rsqrt(ms + RMS_EPS)
            o_ref[0] = (oT.T * g_ref[...]) * (1.0 - lam_init)
        else:
            o_ref[0] = jnp.where(row < HEAD_DIM, a0, a1).T

    n_full = i // KV_SUBS
    rem = i % KV_SUBS
    for j in range(max_full):
        need = jnp.where(j + 1 == n_full, rem + 1, KV_SUBS)
        for n_host in range(1, KV_SUBS + 1):
            @pl.when((j < n_full) & (need == n_host))
            def _(j=j, n_host=n_host):
                if j == 0:
                    full_block(t0_sc, t0max_sc, par, 0, j + 1, n_host)
                else:
                    full_block(s_sc, smax_sc, j % 2, j * KV_SUBS, j + 1, n_host)
    for r in range(KV_SUBS):
        for diag_is_tile0 in (True, False):
            @pl.when((rem == r) & ((n_full == 0) == diag_is_tile0))
            def _(r=r, diag_is_tile0=diag_is_tile0):
                nslot = (r + 1) % 2
                half = tile_keys // 2
                max_a = score_rows(0, half, 1 - par, t0_sc, nslot)
                rows = (r + 1) * TK
                s = t0_sc[r % 2, 0:rows, :] if diag_is_tile0 else s_sc[n_full % 2, 0:rows, :]
                consume(s, n_full * KV_SUBS, r + 1, None)
                max_b = score_rows(half, half, 1 - par, t0_sc, nslot)
                t0max_sc[nslot] = jnp.maximum(max_a, max_b)
                write_output()


def _pair_attn_call(qT, k, kb, vT, lam_pad, g, *, batch, seq_len, diff, lam_init):
    nq = seq_len // TQ
    nk = seq_len // TK
    qT = qT.reshape(batch, nq, BRANCH_WIDTH, TQ)
    vT = vT.reshape(batch, nk, BRANCH_WIDTH, TK)
    k = k.reshape(batch, seq_len, BRANCH_WIDTH)
    kb_batched = kb.shape[0] != 1
    kb_map = (lambda b, p, i: (b, 0, p)) if kb_batched else (lambda b, p, i: (0, 0, p))
    return pl.pallas_call(
        functools.partial(_pair_attn_kernel, diff=diff, lam_init=lam_init,
                          max_full=seq_len // (KV_SUBS * TK) - 1),
        grid=(batch, N_PAIRS, nq),
        in_specs=[
            pl.BlockSpec((1, 1, LANES, TQ), lambda b, p, i: (b, i, p, 0)),
            pl.BlockSpec((1, 1, LANES, TQ),
                         lambda b, p, i: (b, jnp.minimum(i + 1, nq - 1), p, 0)),
            pl.BlockSpec((1, seq_len, LANES), lambda b, p, i: (b, 0, p)),
            pl.BlockSpec((1, seq_len, LANES), kb_map),
            pl.BlockSpec((1, nk, LANES, TK), lambda b, p, i: (b, 0, p, 0)),
            pl.BlockSpec((4, LANES), lambda b, p, i: (0, 0)),
            pl.BlockSpec((1, LANES), lambda b, p, i: (0, p)),
        ],
        out_specs=pl.BlockSpec((1, TQ, LANES), lambda b, p, i: (b, i, p)),
        out_shape=jax.ShapeDtypeStruct((batch, seq_len, BRANCH_WIDTH), F32),
        scratch_shapes=[
            pltpu.VMEM((2, 2 * LANES, 2 * TQ), BF16),
            pltpu.VMEM((1, 2 * TQ), F32),
            pltpu.VMEM((1, 2 * TQ), F32),
            pltpu.VMEM((LANES, 2 * TQ), F32),
            pltpu.VMEM((2, KV_SUBS * TK, 2 * TQ), F32),
            pltpu.VMEM((2, KV_SUBS * TK, 2 * TQ), F32),
            pltpu.VMEM((2, 1, 2 * TQ), F32),
            pltpu.VMEM((2, 1, 2 * TQ), F32),
        ],
        compiler_params=pltpu.CompilerParams(
            dimension_semantics=("arbitrary", "arbitrary", "arbitrary"),
            vmem_limit_bytes=VMEM_LIMIT),
        name="diff_attn" if diff else "fox_attn",
    )(qT, qT, k, kb, vT, lam_pad, g)


def _dil_kernel(q_ref, k_ref, v_ref, kb1_ref, kb4_ref, kb16_ref, o_ref,
                qf, kf, vf, q4f, k4f, v4f, q_sub, k_sub, v_sub, acc, m_sc, l_sc, acc4, m4, l4,
                mask_sc, s_buf, vT_buf, smax_buf, *, seq_len):
    L = DIL_BLOCK
    T = DIL_TQ
    qf[...] = q_ref[0].astype(F32)
    kf[...] = k_ref[0].astype(F32)
    vf[...] = v_ref[0].astype(F32)

    kr = lax.broadcasted_iota(jnp.int32, (2 * L, 2 * L), 0)
    qc = lax.broadcasted_iota(jnp.int32, (2 * L, 2 * L), 1) % L
    band = (kr >= qc) & (kr <= qc + L)
    mask_sc[0] = jnp.where(band & (kr >= L), 0.0, -jnp.inf)
    mask_sc[1] = jnp.where(band, 0.0, -jnp.inf)

    lane = lax.broadcasted_iota(jnp.int32, (T, LANES), 1)
    first = lane < HEAD_DIM
    sel0 = jnp.where(lane < 3, 1.0, 0.0).astype(BF16)
    sel1 = jnp.where((lane >= 3) & (lane < 6), 1.0, 0.0).astype(BF16)
    head0 = jnp.where(first, 1.0, 0.0).astype(BF16)
    head1 = jnp.where(first, 0.0, 1.0).astype(BF16)
    frow = lax.broadcasted_iota(jnp.int32, (LANES, L), 0) < HEAD_DIM
    n_qb = T // L

    assert tuple(d for _, d in DIL_PATTERNS) == (1, 4, 16)
    n4 = seq_len // 4
    n16 = seq_len // 16
    kb_refs = (kb1_ref, kb4_ref, kb16_ref)
    for pattern, (_, d) in enumerate(DIL_PATTERNS):
        nt = seq_len // (T * d)
        n = seq_len // d
        kb_ref = kb_refs[pattern]
        if d == 1:
            q_rows = lambda rows: q_ref[0, rows, :]
            k_rows = lambda rows: k_ref[0, rows, :]
            v_rows = lambda rows: v_ref[0, rows, :]
            m_st, l_st, a_st = m_sc, l_sc, acc
        else:
            if d == 4:
                for r in range(4):
                    dst = slice(r * n4, (r + 1) * n4)
                    for src_f, dst_f, dst_b in ((qf, q4f, q_sub), (kf, k4f, k_sub), (vf, v4f, v_sub)):
                        val = src_f[pl.ds(r, n4, stride=4), :]
                        dst_f[dst, :] = val
                        dst_b[dst, :] = val.astype(BF16)
                    for src_s, dst_s in ((m_sc, m4), (l_sc, l4), (acc, acc4)):
                        dst_s[dst, :] = src_s[pl.ds(r, n4, stride=4), :]
            else:
                for r in range(16):
                    src0 = (r % 4) * n4 + r // 4
                    dst = slice(r * n16, (r + 1) * n16)
                    for src_f, dst_b in ((q4f, q_sub), (k4f, k_sub), (v4f, v_sub)):
                        dst_b[dst, :] = src_f[pl.ds(src0, n16, stride=4), :].astype(BF16)
            q_rows = lambda rows: q_sub[rows, :]
            k_rows = lambda rows: k_sub[rows, :]
            v_rows = lambda rows: v_sub[rows, :]
            m_st, l_st, a_st = m4, l4, acc4

        n_groups = seq_len // (T * DIL_GROUP)

        def addr(u, g, d=d, nt=nt, n=n):
            t = u * DIL_GROUP + g
            r = t // nt
            it = t % nt
            start = pl.multiple_of(r * n + T * it, L)
            prev = pl.multiple_of(jnp.where(it == 0, start, start - L), L)
            if d == 16:
                state_rows = pl.ds((r % 4) * n4 + 4 * T * it + r // 4, T, stride=4)
            else:
                state_rows = pl.ds(start, T)
            return pl.ds(start, T), pl.ds(prev, L), jnp.minimum(it, 1), state_rows

        def score_group(u, buf, q_rows=q_rows, k_rows=k_rows, v_rows=v_rows, kb_ref=kb_ref):
            for g in range(DIL_GROUP):
                rows, prows, slot, _ = addr(u, g)
                qb = q_rows(rows)
                q0 = jnp.concatenate([qb * head0, sel0], axis=1)
                q1 = jnp.concatenate([qb * head1, sel1], axis=1)
                ka = jnp.concatenate(
                    [jnp.concatenate([k_rows(prows), kb_ref[0, prows, :]], axis=1),
                     jnp.concatenate([k_rows(rows), kb_ref[0, rows, :]], axis=1)],
                    axis=0)
                for b in range(n_qb):
                    qa = jnp.concatenate([q0[b * L:(b + 1) * L], q1[b * L:(b + 1) * L]], axis=0)
                    sb = lax.dot_general(ka[b * L:(b + 2) * L], qa, (((1,), (1,)), ((), ())),
                                         preferred_element_type=F32)
                    sb = sb + mask_sc[slot if b == 0 else 1]
                    s_buf[buf, g, b] = sb
                    smax_buf[buf, g, b] = jnp.max(sb, axis=0, keepdims=True)
                vv = jnp.concatenate([v_rows(prows), v_rows(rows)], axis=0)
                vT_buf[buf, g] = vv.astype(F32).T.astype(BF16)

        def consume_group(u, buf, first_pattern=(pattern == 0), m_st=m_st, l_st=l_st, a_st=a_st):
            num, m_blk, l_blk = [], [], []
            for g in range(DIL_GROUP):
                nb_, mb_, lb_ = [], [], []
                for b in range(n_qb):
                    mg = smax_buf[buf, g, b]
                    p = jnp.exp2(s_buf[buf, g, b] - mg)
                    lg = jnp.sum(p, axis=0, keepdims=True)
                    oT = jnp.dot(vT_buf[buf, g, :, b * L:(b + 2) * L], p.astype(BF16),
                                 preferred_element_type=F32)
                    nb_.append(jnp.where(frow, oT[:, :L], oT[:, L:]))
                    mb_.append(jnp.where(frow, jnp.broadcast_to(mg[:, :L], (LANES, L)),
                                         jnp.broadcast_to(mg[:, L:], (LANES, L))))
                    lb_.append(jnp.where(frow, jnp.broadcast_to(lg[:, :L], (LANES, L)),
                                         jnp.broadcast_to(lg[:, L:], (LANES, L))))
                num.append(jnp.concatenate(nb_, axis=1).T)
                m_blk.append(jnp.concatenate(mb_, axis=1).T)
                l_blk.append(jnp.concatenate(lb_, axis=1).T)
            for g in range(DIL_GROUP):
                _, _, _, rows = addr(u, g)
                if first_pattern:
                    l_st[rows, :] = l_blk[g]
                    a_st[rows, :] = num[g]
                    m_st[rows, :] = m_blk[g]
                    continue
                m_old = m_st[rows, :]
                m_new = jnp.maximum(m_old, m_blk[g])
                w_old = jnp.exp2(m_old - m_new)
                w_blk = jnp.exp2(m_blk[g] - m_new)
                l_st[rows, :] = w_old * l_st[rows, :] + w_blk * l_blk[g]
                a_st[rows, :] = w_old * a_st[rows, :] + w_blk * num[g]
                m_st[rows, :] = m_new

        score_group(0, 0)

        def trip(v, carry):
            score_group(2 * v + 1, 1)
            consume_group(2 * v, 0)
            score_group(jnp.minimum(2 * v + 2, n_groups - 1), 0)
            consume_group(2 * v + 1, 1)
            return carry

        lax.fori_loop(0, n_groups // 2, trip, 0)

    for r in range(4):
        rows4 = slice(r * n4, (r + 1) * n4)
        o_ref[0, pl.ds(r, n4, stride=4), :] = acc4[rows4, :] / l4[rows4, :]


def _dil_call(q, k, v, kbs, *, batch, seq_len):
    assert seq_len % (DIL_TQ * max(d for _, d in DIL_PATTERNS)) == 0
    assert seq_len % (DIL_TQ * DIL_GROUP * 2) == 0
    q = q.reshape(batch, seq_len, BRANCH_WIDTH)
    k = k.reshape(batch, seq_len, BRANCH_WIDTH)
    v = v.reshape(batch, seq_len, BRANCH_WIDTH)
    blk = pl.BlockSpec((1, seq_len, LANES), lambda b, p: (b, 0, p))
    kb_blk = pl.BlockSpec((1, seq_len, LANES), lambda b, p: (0, 0, p))
    n_qb = DIL_TQ // DIL_BLOCK
    return pl.pallas_call(
        functools.partial(_dil_kernel, seq_len=seq_len),
        grid=(batch, N_PAIRS),
        in_specs=[blk, blk, blk, kb_blk, kb_blk, kb_blk],
        out_specs=blk,
        out_shape=jax.ShapeDtypeStruct((batch, seq_len, BRANCH_WIDTH), F32),
        scratch_shapes=[pltpu.VMEM((seq_len, LANES), F32) for _ in range(6)]
        + [pltpu.VMEM((seq_len, LANES), BF16) for _ in range(3)]
        + [pltpu.VMEM((seq_len, LANES), F32) for _ in range(6)]
        + [pltpu.VMEM((2, 2 * DIL_BLOCK, 2 * DIL_BLOCK), F32),
           pltpu.VMEM((2, DIL_GROUP, n_qb, 2 * DIL_BLOCK, 2 * DIL_BLOCK), F32),
           pltpu.VMEM((2, DIL_GROUP, LANES, DIL_TQ + DIL_BLOCK), BF16),
           pltpu.VMEM((2, DIL_GROUP, n_qb, 1, 2 * DIL_BLOCK), F32)],
        compiler_params=pltpu.CompilerParams(
            dimension_semantics=("arbitrary", "arbitrary"), vmem_limit_bytes=VMEM_LIMIT),
        name="dil_attn",
    )(q, k, v, *kbs)


def _merge_kernel(x_ref, oa_ref, ob_ref, oc_ref, g_ref, wz_ref, wg_ref, wb_ref, wo_ref,
                  fg_ref, out_ref, *, last):
    x = x_ref[...]
    ms = jnp.mean(x * x, axis=-1, keepdims=True)
    h = (x * lax.rsqrt(ms + RMS_EPS) * g_ref[...]).astype(BF16)
    merged = None
    for n, o_ref in enumerate((oa_ref, ob_ref, oc_ref)):
        z = jnp.dot(h, wz_ref[:, n * BRANCH_WIDTH:(n + 1) * BRANCH_WIDTH],
                    preferred_element_type=F32)
        y = (o_ref[...] * (z * jax.nn.sigmoid(z))).astype(BF16)
        br = jnp.dot(y, wb_ref[n], preferred_element_type=F32)
        gate = jax.nn.sigmoid(jnp.dot(h, wg_ref[:, n * D_MODEL:(n + 1) * D_MODEL],
                                      preferred_element_type=F32))
        merged = gate * br if merged is None else merged + gate * br
    out = x + jnp.dot(merged.astype(BF16), wo_ref[...], preferred_element_type=F32)
    if last:
        ms = jnp.mean(out * out, axis=-1, keepdims=True)
        out = out * lax.rsqrt(ms + RMS_EPS) * fg_ref[...]
    out_ref[...] = out


def _merge_call(x2d, oa, ob, oc, g, wz, wg, wb, wo, fg, *, last):
    n_tok = x2d.shape[0]
    tm = TM_MERGE
    const2 = lambda i: (0, 0)
    tok = pl.BlockSpec((tm, BRANCH_WIDTH), lambda i: (i, 0))
    return pl.pallas_call(
        functools.partial(_merge_kernel, last=last),
        grid=(n_tok // tm,),
        in_specs=[
            pl.BlockSpec((tm, D_MODEL), lambda i: (i, 0)), tok, tok, tok,
            pl.BlockSpec((1, D_MODEL), const2),
            pl.BlockSpec((D_MODEL, 3 * BRANCH_WIDTH), const2),
            pl.BlockSpec((D_MODEL, 3 * D_MODEL), const2),
            pl.BlockSpec((3, BRANCH_WIDTH, D_MODEL), lambda i: (0, 0, 0)),
            pl.BlockSpec((D_MODEL, D_MODEL), const2),
            pl.BlockSpec((1, D_MODEL), const2),
        ],
        out_specs=pl.BlockSpec((tm, D_MODEL), lambda i: (i, 0)),
        out_shape=jax.ShapeDtypeStruct((n_tok, D_MODEL), F32),
        compiler_params=pltpu.CompilerParams(
            dimension_semantics=("arbitrary",), vmem_limit_bytes=VMEM_LIMIT),
        name="merge",
    )(x2d, oa.reshape(n_tok, BRANCH_WIDTH), ob.reshape(n_tok, BRANCH_WIDTH),
      oc.reshape(n_tok, BRANCH_WIDTH), g, wz, wg, wb, wo, fg)


def _alibi_slopes(n_heads):
    return [2.0 ** (-ALIBI_MAX_EXP * (h + 1) / n_heads) for h in range(n_heads)]


def _alibi_key_bias(seq_len, slopes_per_pair, dilation=1):
    pos = np.arange(seq_len, dtype=np.float64).reshape(-1, dilation).T.reshape(-1)
    out = np.zeros((seq_len, BRANCH_WIDTH), dtype=BF16)
    for p, pair in enumerate(slopes_per_pair):
        for c, slope in enumerate(pair):
            rest = pos * (slope * LOG2E)
            for x in range(3):
                piece = rest.astype(BF16)
                out[:, p * LANES + 3 * c + x] = piece
                rest = rest - piece.astype(np.float64)
    return jnp.asarray(out)[None]


def _fox_bias_scatter():
    m = [[0.0] * BRANCH_WIDTH for _ in range(3 * LANES)]
    for x in range(3):
        for h in range(N_HEADS):
            m[x * LANES + h][(h // 2) * LANES + 3 * (h % 2) + x] = 1.0
    return jnp.asarray(m, dtype=BF16)


def kernel(x, norm_g, w_in, fox_fb, diff_lam, diff_norm_g, w_branch, w_out, final_g):
    batch, seq_len, _ = x.shape
    depth = norm_g.shape[0]
    n_tok = batch * seq_len
    scale = HEAD_DIM ** -0.5 * LOG2E

    w = BRANCH_WIDTH
    off = {}
    pos = 0
    for name, width in (("diff_q", w), ("diff_k", w), ("diff_v", w), ("diff_z", w),
                        ("fox_q", w), ("fox_k", w), ("fox_v", w), ("fox_f", N_HEADS),
                        ("fox_z", w), ("dil_q", w), ("dil_k", w), ("dil_v", w),
                        ("dil_z", w), ("merge_g", 3 * D_MODEL)):
        off[name] = (pos, pos + width)
        pos += width

    def cols(wl, name, mult=1.0):
        a, b = off[name]
        return wl[:, a:b] * mult if mult != 1.0 else wl[:, a:b]

    ds = _alibi_slopes(DIFF_HEADS)
    cs = _alibi_slopes(N_HEADS)
    diff_kb = _alibi_key_bias(seq_len, [(ds[h], ds[h]) for h in range(DIFF_HEADS)])
    dil_pairs = [(cs[2 * p], cs[2 * p + 1]) for p in range(N_PAIRS)]
    dil_kb = [_alibi_key_bias(seq_len, dil_pairs, d) for _, d in DIL_PATTERNS]
    pmat = _fox_bias_scatter()
    ones_g = jnp.ones((1, BRANCH_WIDTH), F32)
    zero_lam = jnp.zeros((4, LANES), F32)

    x2d = x.reshape(n_tok, D_MODEL)
    for l in range(depth):
        wl = w_in[l]
        w_qkv = jnp.concatenate(
            [cols(wl, "diff_q", scale), cols(wl, "diff_k"), cols(wl, "diff_v"),
             cols(wl, "fox_q", scale), cols(wl, "fox_k"), cols(wl, "fox_v"),
             cols(wl, "dil_q", scale), cols(wl, "dil_k"), cols(wl, "dil_v")],
            axis=1).astype(BF16)
        w_f = jnp.pad(cols(wl, "fox_f"), ((0, 0), (0, LANES - N_HEADS))).astype(BF16)
        fb = jnp.pad(fox_fb[l].astype(F32), (0, LANES - N_HEADS))[None]
        g = norm_g[l].astype(F32)[None]

        (dqT, dk, dvT, fqT, fk, fvT, fkb, cq, ck, cv) = _proj_call(
            x2d, g, w_qkv, w_f, fb, pmat, seq_len=seq_len)

        lam_init = 0.8 - 0.6 * math.exp(-0.3 * l)
        lam_pad = jnp.pad(diff_lam[l].astype(F32), ((0, 0), (0, LANES - HEAD_DIM)))
        oa = _pair_attn_call(dqT, dk, diff_kb, dvT, lam_pad,
                             diff_norm_g[l].astype(F32)[None],
                             batch=batch, seq_len=seq_len, diff=True, lam_init=lam_init)
        ob = _pair_attn_call(fqT, fk, fkb.reshape(batch, seq_len, BRANCH_WIDTH), fvT,
                             zero_lam, ones_g,
                             batch=batch, seq_len=seq_len, diff=False, lam_init=0.0)
        oc = _dil_call(cq, ck, cv, dil_kb, batch=batch, seq_len=seq_len)

        wz = jnp.concatenate([cols(wl, "diff_z"), cols(wl, "fox_z"), cols(wl, "dil_z")],
                             axis=1).astype(BF16)
        wg = cols(wl, "merge_g").astype(BF16)
        x2d = _merge_call(x2d, oa, ob, oc, g, wz, wg, w_branch[l].astype(BF16),
                          w_out[l].astype(BF16), final_g.astype(F32)[None],
                          last=(l == depth - 1))
    return x2d.reshape(batch, seq_len, D_MODEL)
```

```python
import functools
import math

import jax
import jax.numpy as jnp
import numpy as np
from jax import lax
from jax.experimental import pallas as pl
from jax.experimental.pallas import tpu as pltpu

F32 = jnp.float32
BF16 = jnp.bfloat16

D_MODEL = 1024
HEAD_DIM = 64
BRANCH_WIDTH = 512
LANES = 128
N_PAIRS = BRANCH_WIDTH // LANES
DIFF_HEADS = 4
N_HEADS = 8
DIL_PATTERNS = ((128, 1), (512, 4), (2048, 16))
DIL_BLOCK = 128
DIL_TQ = 256
DIL_GROUP = 2
RMS_EPS = 1e-6
LOG2E = math.log2(math.e)
ALIBI_MAX_EXP = 8.0
N_QKV_GROUPS = 9
TQ = 512
TK = 512
KV_SUBS = 2
TM_PROJ = 512
TM_MERGE = 512
VMEM_LIMIT = 56 * 1024 * 1024


def _split3(a):
    a1 = a.astype(BF16)
    r1 = a - a1.astype(F32)
    a2 = r1.astype(BF16)
    a3 = (r1 - a2.astype(F32)).astype(BF16)
    return a1, a2, a3


def _proj_kernel(x_ref, g_ref, w_ref, wf_ref, fb_ref, pmat_ref,
                 dqT_ref, dk_ref, dvT_ref, fqT_ref, fk_ref, fvT_ref, fkb_ref,
                 cq_ref, ck_ref, cv_ref, carry_ref, *, tiles_per_seq):
    i = pl.program_id(0)
    tm = x_ref.shape[0]
    x = x_ref[...]
    ms = jnp.mean(x * x, axis=-1, keepdims=True)
    h = (x * lax.rsqrt(ms + RMS_EPS) * g_ref[...]).astype(BF16)

    def proj(g):
        return jnp.dot(h, w_ref[:, g * BRANCH_WIDTH:(g + 1) * BRANCH_WIDTH],
                       preferred_element_type=F32)

    def put_t(ref, val):
        for t in range(tm // TQ):
            ref[t] = val[t * TQ:(t + 1) * TQ, :].T.astype(BF16)

    put_t(dqT_ref, proj(0))
    dk_ref[...] = proj(1).astype(BF16)
    put_t(dvT_ref, proj(2))
    put_t(fqT_ref, proj(3))
    fk_ref[...] = proj(4).astype(BF16)
    put_t(fvT_ref, proj(5))
    cq_ref[...] = proj(6).astype(BF16)
    ck_ref[...] = proj(7).astype(BF16)
    cv_ref[...] = proj(8).astype(BF16)

    f = jnp.dot(h, wf_ref[...], preferred_element_type=F32) + fb_ref[...]
    logf = jnp.minimum(f, 0.0) - jnp.log1p(jnp.exp(-jnp.abs(f)))

    @pl.when(i % tiles_per_seq == 0)
    def _():
        carry_ref[...] = jnp.zeros_like(carry_ref)

    row = lax.broadcasted_iota(jnp.int32, (tm, tm), 0)
    col = lax.broadcasted_iota(jnp.int32, (tm, tm), 1)
    tri = jnp.where(row >= col, 1.0, 0.0).astype(BF16)
    a1, a2, a3 = _split3(logf)
    c = (jnp.dot(tri, a1, preferred_element_type=F32)
         + jnp.dot(tri, a2, preferred_element_type=F32)
         + jnp.dot(tri, a3, preferred_element_type=F32)) + carry_ref[0:1, :]
    carry_ref[0:1, :] = c[tm - 1:tm, :]
    b1, b2, b3 = _split3(c * (-LOG2E))
    kb = jnp.dot(jnp.concatenate([b1, b2, b3], axis=1), pmat_ref[...],
                 preferred_element_type=F32)
    fkb_ref[...] = kb.astype(BF16)


def _proj_call(x2d, g, w_qkv, w_f, fb, pmat, *, seq_len):
    n_tok = x2d.shape[0]
    tm = TM_PROJ
    nt = n_tok // tm
    const = lambda i: (0, 0)
    tok = pl.BlockSpec((tm, BRANCH_WIDTH), lambda i: (i, 0))
    tr = pl.BlockSpec((tm // TQ, BRANCH_WIDTH, TQ), lambda i: (i, 0, 0))
    nat_shape = jax.ShapeDtypeStruct((n_tok, BRANCH_WIDTH), BF16)
    tr_shape = jax.ShapeDtypeStruct((n_tok // TQ, BRANCH_WIDTH, TQ), BF16)
    return pl.pallas_call(
        functools.partial(_proj_kernel, tiles_per_seq=seq_len // tm),
        grid=(nt,),
        in_specs=[
            pl.BlockSpec((tm, D_MODEL), lambda i: (i, 0)),
            pl.BlockSpec((1, D_MODEL), const),
            pl.BlockSpec((D_MODEL, N_QKV_GROUPS * BRANCH_WIDTH), const),
            pl.BlockSpec((D_MODEL, LANES), const),
            pl.BlockSpec((1, LANES), const),
            pl.BlockSpec((3 * LANES, BRANCH_WIDTH), const),
        ],
        out_specs=[tr, tok, tr, tr, tok, tr, tok, tok, tok, tok],
        out_shape=[tr_shape, nat_shape, tr_shape, tr_shape, nat_shape, tr_shape,
                   nat_shape, nat_shape, nat_shape, nat_shape],
        scratch_shapes=[pltpu.VMEM((8, LANES), F32)],
        compiler_params=pltpu.CompilerParams(
            dimension_semantics=("arbitrary",), vmem_limit_bytes=VMEM_LIMIT),
        name="proj",
    )(x2d, g, w_qkv, w_f, fb, pmat)


def _pair_attn_kernel(qT_ref, qTn_ref, k_ref, kb_ref, vT_ref, lam_ref, g_ref, o_ref,
                      qa_sc, m_sc, l_sc, acc_sc, s_sc, t0_sc, smax_sc, t0max_sc,
                      *, diff, lam_init, max_full):
    i = pl.program_id(2)
    par = i % 2
    row = lax.broadcasted_iota(jnp.int32, (LANES, TQ), 0)

    def build_qa(q_ref, slot):
        qT = q_ref[0, 0].astype(F32)
        qa_sc[slot, :, 0:TQ] = jnp.concatenate(
            [jnp.where(row < HEAD_DIM, qT, 0.0), jnp.where(row < 3, 1.0, 0.0)],
            axis=0).astype(BF16)
        qa_sc[slot, :, TQ:2 * TQ] = jnp.concatenate(
            [jnp.where(row >= HEAD_DIM, qT, 0.0),
             jnp.where((row >= 3) & (row < 6), 1.0, 0.0)], axis=0).astype(BF16)

    tile_keys = KV_SUBS * TK

    def score_rows(row0, n_rows, qa_slot, dst, dst_slot):
        ka = jnp.concatenate([k_ref[0, row0:row0 + n_rows, :],
                              kb_ref[0, row0:row0 + n_rows, :]], axis=1)
        s = jnp.dot(ka, qa_sc[qa_slot], preferred_element_type=F32)
        dst[dst_slot, row0:row0 + n_rows, :] = s
        return jnp.max(s, axis=0, keepdims=True)

    @pl.when(i == 0)
    def _():
        build_qa(qT_ref, 0)
        t0max_sc[0] = score_rows(0, TK, 0, t0_sc, 0)

    build_qa(qTn_ref, 1 - par)
    m_sc[...] = jnp.full(m_sc.shape, -jnp.inf, F32)
    l_sc[...] = jnp.zeros(l_sc.shape, F32)
    acc_sc[...] = jnp.zeros(acc_sc.shape, F32)

    def consume(s, sub0, n_sub, tile_max):
        rows = n_sub * TK
        if tile_max is None:
            kr = lax.broadcasted_iota(jnp.int32, (TK, 2 * TQ), 0)
            qc = lax.broadcasted_iota(jnp.int32, (TK, 2 * TQ), 1) % TQ
            tail = jnp.where(kr <= qc, s[rows - TK:], -jnp.inf)
            s = tail if n_sub == 1 else jnp.concatenate([s[:rows - TK], tail], axis=0)
            tile_max = jnp.max(s, axis=0, keepdims=True)
        m_old = m_sc[...]
        m_new = jnp.maximum(m_old, tile_max)
        alpha = jnp.exp2(m_old - m_new)
        p = jnp.exp2(s - m_new)
        l_sc[...] = alpha * l_sc[...] + jnp.sum(p, axis=0, keepdims=True)
        pb = p.astype(BF16)
        pv = jnp.dot(vT_ref[0, sub0], pb[0:TK], preferred_element_type=F32)
        for t in range(1, n_sub):
            pv = pv + jnp.dot(vT_ref[0, sub0 + t], pb[t * TK:(t + 1) * TK],
                              preferred_element_type=F32)
        acc_sc[...] = alpha * acc_sc[...] + pv
        m_sc[...] = m_new

    def full_block(cur, cur_max, cur_slot, sub0, t_next, n_host):
        m_old = m_sc[...]
        m_new = jnp.maximum(m_old, cur_max[cur_slot])
        alpha = jnp.exp2(m_old - m_new)
        nslot = t_next % 2
        lsum = pv = nmax = None
        for t in range(KV_SUBS):
            if t < n_host:
                kbase = pl.multiple_of(t_next * tile_keys + t * TK, TK)
                ka = jnp.concatenate([k_ref[0, pl.ds(kbase, TK), :],
                                      kb_ref[0, pl.ds(kbase, TK), :]], axis=1)
                sn = jnp.dot(ka, qa_sc[par], preferred_element_type=F32)
                s_sc[nslot, t * TK:(t + 1) * TK, :] = sn
                cm = jnp.max(sn, axis=0, keepdims=True)
                nmax = cm if nmax is None else jnp.maximum(nmax, cm)
            p = jnp.exp2(cur[cur_slot, t * TK:(t + 1) * TK, :] - m_new)
            ls = jnp.sum(p, axis=0, keepdims=True)
            lsum = ls if lsum is None else lsum + ls
            d = jnp.dot(vT_ref[0, sub0 + t], p.astype(BF16), preferred_element_type=F32)
            pv = d if pv is None else pv + d
        if n_host == KV_SUBS:
            smax_sc[nslot] = nmax
        l_sc[...] = alpha * l_sc[...] + lsum
        acc_sc[...] = alpha * acc_sc[...] + pv
        m_sc[...] = m_new

    def write_output():
        a = acc_sc[...] / l_sc[...]
        a0 = a[:, 0:TQ]
        a1 = a[:, TQ:2 * TQ]
        if diff:
            dl = lam_ref[...]
            lam = (jnp.exp(jnp.sum(dl[0:1] * dl[1:2], axis=1, keepdims=True))
                   - jnp.exp(jnp.sum(dl[2:3] * dl[3:4], axis=1, keepdims=True)) + lam_init)
            oT = a0 - lam * a1
            ms = jnp.mean(oT * oT, axis=0, keepdims=True)
            oT = oT * lax.rsqrt(ms + RMS_EPS)
            o_ref[0] = (oT.T * g_ref[...]) * (1.0 - lam_init)
        else:
            o_ref[0] = jnp.where(row < HEAD_DIM, a0, a1).T

    n_full = i // KV_SUBS
    rem = i % KV_SUBS
    def diag_block(r, s, sub0):
        nslot = (r + 1) % 2
        half = tile_keys // 2
        max_a = score_rows(0, half, 1 - par, t0_sc, nslot)
        consume(s, sub0, r + 1, None)
        max_b = score_rows(half, half, 1 - par, t0_sc, nslot)
        t0max_sc[nslot] = jnp.maximum(max_a, max_b)
        write_output()

    def run_full(j, n_host):
        if j == 0:
            full_block(t0_sc, t0max_sc, par, 0, j + 1, n_host)
        else:
            full_block(s_sc, smax_sc, j % 2, j * KV_SUBS, j + 1, n_host)

    for j in range(max_full):
        @pl.when(j + 1 < n_full)
        def _(j=j):
            run_full(j, KV_SUBS)
        for r in range(KV_SUBS):
            @pl.when((j + 1 == n_full) & (rem == r))
            def _(j=j, r=r):
                run_full(j, r + 1)
                diag_block(r, s_sc[(j + 1) % 2, 0:(r + 1) * TK, :], (j + 1) * KV_SUBS)
    for r in range(KV_SUBS):
        @pl.when((n_full == 0) & (rem == r))
        def _(r=r):
            diag_block(r, t0_sc[r % 2, 0:(r + 1) * TK, :], 0)


def _pair_attn_call(qT, k, kb, vT, lam_pad, g, *, batch, seq_len, diff, lam_init):
    nq = seq_len // TQ
    nk = seq_len // TK
    qT = qT.reshape(batch, nq, BRANCH_WIDTH, TQ)
    vT = vT.reshape(batch, nk, BRANCH_WIDTH, TK)
    k = k.reshape(batch, seq_len, BRANCH_WIDTH)
    kb_batched = kb.shape[0] != 1
    kb_map = (lambda b, p, i: (b, 0, p)) if kb_batched else (lambda b, p, i: (0, 0, p))
    return pl.pallas_call(
        functools.partial(_pair_attn_kernel, diff=diff, lam_init=lam_init,
                          max_full=seq_len // (KV_SUBS * TK) - 1),
        grid=(batch, N_PAIRS, nq),
        in_specs=[
            pl.BlockSpec((1, 1, LANES, TQ), lambda b, p, i: (b, i, p, 0)),
            pl.BlockSpec((1, 1, LANES, TQ),
                         lambda b, p, i: (b, jnp.minimum(i + 1, nq - 1), p, 0)),
            pl.BlockSpec((1, seq_len, LANES), lambda b, p, i: (b, 0, p)),
            pl.BlockSpec((1, seq_len, LANES), kb_map),
            pl.BlockSpec((1, nk, LANES, TK), lambda b, p, i: (b, 0, p, 0)),
            pl.BlockSpec((4, LANES), lambda b, p, i: (0, 0)),
            pl.BlockSpec((1, LANES), lambda b, p, i: (0, p)),
        ],
        out_specs=pl.BlockSpec((1, TQ, LANES), lambda b, p, i: (b, i, p)),
        out_shape=jax.ShapeDtypeStruct((batch, seq_len, BRANCH_WIDTH), F32),
        scratch_shapes=[
            pltpu.VMEM((2, 2 * LANES, 2 * TQ), BF16),
            pltpu.VMEM((1, 2 * TQ), F32),
            pltpu.VMEM((1, 2 * TQ), F32),
            pltpu.VMEM((LANES, 2 * TQ), F32),
            pltpu.VMEM((2, KV_SUBS * TK, 2 * TQ), F32),
            pltpu.VMEM((2, KV_SUBS * TK, 2 * TQ), F32),
            pltpu.VMEM((2, 1, 2 * TQ), F32),
            pltpu.VMEM((2, 1, 2 * TQ), F32),
        ],
        compiler_params=pltpu.CompilerParams(
            dimension_semantics=("arbitrary", "arbitrary", "arbitrary"),
            vmem_limit_bytes=VMEM_LIMIT),
        name="diff_attn" if diff else "fox_attn",
    )(qT, qT, k, kb, vT, lam_pad, g)


def _dil_kernel(q_ref, k_ref, v_ref, kb1_ref, kb4_ref, kb16_ref, o_ref,
                qf, kf, vf, q4f, k4f, v4f, q_sub, k_sub, v_sub, acc, m_sc, l_sc, acc4, m4, l4,
                mask_sc, s_buf, vT_buf, smax_buf, *, seq_len):
    L = DIL_BLOCK
    T = DIL_TQ
    qf[...] = q_ref[0].astype(F32)
    kf[...] = k_ref[0].astype(F32)
    vf[...] = v_ref[0].astype(F32)

    kr = lax.broadcasted_iota(jnp.int32, (2 * L, 2 * L), 0)
    qc = lax.broadcasted_iota(jnp.int32, (2 * L, 2 * L), 1) % L
    band = (kr >= qc) & (kr <= qc + L)
    mask_sc[0] = jnp.where(band & (kr >= L), 0.0, -jnp.inf)
    mask_sc[1] = jnp.where(band, 0.0, -jnp.inf)

    lane = lax.broadcasted_iota(jnp.int32, (T, LANES), 1)
    first = lane < HEAD_DIM
    sel0 = jnp.where(lane < 3, 1.0, 0.0).astype(BF16)
    sel1 = jnp.where((lane >= 3) & (lane < 6), 1.0, 0.0).astype(BF16)
    head0 = jnp.where(first, 1.0, 0.0).astype(BF16)
    head1 = jnp.where(first, 0.0, 1.0).astype(BF16)
    frow = lax.broadcasted_iota(jnp.int32, (LANES, L), 0) < HEAD_DIM
    n_qb = T // L

    assert tuple(d for _, d in DIL_PATTERNS) == (1, 4, 16)
    n4 = seq_len // 4
    n16 = seq_len // 16
    kb_refs = (kb1_ref, kb4_ref, kb16_ref)
    for pattern, (_, d) in enumerate(DIL_PATTERNS):
        nt = seq_len // (T * d)
        n = seq_len // d
        kb_ref = kb_refs[pattern]
        if d == 1:
            q_rows = lambda rows: q_ref[0, rows, :]
            k_rows = lambda rows: k_ref[0, rows, :]
            v_rows = lambda rows: v_ref[0, rows, :]
            m_st, l_st, a_st = m_sc, l_sc, acc
        else:
            if d == 4:
                for r in range(4):
                    dst = slice(r * n4, (r + 1) * n4)
                    for src_f, dst_f, dst_b in ((qf, q4f, q_sub), (kf, k4f, k_sub), (vf, v4f, v_sub)):
                        val = src_f[pl.ds(r, n4, stride=4), :]
                        dst_f[dst, :] = val
                        dst_b[dst, :] = val.astype(BF16)
                    for src_s, dst_s in ((m_sc, m4), (l_sc, l4), (acc, acc4)):
                        dst_s[dst, :] = src_s[pl.ds(r, n4, stride=4), :]
            else:
                for r in range(16):
                    src0 = (r % 4) * n4 + r // 4
                    dst = slice(r * n16, (r + 1) * n16)
                    for src_f, dst_b in ((q4f, q_sub), (k4f, k_sub), (v4f, v_sub)):
                        dst_b[dst, :] = src_f[pl.ds(src0, n16, stride=4), :].astype(BF16)
            q_rows = lambda rows: q_sub[rows, :]
            k_rows = lambda rows: k_sub[rows, :]
            v_rows = lambda rows: v_sub[rows, :]
            m_st, l_st, a_st = m4, l4, acc4

        n_groups = seq_len // (T * DIL_GROUP)

        def addr(u, g, d=d, nt=nt, n=n):
            t = u * DIL_GROUP + g
            r = t // nt
            it = t % nt
            start = pl.multiple_of(r * n + T * it, L)
            prev = pl.multiple_of(jnp.where(it == 0, start, start - L), L)
            if d == 16:
                state_rows = pl.ds((r % 4) * n4 + 4 * T * it + r // 4, T, stride=4)
            else:
                state_rows = pl.ds(start, T)
            return pl.ds(start, T), pl.ds(prev, L), jnp.minimum(it, 1), state_rows

        def score_group(u, buf, q_rows=q_rows, k_rows=k_rows, v_rows=v_rows, kb_ref=kb_ref):
            for g in range(DIL_GROUP):
                rows, prows, slot, _ = addr(u, g)
                qb = q_rows(rows)
                q0 = jnp.concatenate([qb * head0, sel0], axis=1)
                q1 = jnp.concatenate([qb * head1, sel1], axis=1)
                ka = jnp.concatenate(
                    [jnp.concatenate([k_rows(prows), kb_ref[0, prows, :]], axis=1),
                     jnp.concatenate([k_rows(rows), kb_ref[0, rows, :]], axis=1)],
                    axis=0)
                for b in range(n_qb):
                    qa = jnp.concatenate([q0[b * L:(b + 1) * L], q1[b * L:(b + 1) * L]], axis=0)
                    sb = lax.dot_general(ka[b * L:(b + 2) * L], qa, (((1,), (1,)), ((), ())),
                                         preferred_element_type=F32)
                    sb = sb + mask_sc[slot if b == 0 else 1]
                    s_buf[buf, g, b] = sb
                    smax_buf[buf, g, b] = jnp.max(sb, axis=0, keepdims=True)
                vv = jnp.concatenate([v_rows(prows), v_rows(rows)], axis=0)
                vT_buf[buf, g] = vv.astype(F32).T.astype(BF16)

        def consume_group(u, buf, first_pattern=(pattern == 0), m_st=m_st, l_st=l_st, a_st=a_st):
            num, m_blk, l_blk = [], [], []
            for g in range(DIL_GROUP):
                nb_, mb_, lb_ = [], [], []
                for b in range(n_qb):
                    mg = smax_buf[buf, g, b]
                    p = jnp.exp2(s_buf[buf, g, b] - mg)
                    lg = jnp.sum(p, axis=0, keepdims=True)
                    oT = jnp.dot(vT_buf[buf, g, :, b * L:(b + 2) * L], p.astype(BF16),
                                 preferred_element_type=F32)
                    nb_.append(jnp.where(frow, oT[:, :L], oT[:, L:]))
                    mb_.append(jnp.where(frow, jnp.broadcast_to(mg[:, :L], (LANES, L)),
                                         jnp.broadcast_to(mg[:, L:], (LANES, L))))
                    lb_.append(jnp.where(frow, jnp.broadcast_to(lg[:, :L], (LANES, L)),
                                         jnp.broadcast_to(lg[:, L:], (LANES, L))))
                num.append(jnp.concatenate(nb_, axis=1).T)
                m_blk.append(jnp.concatenate(mb_, axis=1).T)
                l_blk.append(jnp.concatenate(lb_, axis=1).T)
            for g in range(DIL_GROUP):
                _, _, _, rows = addr(u, g)
                if first_pattern:
                    l_st[rows, :] = l_blk[g]
                    a_st[rows, :] = num[g]
                    m_st[rows, :] = m_blk[g]
                    continue
                m_old = m_st[rows, :]
                m_new = jnp.maximum(m_old, m_blk[g])
                w_old = jnp.exp2(m_old - m_new)
                w_blk = jnp.exp2(m_blk[g] - m_new)
                l_st[rows, :] = w_old * l_st[rows, :] + w_blk * l_blk[g]
                a_st[rows, :] = w_old * a_st[rows, :] + w_blk * num[g]
                m_st[rows, :] = m_new

        score_group(0, 0)

        def trip(v, carry):
            score_group(2 * v + 1, 1)
            consume_group(2 * v, 0)
            score_group(jnp.minimum(2 * v + 2, n_groups - 1), 0)
            consume_group(2 * v + 1, 1)
            return carry

        lax.fori_loop(0, n_groups // 2, trip, 0)

    for r in range(4):
        rows4 = slice(r * n4, (r + 1) * n4)
        o_ref[0, pl.ds(r, n4, stride=4), :] = acc4[rows4, :] / l4[rows4, :]


def _dil_call(q, k, v, kbs, *, batch, seq_len):
    assert seq_len % (DIL_TQ * max(d for _, d in DIL_PATTERNS)) == 0
    assert seq_len % (DIL_TQ * DIL_GROUP * 2) == 0
    q = q.reshape(batch, seq_len, BRANCH_WIDTH)
    k = k.reshape(batch, seq_len, BRANCH_WIDTH)
    v = v.reshape(batch, seq_len, BRANCH_WIDTH)
    blk = pl.BlockSpec((1, seq_len, LANES), lambda b, p: (b, 0, p))
    kb_blk = pl.BlockSpec((1, seq_len, LANES), lambda b, p: (0, 0, p))
    n_qb = DIL_TQ // DIL_BLOCK
    return pl.pallas_call(
        functools.partial(_dil_kernel, seq_len=seq_len),
        grid=(batch, N_PAIRS),
        in_specs=[blk, blk, blk, kb_blk, kb_blk, kb_blk],
        out_specs=blk,
        out_shape=jax.ShapeDtypeStruct((batch, seq_len, BRANCH_WIDTH), F32),
        scratch_shapes=[pltpu.VMEM((seq_len, LANES), F32) for _ in range(6)]
        + [pltpu.VMEM((seq_len, LANES), BF16) for _ in range(3)]
        + [pltpu.VMEM((seq_len, LANES), F32) for _ in range(6)]
        + [pltpu.VMEM((2, 2 * DIL_BLOCK, 2 * DIL_BLOCK), F32),
           pltpu.VMEM((2, DIL_GROUP, n_qb, 2 * DIL_BLOCK, 2 * DIL_BLOCK), F32),
           pltpu.VMEM((2, DIL_GROUP, LANES, DIL_TQ + DIL_BLOCK), BF16),
           pltpu.VMEM((2, DIL_GROUP, n_qb, 1, 2 * DIL_BLOCK), F32)],
        compiler_params=pltpu.CompilerParams(
            dimension_semantics=("arbitrary", "arbitrary"), vmem_limit_bytes=VMEM_LIMIT),
        name="dil_attn",
    )(q, k, v, *kbs)


def _merge_kernel(x_ref, oa_ref, ob_ref, oc_ref, g_ref, wz_ref, wg_ref, wb_ref, wo_ref,
                  fg_ref, out_ref, *, last):
    x = x_ref[...]
    ms = jnp.mean(x * x, axis=-1, keepdims=True)
    h = (x * lax.rsqrt(ms + RMS_EPS) * g_ref[...]).astype(BF16)
    merged = None
    for n, o_ref in enumerate((oa_ref, ob_ref, oc_ref)):
        z = jnp.dot(h, wz_ref[:, n * BRANCH_WIDTH:(n + 1) * BRANCH_WIDTH],
                    preferred_element_type=F32)
        y = (o_ref[...] * (z * jax.nn.sigmoid(z))).astype(BF16)
        br = jnp.dot(y, wb_ref[n], preferred_element_type=F32)
        gate = jax.nn.sigmoid(jnp.dot(h, wg_ref[:, n * D_MODEL:(n + 1) * D_MODEL],
                                      preferred_element_type=F32))
        merged = gate * br if merged is None else merged + gate * br
    out = x + jnp.dot(merged.astype(BF16), wo_ref[...], preferred_element_type=F32)
    if last:
        ms = jnp.mean(out * out, axis=-1, keepdims=True)
        out = out * lax.rsqrt(ms + RMS_EPS) * fg_ref[...]
    out_ref[...] = out


def _merge_call(x2d, oa, ob, oc, g, wz, wg, wb, wo, fg, *, last):
    n_tok = x2d.shape[0]
    tm = TM_MERGE
    const2 = lambda i: (0, 0)
    tok = pl.BlockSpec((tm, BRANCH_WIDTH), lambda i: (i, 0))
    return pl.pallas_call(
        functools.partial(_merge_kernel, last=last),
        grid=(n_tok // tm,),
        in_specs=[
            pl.BlockSpec((tm, D_MODEL), lambda i: (i, 0)), tok, tok, tok,
            pl.BlockSpec((1, D_MODEL), const2),
            pl.BlockSpec((D_MODEL, 3 * BRANCH_WIDTH), const2),
            pl.BlockSpec((D_MODEL, 3 * D_MODEL), const2),
            pl.BlockSpec((3, BRANCH_WIDTH, D_MODEL), lambda i: (0, 0, 0)),
            pl.BlockSpec((D_MODEL, D_MODEL), const2),
            pl.BlockSpec((1, D_MODEL), const2),
        ],
        out_specs=pl.BlockSpec((tm, D_MODEL), lambda i: (i, 0)),
        out_shape=jax.ShapeDtypeStruct((n_tok, D_MODEL), F32),
        compiler_params=pltpu.CompilerParams(
            dimension_semantics=("arbitrary",), vmem_limit_bytes=VMEM_LIMIT),
        name="merge",
    )(x2d, oa.reshape(n_tok, BRANCH_WIDTH), ob.reshape(n_tok, BRANCH_WIDTH),
      oc.reshape(n_tok, BRANCH_WIDTH), g, wz, wg, wb, wo, fg)


def _alibi_slopes(n_heads):
    return [2.0 ** (-ALIBI_MAX_EXP * (h + 1) / n_heads) for h in range(n_heads)]


def _alibi_key_bias(seq_len, slopes_per_pair, dilation=1):
    pos = np.arange(seq_len, dtype=np.float64).reshape(-1, dilation).T.reshape(-1)
    out = np.zeros((seq_len, BRANCH_WIDTH), dtype=BF16)
    for p, pair in enumerate(slopes_per_pair):
        for c, slope in enumerate(pair):
            rest = pos * (slope * LOG2E)
            for x in range(3):
                piece = rest.astype(BF16)
                out[:, p * LANES + 3 * c + x] = piece
                rest = rest - piece.astype(np.float64)
    return jnp.asarray(out)[None]


def _fox_bias_scatter():
    m = [[0.0] * BRANCH_WIDTH for _ in range(3 * LANES)]
    for x in range(3):
        for h in range(N_HEADS):
            m[x * LANES + h][(h // 2) * LANES + 3 * (h % 2) + x] = 1.0
    return jnp.asarray(m, dtype=BF16)


def kernel(x, norm_g, w_in, fox_fb, diff_lam, diff_norm_g, w_branch, w_out, final_g):
    batch, seq_len, _ = x.shape
    depth = norm_g.shape[0]
    n_tok = batch * seq_len
    scale = HEAD_DIM ** -0.5 * LOG2E

    w = BRANCH_WIDTH
    off = {}
    pos = 0
    for name, width in (("diff_q", w), ("diff_k", w), ("diff_v", w), ("diff_z", w),
                        ("fox_q", w), ("fox_k", w), ("fox_v", w), ("fox_f", N_HEADS),
                        ("fox_z", w), ("dil_q", w), ("dil_k", w), ("dil_v", w),
                        ("dil_z", w), ("merge_g", 3 * D_MODEL)):
        off[name] = (pos, pos + width)
        pos += width

    def cols(wl, name, mult=1.0):
        a, b = off[name]
        return wl[:, a:b] * mult if mult != 1.0 else wl[:, a:b]

    ds = _alibi_slopes(DIFF_HEADS)
    cs = _alibi_slopes(N_HEADS)
    diff_kb = _alibi_key_bias(seq_len, [(ds[h], ds[h]) for h in range(DIFF_HEADS)])
    dil_pairs = [(cs[2 * p], cs[2 * p + 1]) for p in range(N_PAIRS)]
    dil_kb = [_alibi_key_bias(seq_len, dil_pairs, d) for _, d in DIL_PATTERNS]
    pmat = _fox_bias_scatter()
    ones_g = jnp.ones((1, BRANCH_WIDTH), F32)
    zero_lam = jnp.zeros((4, LANES), F32)

    x2d = x.reshape(n_tok, D_MODEL)
    for l in range(depth):
        wl = w_in[l]
        w_qkv = jnp.concatenate(
            [cols(wl, "diff_q", scale), cols(wl, "diff_k"), cols(wl, "diff_v"),
             cols(wl, "fox_q", scale), cols(wl, "fox_k"), cols(wl, "fox_v"),
             cols(wl, "dil_q", scale), cols(wl, "dil_k"), cols(wl, "dil_v")],
            axis=1).astype(BF16)
        w_f = jnp.pad(cols(wl, "fox_f"), ((0, 0), (0, LANES - N_HEADS))).astype(BF16)
        fb = jnp.pad(fox_fb[l].astype(F32), (0, LANES - N_HEADS))[None]
        g = norm_g[l].astype(F32)[None]

        (dqT, dk, dvT, fqT, fk, fvT, fkb, cq, ck, cv) = _proj_call(
            x2d, g, w_qkv, w_f, fb, pmat, seq_len=seq_len)

        lam_init = 0.8 - 0.6 * math.exp(-0.3 * l)
        lam_pad = jnp.pad(diff_lam[l].astype(F32), ((0, 0), (0, LANES - HEAD_DIM)))
        oa = _pair_attn_call(dqT, dk, diff_kb, dvT, lam_pad,
                             diff_norm_g[l].astype(F32)[None],
                             batch=batch, seq_len=seq_len, diff=True, lam_init=lam_init)
        ob = _pair_attn_call(fqT, fk, fkb.reshape(batch, seq_len, BRANCH_WIDTH), fvT,
                             zero_lam, ones_g,
                             batch=batch, seq_len=seq_len, diff=False, lam_init=0.0)
        oc = _dil_call(cq, ck, cv, dil_kb, batch=batch, seq_len=seq_len)

        wz = jnp.concatenate([cols(wl, "diff_z"), cols(wl, "fox_z"), cols(wl, "dil_z")],
                             axis=1).astype(BF16)
        wg = cols(wl, "merge_g").astype(BF16)
        x2d = _merge_call(x2d, oa, ob, oc, g, wz, wg, w_branch[l].astype(BF16),
                          w_out[l].astype(BF16), final_g.astype(F32)[None],
                          last=(l == depth - 1))
    return x2d.reshape(batch, seq_len, D_MODEL)
```

```python
import functools
import math

import jax
import jax.numpy as jnp
import numpy as np
from jax import lax
from jax.experimental import pallas as pl
from jax.experimental.pallas import tpu as pltpu

F32 = jnp.float32
BF16 = jnp.bfloat16

D_MODEL = 1024
HEAD_DIM = 64
BRANCH_WIDTH = 512
LANES = 128
N_PAIRS = BRANCH_WIDTH // LANES
DIFF_HEADS = 4
N_HEADS = 8
DIL_PATTERNS = ((128, 1), (512, 4), (2048, 16))
DIL_BLOCK = 128
DIL_TQ = 256
DIL_GROUP = 2
RMS_EPS = 1e-6
LOG2E = math.log2(math.e)
ALIBI_MAX_EXP = 8.0
N_QKV_GROUPS = 9
TQ = 512
TK = 512
KV_SUBS = 2
TM_PROJ = 512
TM_MERGE = 512
VMEM_LIMIT = 56 * 1024 * 1024


def _split3(a):
    a1 = a.astype(BF16)
    r1 = a - a1.astype(F32)
    a2 = r1.astype(BF16)
    a3 = (r1 - a2.astype(F32)).astype(BF16)
    return a1, a2, a3


def _proj_kernel(x_ref, g_ref, w_ref, wf_ref, fb_ref, pmat_ref,
                 dqT_ref, dk_ref, dvT_ref, fqT_ref, fk_ref, fvT_ref, fkb_ref,
                 cq_ref, ck_ref, cv_ref, carry_ref, *, tiles_per_seq):
    i = pl.program_id(0)
    tm = x_ref.shape[0]
    x = x_ref[...]
    ms = jnp.mean(x * x, axis=-1, keepdims=True)
    h = (x * lax.rsqrt(ms + RMS_EPS) * g_ref[...]).astype(BF16)

    def proj(g):
        return jnp.dot(h, w_ref[:, g * BRANCH_WIDTH:(g + 1) * BRANCH_WIDTH],
                       preferred_element_type=F32)

    def put_t(ref, val):
        for t in range(tm // TQ):
            ref[t] = val[t * TQ:(t + 1) * TQ, :].T.astype(BF16)

    put_t(dqT_ref, proj(0))
    dk_ref[...] = proj(1).astype(BF16)
    put_t(dvT_ref, proj(2))
    put_t(fqT_ref, proj(3))
    fk_ref[...] = proj(4).astype(BF16)
    put_t(fvT_ref, proj(5))
    cq_ref[...] = proj(6).astype(BF16)
    ck_ref[...] = proj(7).astype(BF16)
    cv_ref[...] = proj(8).astype(BF16)

    f = jnp.dot(h, wf_ref[...], preferred_element_type=F32) + fb_ref[...]
    logf = jnp.minimum(f, 0.0) - jnp.log1p(jnp.exp(-jnp.abs(f)))

    @pl.when(i % tiles_per_seq == 0)
    def _():
        carry_ref[...] = jnp.zeros_like(carry_ref)

    row = lax.broadcasted_iota(jnp.int32, (tm, tm), 0)
    col = lax.broadcasted_iota(jnp.int32, (tm, tm), 1)
    tri = jnp.where(row >= col, 1.0, 0.0).astype(BF16)
    a1, a2, a3 = _split3(logf)
    c = (jnp.dot(tri, a1, preferred_element_type=F32)
         + jnp.dot(tri, a2, preferred_element_type=F32)
         + jnp.dot(tri, a3, preferred_element_type=F32)) + carry_ref[0:1, :]
    carry_ref[0:1, :] = c[tm - 1:tm, :]
    b1, b2, b3 = _split3(c * (-LOG2E))
    kb = jnp.dot(jnp.concatenate([b1, b2, b3], axis=1), pmat_ref[...],
                 preferred_element_type=F32)
    fkb_ref[...] = kb.astype(BF16)


def _proj_call(x2d, g, w_qkv, w_f, fb, pmat, *, seq_len):
    n_tok = x2d.shape[0]
    tm = TM_PROJ
    nt = n_tok // tm
    const = lambda i: (0, 0)
    tok = pl.BlockSpec((tm, BRANCH_WIDTH), lambda i: (i, 0))
    tr = pl.BlockSpec((tm // TQ, BRANCH_WIDTH, TQ), lambda i: (i, 0, 0))
    nat_shape = jax.ShapeDtypeStruct((n_tok, BRANCH_WIDTH), BF16)
    tr_shape = jax.ShapeDtypeStruct((n_tok // TQ, BRANCH_WIDTH, TQ), BF16)
    return pl.pallas_call(
        functools.partial(_proj_kernel, tiles_per_seq=seq_len // tm),
        grid=(nt,),
        in_specs=[
            pl.BlockSpec((tm, D_MODEL), lambda i: (i, 0)),
            pl.BlockSpec((1, D_MODEL), const),
            pl.BlockSpec((D_MODEL, N_QKV_GROUPS * BRANCH_WIDTH), const),
            pl.BlockSpec((D_MODEL, LANES), const),
            pl.BlockSpec((1, LANES), const),
            pl.BlockSpec((3 * LANES, BRANCH_WIDTH), const),
        ],
        out_specs=[tr, tok, tr, tr, tok, tr, tok, tok, tok, tok],
        out_shape=[tr_shape, nat_shape, tr_shape, tr_shape, nat_shape, tr_shape,
                   nat_shape, nat_shape, nat_shape, nat_shape],
        scratch_shapes=[pltpu.VMEM((8, LANES), F32)],
        compiler_params=pltpu.CompilerParams(
            dimension_semantics=("arbitrary",), vmem_limit_bytes=VMEM_LIMIT),
        name="proj",
    )(x2d, g, w_qkv, w_f, fb, pmat)


def _pair_attn_kernel(qT_ref, qTn_ref, k_ref, kb_ref, vT_ref, lam_ref, g_ref, o_ref,
                      qa_sc, m_sc, l_sc, acc_sc, s_sc, t0_sc, smax_sc, t0max_sc,
                      *, diff, lam_init, nq):
    i = pl.program_id(2)
    par = i % 2
    row = lax.broadcasted_iota(jnp.int32, (LANES, TQ), 0)

    def build_qa(q_ref, slot):
        qT = q_ref[0, 0].astype(F32)
        qa_sc[slot, :, 0:TQ] = jnp.concatenate(
            [jnp.where(row < HEAD_DIM, qT, 0.0), jnp.where(row < 3, 1.0, 0.0)],
            axis=0).astype(BF16)
        qa_sc[slot, :, TQ:2 * TQ] = jnp.concatenate(
            [jnp.where(row >= HEAD_DIM, qT, 0.0),
             jnp.where((row >= 3) & (row < 6), 1.0, 0.0)], axis=0).astype(BF16)

    tile_keys = KV_SUBS * TK

    def score_rows(row0, n_rows, qa_slot, dst, dst_slot):
        ka = jnp.concatenate([k_ref[0, row0:row0 + n_rows, :],
                              kb_ref[0, row0:row0 + n_rows, :]], axis=1)
        s = jnp.dot(ka, qa_sc[qa_slot], preferred_element_type=F32)
        dst[dst_slot, row0:row0 + n_rows, :] = s
        return jnp.max(s, axis=0, keepdims=True)

    @pl.when(i == 0)
    def _():
        build_qa(qT_ref, 0)
        t0max_sc[0] = score_rows(0, TK, 0, t0_sc, 0)

    build_qa(qTn_ref, 1 - par)
    m_sc[...] = jnp.full(m_sc.shape, -jnp.inf, F32)
    l_sc[...] = jnp.zeros(l_sc.shape, F32)
    acc_sc[...] = jnp.zeros(acc_sc.shape, F32)

    def consume(s, sub0, n_sub, tile_max):
        rows = n_sub * TK
        if tile_max is None:
            kr = lax.broadcasted_iota(jnp.int32, (TK, 2 * TQ), 0)
            qc = lax.broadcasted_iota(jnp.int32, (TK, 2 * TQ), 1) % TQ
            tail = jnp.where(kr <= qc, s[rows - TK:], -jnp.inf)
            s = tail if n_sub == 1 else jnp.concatenate([s[:rows - TK], tail], axis=0)
            tile_max = jnp.max(s, axis=0, keepdims=True)
        m_old = m_sc[...]
        m_new = jnp.maximum(m_old, tile_max)
        alpha = jnp.exp2(m_old - m_new)
        p = jnp.exp2(s - m_new)
        l_sc[...] = alpha * l_sc[...] + jnp.sum(p, axis=0, keepdims=True)
        pb = p.astype(BF16)
        pv = jnp.dot(vT_ref[0, sub0], pb[0:TK], preferred_element_type=F32)
        for t in range(1, n_sub):
            pv = pv + jnp.dot(vT_ref[0, sub0 + t], pb[t * TK:(t + 1) * TK],
                              preferred_element_type=F32)
        acc_sc[...] = alpha * acc_sc[...] + pv
        m_sc[...] = m_new

    def full_block(cur, cur_max, cur_slot, sub0, t_next, n_host, par):
        m_old = m_sc[...]
        m_new = jnp.maximum(m_old, cur_max[cur_slot])
        alpha = jnp.exp2(m_old - m_new)
        nslot = t_next % 2
        lsum = pv = nmax = None
        for t in range(KV_SUBS):
            if t < n_host:
                kbase = pl.multiple_of(t_next * tile_keys + t * TK, TK)
                ka = jnp.concatenate([k_ref[0, pl.ds(kbase, TK), :],
                                      kb_ref[0, pl.ds(kbase, TK), :]], axis=1)
                sn = jnp.dot(ka, qa_sc[par], preferred_element_type=F32)
                s_sc[nslot, t * TK:(t + 1) * TK, :] = sn
                cm = jnp.max(sn, axis=0, keepdims=True)
                nmax = cm if nmax is None else jnp.maximum(nmax, cm)
            p = jnp.exp2(cur[cur_slot, t * TK:(t + 1) * TK, :] - m_new)
            ls = jnp.sum(p, axis=0, keepdims=True)
            lsum = ls if lsum is None else lsum + ls
            d = jnp.dot(vT_ref[0, sub0 + t], p.astype(BF16), preferred_element_type=F32)
            pv = d if pv is None else pv + d
        if n_host == KV_SUBS:
            smax_sc[nslot] = nmax
        l_sc[...] = alpha * l_sc[...] + lsum
        acc_sc[...] = alpha * acc_sc[...] + pv
        m_sc[...] = m_new

    def write_output():
        a = acc_sc[...] / l_sc[...]
        a0 = a[:, 0:TQ]
        a1 = a[:, TQ:2 * TQ]
        if diff:
            dl = lam_ref[...]
            lam = (jnp.exp(jnp.sum(dl[0:1] * dl[1:2], axis=1, keepdims=True))
                   - jnp.exp(jnp.sum(dl[2:3] * dl[3:4], axis=1, keepdims=True)) + lam_init)
            oT = a0 - lam * a1
            ms = jnp.mean(oT * oT, axis=0, keepdims=True)
            oT = oT * lax.rsqrt(ms + RMS_EPS)
            o_ref[0] = (oT.T * g_ref[...]) * (1.0 - lam_init)
        else:
            o_ref[0] = jnp.where(row < HEAD_DIM, a0, a1).T

    def diag_block(r, s, sub0, par):
        nslot = (r + 1) % 2
        half = tile_keys // 2
        max_a = score_rows(0, half, 1 - par, t0_sc, nslot)
        consume(s, sub0, r + 1, None)
        max_b = score_rows(half, half, 1 - par, t0_sc, nslot)
        t0max_sc[nslot] = jnp.maximum(max_a, max_b)
        write_output()

    def run_full(j, n_host, par):
        if j == 0:
            full_block(t0_sc, t0max_sc, par, 0, j + 1, n_host, par)
        else:
            full_block(s_sc, smax_sc, j % 2, j * KV_SUBS, j + 1, n_host, par)

    for i_s in range(nq):
        @pl.when(i == i_s)
        def _(i_s=i_s):
            full, r = divmod(i_s, KV_SUBS)
            p_s = i_s % 2
            for j in range(full):
                run_full(j, r + 1 if j + 1 == full else KV_SUBS, p_s)
            rows = (r + 1) * TK
            s = t0_sc[p_s, 0:rows, :] if full == 0 else s_sc[full % 2, 0:rows, :]
            diag_block(r, s, full * KV_SUBS, p_s)


def _pair_attn_call(qT, k, kb, vT, lam_pad, g, *, batch, seq_len, diff, lam_init):
    nq = seq_len // TQ
    nk = seq_len // TK
    qT = qT.reshape(batch, nq, BRANCH_WIDTH, TQ)
    vT = vT.reshape(batch, nk, BRANCH_WIDTH, TK)
    k = k.reshape(batch, seq_len, BRANCH_WIDTH)
    kb_batched = kb.shape[0] != 1
    kb_map = (lambda b, p, i: (b, 0, p)) if kb_batched else (lambda b, p, i: (0, 0, p))
    return pl.pallas_call(
        functools.partial(_pair_attn_kernel, diff=diff, lam_init=lam_init, nq=nq),
        grid=(batch, N_PAIRS, nq),
        in_specs=[
            pl.BlockSpec((1, 1, LANES, TQ), lambda b, p, i: (b, i, p, 0)),
            pl.BlockSpec((1, 1, LANES, TQ),
                         lambda b, p, i: (b, jnp.minimum(i + 1, nq - 1), p, 0)),
            pl.BlockSpec((1, seq_len, LANES), lambda b, p, i: (b, 0, p)),
            pl.BlockSpec((1, seq_len, LANES), kb_map),
            pl.BlockSpec((1, nk, LANES, TK), lambda b, p, i: (b, 0, p, 0)),
            pl.BlockSpec((4, LANES), lambda b, p, i: (0, 0)),
            pl.BlockSpec((1, LANES), lambda b, p, i: (0, p)),
        ],
        out_specs=pl.BlockSpec((1, TQ, LANES), lambda b, p, i: (b, i, p)),
        out_shape=jax.ShapeDtypeStruct((batch, seq_len, BRANCH_WIDTH), F32),
        scratch_shapes=[
            pltpu.VMEM((2, 2 * LANES, 2 * TQ), BF16),
            pltpu.VMEM((1, 2 * TQ), F32),
            pltpu.VMEM((1, 2 * TQ), F32),
            pltpu.VMEM((LANES, 2 * TQ), F32),
            pltpu.VMEM((2, KV_SUBS * TK, 2 * TQ), F32),
            pltpu.VMEM((2, KV_SUBS * TK, 2 * TQ), F32),
            pltpu.VMEM((2, 1, 2 * TQ), F32),
            pltpu.VMEM((2, 1, 2 * TQ), F32),
        ],
        compiler_params=pltpu.CompilerParams(
            dimension_semantics=("arbitrary", "arbitrary", "arbitrary"),
            vmem_limit_bytes=VMEM_LIMIT),
        name="diff_attn" if diff else "fox_attn",
    )(qT, qT, k, kb, vT, lam_pad, g)


def _dil_kernel(q_ref, k_ref, v_ref, kb1_ref, kb4_ref, kb16_ref, o_ref,
                qf, kf, vf, q4f, k4f, v4f, q_sub, k_sub, v_sub, acc, m_sc, l_sc, acc4, m4, l4,
                mask_sc, s_buf, vT_buf, smax_buf, *, seq_len):
    L = DIL_BLOCK
    T = DIL_TQ
    qf[...] = q_ref[0].astype(F32)
    kf[...] = k_ref[0].astype(F32)
    vf[...] = v_ref[0].astype(F32)

    kr = lax.broadcasted_iota(jnp.int32, (2 * L, 2 * L), 0)
    qc = lax.broadcasted_iota(jnp.int32, (2 * L, 2 * L), 1) % L
    band = (kr >= qc) & (kr <= qc + L)
    mask_sc[0] = jnp.where(band & (kr >= L), 0.0, -jnp.inf)
    mask_sc[1] = jnp.where(band, 0.0, -jnp.inf)

    lane = lax.broadcasted_iota(jnp.int32, (T, LANES), 1)
    first = lane < HEAD_DIM
    sel0 = jnp.where(lane < 3, 1.0, 0.0).astype(BF16)
    sel1 = jnp.where((lane >= 3) & (lane < 6), 1.0, 0.0).astype(BF16)
    head0 = jnp.where(first, 1.0, 0.0).astype(BF16)
    head1 = jnp.where(first, 0.0, 1.0).astype(BF16)
    frow = lax.broadcasted_iota(jnp.int32, (LANES, L), 0) < HEAD_DIM
    n_qb = T // L

    assert tuple(d for _, d in DIL_PATTERNS) == (1, 4, 16)
    n4 = seq_len // 4
    n16 = seq_len // 16
    kb_refs = (kb1_ref, kb4_ref, kb16_ref)
    for pattern, (_, d) in enumerate(DIL_PATTERNS):
        nt = seq_len // (T * d)
        n = seq_len // d
        kb_ref = kb_refs[pattern]
        if d == 1:
            q_rows = lambda rows: q_ref[0, rows, :]
            k_rows = lambda rows: k_ref[0, rows, :]
            v_rows = lambda rows: v_ref[0, rows, :]
            m_st, l_st, a_st = m_sc, l_sc, acc
        else:
            if d == 4:
                for r in range(4):
                    dst = slice(r * n4, (r + 1) * n4)
                    for src_f, dst_f, dst_b in ((qf, q4f, q_sub), (kf, k4f, k_sub), (vf, v4f, v_sub)):
                        val = src_f[pl.ds(r, n4, stride=4), :]
                        dst_f[dst, :] = val
                        dst_b[dst, :] = val.astype(BF16)
                    for src_s, dst_s in ((m_sc, m4), (l_sc, l4), (acc, acc4)):
                        dst_s[dst, :] = src_s[pl.ds(r, n4, stride=4), :]
            else:
                for r in range(16):
                    src0 = (r % 4) * n4 + r // 4
                    dst = slice(r * n16, (r + 1) * n16)
                    for src_f, dst_b in ((q4f, q_sub), (k4f, k_sub), (v4f, v_sub)):
                        dst_b[dst, :] = src_f[pl.ds(src0, n16, stride=4), :].astype(BF16)
            q_rows = lambda rows: q_sub[rows, :]
            k_rows = lambda rows: k_sub[rows, :]
            v_rows = lambda rows: v_sub[rows, :]
            m_st, l_st, a_st = m4, l4, acc4

        n_groups = seq_len // (T * DIL_GROUP)

        def addr(u, g, d=d, nt=nt, n=n):
            t = u * DIL_GROUP + g
            r = t // nt
            it = t % nt
            start = pl.multiple_of(r * n + T * it, L)
            prev = pl.multiple_of(jnp.where(it == 0, start, start - L), L)
            if d == 16:
                state_rows = pl.ds((r % 4) * n4 + 4 * T * it + r // 4, T, stride=4)
            else:
                state_rows = pl.ds(start, T)
            return pl.ds(start, T), pl.ds(prev, L), jnp.minimum(it, 1), state_rows

        def score_group(u, buf, q_rows=q_rows, k_rows=k_rows, v_rows=v_rows, kb_ref=kb_ref):
            for g in range(DIL_GROUP):
                rows, prows, slot, _ = addr(u, g)
                qb = q_rows(rows)
                q0 = jnp.concatenate([qb * head0, sel0], axis=1)
                q1 = jnp.concatenate([qb * head1, sel1], axis=1)
                ka = jnp.concatenate(
                    [jnp.concatenate([k_rows(prows), kb_ref[0, prows, :]], axis=1),
                     jnp.concatenate([k_rows(rows), kb_ref[0, rows, :]], axis=1)],
                    axis=0)
                for b in range(n_qb):
                    qa = jnp.concatenate([q0[b * L:(b + 1) * L], q1[b * L:(b + 1) * L]], axis=0)
                    sb = lax.dot_general(ka[b * L:(b + 2) * L], qa, (((1,), (1,)), ((), ())),
                                         preferred_element_type=F32)
                    sb = sb + mask_sc[slot if b == 0 else 1]
                    s_buf[buf, g, b] = sb
                    smax_buf[buf, g, b] = jnp.max(sb, axis=0, keepdims=True)
                vv = jnp.concatenate([v_rows(prows), v_rows(rows)], axis=0)
                vT_buf[buf, g] = vv.astype(F32).T.astype(BF16)

        def consume_group(u, buf, first_pattern=(pattern == 0), m_st=m_st, l_st=l_st, a_st=a_st):
            num, m_blk, l_blk = [], [], []
            for g in range(DIL_GROUP):
                nb_, mb_, lb_ = [], [], []
                for b in range(n_qb):
                    mg = smax_buf[buf, g, b]
                    p = jnp.exp2(s_buf[buf, g, b] - mg)
                    lg = jnp.sum(p, axis=0, keepdims=True)
                    oT = jnp.dot(vT_buf[buf, g, :, b * L:(b + 2) * L], p.astype(BF16),
                                 preferred_element_type=F32)
                    nb_.append(jnp.where(frow, oT[:, :L], oT[:, L:]))
                    mb_.append(jnp.where(frow, jnp.broadcast_to(mg[:, :L], (LANES, L)),
                                         jnp.broadcast_to(mg[:, L:], (LANES, L))))
                    lb_.append(jnp.where(frow, jnp.broadcast_to(lg[:, :L], (LANES, L)),
                                         jnp.broadcast_to(lg[:, L:], (LANES, L))))
                num.append(jnp.concatenate(nb_, axis=1).T)
                m_blk.append(jnp.concatenate(mb_, axis=1).T)
                l_blk.append(jnp.concatenate(lb_, axis=1).T)
            for g in range(DIL_GROUP):
                _, _, _, rows = addr(u, g)
                if first_pattern:
                    l_st[rows, :] = l_blk[g]
                    a_st[rows, :] = num[g]
                    m_st[rows, :] = m_blk[g]
                    continue
                m_old = m_st[rows, :]
                m_new = jnp.maximum(m_old, m_blk[g])
                w_old = jnp.exp2(m_old - m_new)
                w_blk = jnp.exp2(m_blk[g] - m_new)
                l_st[rows, :] = w_old * l_st[rows, :] + w_blk * l_blk[g]
                a_st[rows, :] = w_old * a_st[rows, :] + w_blk * num[g]
                m_st[rows, :] = m_new

        score_group(0, 0)

        def trip(v, carry):
            score_group(2 * v + 1, 1)
            consume_group(2 * v, 0)
            score_group(jnp.minimum(2 * v + 2, n_groups - 1), 0)
            consume_group(2 * v + 1, 1)
            return carry

        lax.fori_loop(0, n_groups // 2, trip, 0)

    for r in range(4):
        rows4 = slice(r * n4, (r + 1) * n4)
        o_ref[0, pl.ds(r, n4, stride=4), :] = acc4[rows4, :] / l4[rows4, :]


def _dil_call(q, k, v, kbs, *, batch, seq_len):
    assert seq_len % (DIL_TQ * max(d for _, d in DIL_PATTERNS)) == 0
    assert seq_len % (DIL_TQ * DIL_GROUP * 2) == 0
    q = q.reshape(batch, seq_len, BRANCH_WIDTH)
    k = k.reshape(batch, seq_len, BRANCH_WIDTH)
    v = v.reshape(batch, seq_len, BRANCH_WIDTH)
    blk = pl.BlockSpec((1, seq_len, LANES), lambda b, p: (b, 0, p))
    kb_blk = pl.BlockSpec((1, seq_len, LANES), lambda b, p: (0, 0, p))
    n_qb = DIL_TQ // DIL_BLOCK
    return pl.pallas_call(
        functools.partial(_dil_kernel, seq_len=seq_len),
        grid=(batch, N_PAIRS),
        in_specs=[blk, blk, blk, kb_blk, kb_blk, kb_blk],
        out_specs=blk,
        out_shape=jax.ShapeDtypeStruct((batch, seq_len, BRANCH_WIDTH), F32),
        scratch_shapes=[pltpu.VMEM((seq_len, LANES), F32) for _ in range(6)]
        + [pltpu.VMEM((seq_len, LANES), BF16) for _ in range(3)]
        + [pltpu.VMEM((seq_len, LANES), F32) for _ in range(6)]
        + [pltpu.VMEM((2, 2 * DIL_BLOCK, 2 * DIL_BLOCK), F32),
           pltpu.VMEM((2, DIL_GROUP, n_qb, 2 * DIL_BLOCK, 2 * DIL_BLOCK), F32),
           pltpu.VMEM((2, DIL_GROUP, LANES, DIL_TQ + DIL_BLOCK), BF16),
           pltpu.VMEM((2, DIL_GROUP, n_qb, 1, 2 * DIL_BLOCK), F32)],
        compiler_params=pltpu.CompilerParams(
            dimension_semantics=("arbitrary", "arbitrary"), vmem_limit_bytes=VMEM_LIMIT),
        name="dil_attn",
    )(q, k, v, *kbs)


def _merge_kernel(x_ref, oa_ref, ob_ref, oc_ref, g_ref, wz_ref, wg_ref, wb_ref, wo_ref,
                  fg_ref, out_ref, *, last):
    x = x_ref[...]
    ms = jnp.mean(x * x, axis=-1, keepdims=True)
    h = (x * lax.rsqrt(ms + RMS_EPS) * g_ref[...]).astype(BF16)
    merged = None
    for n, o_ref in enumerate((oa_ref, ob_ref, oc_ref)):
        z = jnp.dot(h, wz_ref[:, n * BRANCH_WIDTH:(n + 1) * BRANCH_WIDTH],
                    preferred_element_type=F32)
        y = (o_ref[...] * (z * jax.nn.sigmoid(z))).astype(BF16)
        br = jnp.dot(y, wb_ref[n], preferred_element_type=F32)
        gate = jax.nn.sigmoid(jnp.dot(h, wg_ref[:, n * D_MODEL:(n + 1) * D_MODEL],
                                      preferred_element_type=F32))
        merged = gate * br if merged is None else merged + gate * br
    out = x + jnp.dot(merged.astype(BF16), wo_ref[...], preferred_element_type=F32)
    if last:
        ms = jnp.mean(out * out, axis=-1, keepdims=True)
        out = out * lax.rsqrt(ms + RMS_EPS) * fg_ref[...]
    out_ref[...] = out


def _merge_call(x2d, oa, ob, oc, g, wz, wg, wb, wo, fg, *, last):
    n_tok = x2d.shape[0]
    tm = TM_MERGE
    const2 = lambda i: (0, 0)
    tok = pl.BlockSpec((tm, BRANCH_WIDTH), lambda i: (i, 0))
    return pl.pallas_call(
        functools.partial(_merge_kernel, last=last),
        grid=(n_tok // tm,),
        in_specs=[
            pl.BlockSpec((tm, D_MODEL), lambda i: (i, 0)), tok, tok, tok,
            pl.BlockSpec((1, D_MODEL), const2),
            pl.BlockSpec((D_MODEL, 3 * BRANCH_WIDTH), const2),
            pl.BlockSpec((D_MODEL, 3 * D_MODEL), const2),
            pl.BlockSpec((3, BRANCH_WIDTH, D_MODEL), lambda i: (0, 0, 0)),
            pl.BlockSpec((D_MODEL, D_MODEL), const2),
            pl.BlockSpec((1, D_MODEL), const2),
        ],
        out_specs=pl.BlockSpec((tm, D_MODEL), lambda i: (i, 0)),
        out_shape=jax.ShapeDtypeStruct((n_tok, D_MODEL), F32),
        compiler_params=pltpu.CompilerParams(
            dimension_semantics=("arbitrary",), vmem_limit_bytes=VMEM_LIMIT),
        name="merge",
    )(x2d, oa.reshape(n_tok, BRANCH_WIDTH), ob.reshape(n_tok, BRANCH_WIDTH),
      oc.reshape(n_tok, BRANCH_WIDTH), g, wz, wg, wb, wo, fg)


def _alibi_slopes(n_heads):
    return [2.0 ** (-ALIBI_MAX_EXP * (h + 1) / n_heads) for h in range(n_heads)]


def _alibi_key_bias(seq_len, slopes_per_pair, dilation=1):
    pos = np.arange(seq_len, dtype=np.float64).reshape(-1, dilation).T.reshape(-1)
    out = np.zeros((seq_len, BRANCH_WIDTH), dtype=BF16)
    for p, pair in enumerate(slopes_per_pair):
        for c, slope in enumerate(pair):
            rest = pos * (slope * LOG2E)
            for x in range(3):
                piece = rest.astype(BF16)
                out[:, p * LANES + 3 * c + x] = piece
                rest = rest - piece.astype(np.float64)
    return jnp.asarray(out)[None]


def _fox_bias_scatter():
    m = [[0.0] * BRANCH_WIDTH for _ in range(3 * LANES)]
    for x in range(3):
        for h in range(N_HEADS):
            m[x * LANES + h][(h // 2) * LANES + 3 * (h % 2) + x] = 1.0
    return jnp.asarray(m, dtype=BF16)


def kernel(x, norm_g, w_in, fox_fb, diff_lam, diff_norm_g, w_branch, w_out, final_g):
    batch, seq_len, _ = x.shape
    depth = norm_g.shape[0]
    n_tok = batch * seq_len
    scale = HEAD_DIM ** -0.5 * LOG2E

    w = BRANCH_WIDTH
    off = {}
    pos = 0
    for name, width in (("diff_q", w), ("diff_k", w), ("diff_v", w), ("diff_z", w),
                        ("fox_q", w), ("fox_k", w), ("fox_v", w), ("fox_f", N_HEADS),
                        ("fox_z", w), ("dil_q", w), ("dil_k", w), ("dil_v", w),
                        ("dil_z", w), ("merge_g", 3 * D_MODEL)):
        off[name] = (pos, pos + width)
        pos += width

    def cols(wl, name, mult=1.0):
        a, b = off[name]
        return wl[:, a:b] * mult if mult != 1.0 else wl[:, a:b]

    ds = _alibi_slopes(DIFF_HEADS)
    cs = _alibi_slopes(N_HEADS)
    diff_kb = _alibi_key_bias(seq_len, [(ds[h], ds[h]) for h in range(DIFF_HEADS)])
    dil_pairs = [(cs[2 * p], cs[2 * p + 1]) for p in range(N_PAIRS)]
    dil_kb = [_alibi_key_bias(seq_len, dil_pairs, d) for _, d in DIL_PATTERNS]
    pmat = _fox_bias_scatter()
    ones_g = jnp.ones((1, BRANCH_WIDTH), F32)
    zero_lam = jnp.zeros((4, LANES), F32)

    x2d = x.reshape(n_tok, D_MODEL)
    for l in range(depth):
        wl = w_in[l]
        w_qkv = jnp.concatenate(
            [cols(wl, "diff_q", scale), cols(wl, "diff_k"), cols(wl, "diff_v"),
             cols(wl, "fox_q", scale), cols(wl, "fox_k"), cols(wl, "fox_v"),
             cols(wl, "dil_q", scale), cols(wl, "dil_k"), cols(wl, "dil_v")],
            axis=1).astype(BF16)
        w_f = jnp.pad(cols(wl, "fox_f"), ((0, 0), (0, LANES - N_HEADS))).astype(BF16)
        fb = jnp.pad(fox_fb[l].astype(F32), (0, LANES - N_HEADS))[None]
        g = norm_g[l].astype(F32)[None]

        (dqT, dk, dvT, fqT, fk, fvT, fkb, cq, ck, cv) = _proj_call(
            x2d, g, w_qkv, w_f, fb, pmat, seq_len=seq_len)

        lam_init = 0.8 - 0.6 * math.exp(-0.3 * l)
        lam_pad = jnp.pad(diff_lam[l].astype(F32), ((0, 0), (0, LANES - HEAD_DIM)))
        oa = _pair_attn_call(dqT, dk, diff_kb, dvT, lam_pad,
                             diff_norm_g[l].astype(F32)[None],
                             batch=batch, seq_len=seq_len, diff=True, lam_init=lam_init)
        ob = _pair_attn_call(fqT, fk, fkb.reshape(batch, seq_len, BRANCH_WIDTH), fvT,
                             zero_lam, ones_g,
                             batch=batch, seq_len=seq_len, diff=False, lam_init=0.0)
        oc = _dil_call(cq, ck, cv, dil_kb, batch=batch, seq_len=seq_len)

        wz = jnp.concatenate([cols(wl, "diff_z"), cols(wl, "fox_z"), cols(wl, "dil_z")],
                             axis=1).astype(BF16)
        wg = cols(wl, "merge_g").astype(BF16)
        x2d = _merge_call(x2d, oa, ob, oc, g, wz, wg, w_branch[l].astype(BF16),
                          w_out[l].astype(BF16), final_g.astype(F32)[None],
                          last=(l == depth - 1))
    return x2d.reshape(batch, seq_len, D_MODEL)
```

```python
import functools
import math

import jax
import jax.numpy as jnp
import numpy as np
from jax import lax
from jax.experimental import pallas as pl
from jax.experimental.pallas import tpu as pltpu

F32 = jnp.float32
BF16 = jnp.bfloat16

D_MODEL = 1024
HEAD_DIM = 64
BRANCH_WIDTH = 512
LANES = 128
N_PAIRS = BRANCH_WIDTH // LANES
DIFF_HEADS = 4
N_HEADS = 8
DIL_PATTERNS = ((128, 1), (512, 4), (2048, 16))
DIL_BLOCK = 128
DIL_TQ = 256
DIL_GROUP = 2
DIL_TRIP = 4
RMS_EPS = 1e-6
LOG2E = math.log2(math.e)
ALIBI_MAX_EXP = 8.0
N_QKV_GROUPS = 9
TQ = 512
TK = 512
KV_SUBS = 2
TM_PROJ = 512
TM_MERGE = 512
VMEM_LIMIT = 56 * 1024 * 1024


def _split3(a):
    a1 = a.astype(BF16)
    r1 = a - a1.astype(F32)
    a2 = r1.astype(BF16)
    a3 = (r1 - a2.astype(F32)).astype(BF16)
    return a1, a2, a3


def _proj_kernel(x_ref, g_ref, w_ref, wf_ref, fb_ref, pmat_ref,
                 dqT_ref, dk_ref, dvT_ref, fqT_ref, fk_ref, fvT_ref, fkb_ref,
                 cq_ref, ck_ref, cv_ref, carry_ref, *, tiles_per_seq):
    i = pl.program_id(0)
    tm = x_ref.shape[0]
    x = x_ref[...]
    ms = jnp.mean(x * x, axis=-1, keepdims=True)
    h = (x * lax.rsqrt(ms + RMS_EPS) * g_ref[...]).astype(BF16)

    def proj(g):
        return jnp.dot(h, w_ref[:, g * BRANCH_WIDTH:(g + 1) * BRANCH_WIDTH],
                       preferred_element_type=F32)

    def put_t(ref, val):
        for t in range(tm // TQ):
            ref[t] = val[t * TQ:(t + 1) * TQ, :].T.astype(BF16)

    put_t(dqT_ref, proj(0))
    dk_ref[...] = proj(1).astype(BF16)
    put_t(dvT_ref, proj(2))
    put_t(fqT_ref, proj(3))
    fk_ref[...] = proj(4).astype(BF16)
    put_t(fvT_ref, proj(5))
    cq_ref[...] = proj(6).astype(BF16)
    ck_ref[...] = proj(7).astype(BF16)
    cv_ref[...] = proj(8).astype(BF16)

    f = jnp.dot(h, wf_ref[...], preferred_element_type=F32) + fb_ref[...]
    logf = jnp.minimum(f, 0.0) - jnp.log1p(jnp.exp(-jnp.abs(f)))

    @pl.when(i % tiles_per_seq == 0)
    def _():
        carry_ref[...] = jnp.zeros_like(carry_ref)

    row = lax.broadcasted_iota(jnp.int32, (tm, tm), 0)
    col = lax.broadcasted_iota(jnp.int32, (tm, tm), 1)
    tri = jnp.where(row >= col, 1.0, 0.0).astype(BF16)
    a1, a2, a3 = _split3(logf)
    c = (jnp.dot(tri, a1, preferred_element_type=F32)
         + jnp.dot(tri, a2, preferred_element_type=F32)
         + jnp.dot(tri, a3, preferred_element_type=F32)) + carry_ref[0:1, :]
    carry_ref[0:1, :] = c[tm - 1:tm, :]
    b1, b2, b3 = _split3(c * (-LOG2E))
    kb = jnp.dot(jnp.concatenate([b1, b2, b3], axis=1), pmat_ref[...],
                 preferred_element_type=F32)
    fkb_ref[...] = kb.astype(BF16)


def _proj_call(x2d, g, w_qkv, w_f, fb, pmat, *, seq_len):
    n_tok = x2d.shape[0]
    tm = TM_PROJ
    nt = n_tok // tm
    const = lambda i: (0, 0)
    tok = pl.BlockSpec((tm, BRANCH_WIDTH), lambda i: (i, 0))
    tr = pl.BlockSpec((tm // TQ, BRANCH_WIDTH, TQ), lambda i: (i, 0, 0))
    nat_shape = jax.ShapeDtypeStruct((n_tok, BRANCH_WIDTH), BF16)
    tr_shape = jax.ShapeDtypeStruct((n_tok // TQ, BRANCH_WIDTH, TQ), BF16)
    return pl.pallas_call(
        functools.partial(_proj_kernel, tiles_per_seq=seq_len // tm),
        grid=(nt,),
        in_specs=[
            pl.BlockSpec((tm, D_MODEL), lambda i: (i, 0)),
            pl.BlockSpec((1, D_MODEL), const),
            pl.BlockSpec((D_MODEL, N_QKV_GROUPS * BRANCH_WIDTH), const),
            pl.BlockSpec((D_MODEL, LANES), const),
            pl.BlockSpec((1, LANES), const),
            pl.BlockSpec((3 * LANES, BRANCH_WIDTH), const),
        ],
        out_specs=[tr, tok, tr, tr, tok, tr, tok, tok, tok, tok],
        out_shape=[tr_shape, nat_shape, tr_shape, tr_shape, nat_shape, tr_shape,
                   nat_shape, nat_shape, nat_shape, nat_shape],
        scratch_shapes=[pltpu.VMEM((8, LANES), F32)],
        compiler_params=pltpu.CompilerParams(
            dimension_semantics=("arbitrary",), vmem_limit_bytes=VMEM_LIMIT),
        name="proj",
    )(x2d, g, w_qkv, w_f, fb, pmat)


def _pair_attn_kernel(qT_ref, qTn_ref, k_ref, kb_ref, vT_ref, lam_ref, g_ref, o_ref,
                      qa_sc, m_sc, l_sc, acc_sc, s_sc, t0_sc, smax_sc, t0max_sc,
                      *, diff, lam_init, nq):
    i = pl.program_id(2)
    row = lax.broadcasted_iota(jnp.int32, (LANES, TQ), 0)

    def build_qa(q_ref, slot):
        qT = q_ref[0, 0].astype(F32)
        qa_sc[slot, :, 0:TQ] = jnp.concatenate(
            [jnp.where(row < HEAD_DIM, qT, 0.0), jnp.where(row < 3, 1.0, 0.0)],
            axis=0).astype(BF16)
        qa_sc[slot, :, TQ:2 * TQ] = jnp.concatenate(
            [jnp.where(row >= HEAD_DIM, qT, 0.0),
             jnp.where((row >= 3) & (row < 6), 1.0, 0.0)], axis=0).astype(BF16)

    tile_keys = KV_SUBS * TK

    def score_rows(row0, n_rows, qa_slot, dst, dst_slot):
        ka = jnp.concatenate([k_ref[0, row0:row0 + n_rows, :],
                              kb_ref[0, row0:row0 + n_rows, :]], axis=1)
        s = jnp.dot(ka, qa_sc[qa_slot], preferred_element_type=F32)
        dst[dst_slot, row0:row0 + n_rows, :] = s
        return jnp.max(s, axis=0, keepdims=True)

    @pl.when(i == 0)
    def _():
        build_qa(qT_ref, 0)
        t0max_sc[0] = score_rows(0, TK, 0, t0_sc, 0)

    def consume(s, sub0, n_sub, tile_max):
        rows = n_sub * TK
        if tile_max is None:
            kr = lax.broadcasted_iota(jnp.int32, (TK, 2 * TQ), 0)
            qc = lax.broadcasted_iota(jnp.int32, (TK, 2 * TQ), 1) % TQ
            tail = jnp.where(kr <= qc, s[rows - TK:], -jnp.inf)
            s = tail if n_sub == 1 else jnp.concatenate([s[:rows - TK], tail], axis=0)
            tile_max = jnp.max(s, axis=0, keepdims=True)
        m_old = m_sc[...]
        m_new = jnp.maximum(m_old, tile_max)
        alpha = jnp.exp2(m_old - m_new)
        p = jnp.exp2(s - m_new)
        l_sc[...] = alpha * l_sc[...] + jnp.sum(p, axis=0, keepdims=True)
        pb = p.astype(BF16)
        pv = jnp.dot(vT_ref[0, sub0], pb[0:TK], preferred_element_type=F32)
        for t in range(1, n_sub):
            pv = pv + jnp.dot(vT_ref[0, sub0 + t], pb[t * TK:(t + 1) * TK],
                              preferred_element_type=F32)
        acc_sc[...] = alpha * acc_sc[...] + pv
        m_sc[...] = m_new

    def full_block(cur, cur_max, cur_slot, sub0, t_next, n_host, par):
        m_old = m_sc[...]
        m_new = jnp.maximum(m_old, cur_max[cur_slot])
        alpha = jnp.exp2(m_old - m_new)
        nslot = t_next % 2
        lsum = pv = nmax = None
        for t in range(KV_SUBS):
            if t < n_host:
                kbase = pl.multiple_of(t_next * tile_keys + t * TK, TK)
                ka = jnp.concatenate([k_ref[0, pl.ds(kbase, TK), :],
                                      kb_ref[0, pl.ds(kbase, TK), :]], axis=1)
                sn = jnp.dot(ka, qa_sc[par], preferred_element_type=F32)
                s_sc[nslot, t * TK:(t + 1) * TK, :] = sn
                cm = jnp.max(sn, axis=0, keepdims=True)
                nmax = cm if nmax is None else jnp.maximum(nmax, cm)
            p = jnp.exp2(cur[cur_slot, t * TK:(t + 1) * TK, :] - m_new)
            ls = jnp.sum(p, axis=0, keepdims=True)
            lsum = ls if lsum is None else lsum + ls
            d = jnp.dot(vT_ref[0, sub0 + t], p.astype(BF16), preferred_element_type=F32)
            pv = d if pv is None else pv + d
        if n_host == KV_SUBS:
            smax_sc[nslot] = nmax
        l_sc[...] = alpha * l_sc[...] + lsum
        acc_sc[...] = alpha * acc_sc[...] + pv
        m_sc[...] = m_new

    def write_output():
        a = acc_sc[...] / l_sc[...]
        a0 = a[:, 0:TQ]
        a1 = a[:, TQ:2 * TQ]
        if diff:
            dl = lam_ref[...]
            lam = (jnp.exp(jnp.sum(dl[0:1] * dl[1:2], axis=1, keepdims=True))
                   - jnp.exp(jnp.sum(dl[2:3] * dl[3:4], axis=1, keepdims=True)) + lam_init)
            oT = a0 - lam * a1
            ms = jnp.mean(oT * oT, axis=0, keepdims=True)
            oT = oT * lax.rsqrt(ms + RMS_EPS)
            o_ref[0] = (oT.T * g_ref[...]) * (1.0 - lam_init)
        else:
            o_ref[0] = jnp.where(row < HEAD_DIM, a0, a1).T

    def diag_block(r, s, sub0, par):
        nslot = (r + 1) % 2
        half = tile_keys // 2
        max_a = score_rows(0, half, 1 - par, t0_sc, nslot)
        consume(s, sub0, r + 1, None)
        max_b = score_rows(half, half, 1 - par, t0_sc, nslot)
        t0max_sc[nslot] = jnp.maximum(max_a, max_b)
        write_output()

    def run_full(j, n_host, par):
        if j == 0:
            full_block(t0_sc, t0max_sc, par, 0, j + 1, n_host, par)
        else:
            full_block(s_sc, smax_sc, j % 2, j * KV_SUBS, j + 1, n_host, par)

    for i_s in range(nq):
        @pl.when(i == i_s)
        def _(i_s=i_s):
            full, r = divmod(i_s, KV_SUBS)
            p_s = i_s % 2
            build_qa(qTn_ref, 1 - p_s)
            m_sc[...] = jnp.full(m_sc.shape, -jnp.inf, F32)
            l_sc[...] = jnp.zeros(l_sc.shape, F32)
            acc_sc[...] = jnp.zeros(acc_sc.shape, F32)
            for j in range(full):
                run_full(j, r + 1 if j + 1 == full else KV_SUBS, p_s)
            rows = (r + 1) * TK
            s = t0_sc[p_s, 0:rows, :] if full == 0 else s_sc[full % 2, 0:rows, :]
            diag_block(r, s, full * KV_SUBS, p_s)


def _pair_attn_call(qT, k, kb, vT, lam_pad, g, *, batch, seq_len, diff, lam_init):
    nq = seq_len // TQ
    nk = seq_len // TK
    qT = qT.reshape(batch, nq, BRANCH_WIDTH, TQ)
    vT = vT.reshape(batch, nk, BRANCH_WIDTH, TK)
    k = k.reshape(batch, seq_len, BRANCH_WIDTH)
    kb_batched = kb.shape[0] != 1
    kb_map = (lambda b, p, i: (b, 0, p)) if kb_batched else (lambda b, p, i: (0, 0, p))
    return pl.pallas_call(
        functools.partial(_pair_attn_kernel, diff=diff, lam_init=lam_init, nq=nq),
        grid=(batch, N_PAIRS, nq),
        in_specs=[
            pl.BlockSpec((1, 1, LANES, TQ), lambda b, p, i: (b, i, p, 0)),
            pl.BlockSpec((1, 1, LANES, TQ),
                         lambda b, p, i: (b, jnp.minimum(i + 1, nq - 1), p, 0)),
            pl.BlockSpec((1, seq_len, LANES), lambda b, p, i: (b, 0, p)),
            pl.BlockSpec((1, seq_len, LANES), kb_map),
            pl.BlockSpec((1, nk, LANES, TK), lambda b, p, i: (b, 0, p, 0)),
            pl.BlockSpec((4, LANES), lambda b, p, i: (0, 0)),
            pl.BlockSpec((1, LANES), lambda b, p, i: (0, p)),
        ],
        out_specs=pl.BlockSpec((1, TQ, LANES), lambda b, p, i: (b, i, p)),
        out_shape=jax.ShapeDtypeStruct((batch, seq_len, BRANCH_WIDTH), F32),
        scratch_shapes=[
            pltpu.VMEM((2, 2 * LANES, 2 * TQ), BF16),
            pltpu.VMEM((1, 2 * TQ), F32),
            pltpu.VMEM((1, 2 * TQ), F32),
            pltpu.VMEM((LANES, 2 * TQ), F32),
            pltpu.VMEM((2, KV_SUBS * TK, 2 * TQ), F32),
            pltpu.VMEM((2, KV_SUBS * TK, 2 * TQ), F32),
            pltpu.VMEM((2, 1, 2 * TQ), F32),
            pltpu.VMEM((2, 1, 2 * TQ), F32),
        ],
        compiler_params=pltpu.CompilerParams(
            dimension_semantics=("arbitrary", "arbitrary", "arbitrary"),
            vmem_limit_bytes=VMEM_LIMIT),
        name="diff_attn" if diff else "fox_attn",
    )(qT, qT, k, kb, vT, lam_pad, g)


def _dil_kernel(q_ref, k_ref, v_ref, kb1_ref, kb4_ref, kb16_ref, o_ref,
                qf, kf, vf, q4f, k4f, v4f, q_sub, k_sub, v_sub, acc, m_sc, l_sc, acc4, m4, l4,
                mask_sc, s_buf, vT_buf, smax_buf, *, seq_len):
    L = DIL_BLOCK
    T = DIL_TQ
    qf[...] = q_ref[0].astype(F32)
    kf[...] = k_ref[0].astype(F32)
    vf[...] = v_ref[0].astype(F32)

    kr = lax.broadcasted_iota(jnp.int32, (2 * L, 2 * L), 0)
    qc = lax.broadcasted_iota(jnp.int32, (2 * L, 2 * L), 1) % L
    band = (kr >= qc) & (kr <= qc + L)
    mask_sc[0] = jnp.where(band & (kr >= L), 0.0, -jnp.inf)
    mask_sc[1] = jnp.where(band, 0.0, -jnp.inf)

    lane = lax.broadcasted_iota(jnp.int32, (T, LANES), 1)
    first = lane < HEAD_DIM
    sel0 = jnp.where(lane < 3, 1.0, 0.0).astype(BF16)
    sel1 = jnp.where((lane >= 3) & (lane < 6), 1.0, 0.0).astype(BF16)
    head0 = jnp.where(first, 1.0, 0.0).astype(BF16)
    head1 = jnp.where(first, 0.0, 1.0).astype(BF16)
    frow = lax.broadcasted_iota(jnp.int32, (LANES, L), 0) < HEAD_DIM
    n_qb = T // L

    assert tuple(d for _, d in DIL_PATTERNS) == (1, 4, 16)
    n4 = seq_len // 4
    n16 = seq_len // 16
    kb_refs = (kb1_ref, kb4_ref, kb16_ref)
    for pattern, (_, d) in enumerate(DIL_PATTERNS):
        nt = seq_len // (T * d)
        n = seq_len // d
        kb_ref = kb_refs[pattern]
        if d == 1:
            q_rows = lambda rows: q_ref[0, rows, :]
            k_rows = lambda rows: k_ref[0, rows, :]
            v_rows = lambda rows: v_ref[0, rows, :]
            m_st, l_st, a_st = m_sc, l_sc, acc
        else:
            if d == 4:
                for r in range(4):
                    dst = slice(r * n4, (r + 1) * n4)
                    for src_f, dst_f, dst_b in ((qf, q4f, q_sub), (kf, k4f, k_sub), (vf, v4f, v_sub)):
                        val = src_f[pl.ds(r, n4, stride=4), :]
                        dst_f[dst, :] = val
                        dst_b[dst, :] = val.astype(BF16)
                    for src_s, dst_s in ((m_sc, m4), (l_sc, l4), (acc, acc4)):
                        dst_s[dst, :] = src_s[pl.ds(r, n4, stride=4), :]
            else:
                for r in range(16):
                    src0 = (r % 4) * n4 + r // 4
                    dst = slice(r * n16, (r + 1) * n16)
                    for src_f, dst_b in ((q4f, q_sub), (k4f, k_sub), (v4f, v_sub)):
                        dst_b[dst, :] = src_f[pl.ds(src0, n16, stride=4), :].astype(BF16)
            q_rows = lambda rows: q_sub[rows, :]
            k_rows = lambda rows: k_sub[rows, :]
            v_rows = lambda rows: v_sub[rows, :]
            m_st, l_st, a_st = m4, l4, acc4

        n_groups = seq_len // (T * DIL_GROUP)

        def addr(u, g, d=d, nt=nt, n=n):
            t = u * DIL_GROUP + g
            r = t // nt
            it = t % nt
            start = pl.multiple_of(r * n + T * it, L)
            prev = pl.multiple_of(jnp.where(it == 0, start, start - L), L)
            if d == 16:
                state_rows = pl.ds((r % 4) * n4 + 4 * T * it + r // 4, T, stride=4)
            else:
                state_rows = pl.ds(start, T)
            return pl.ds(start, T), pl.ds(prev, L), jnp.minimum(it, 1), state_rows

        def score_group(u, buf, q_rows=q_rows, k_rows=k_rows, v_rows=v_rows, kb_ref=kb_ref):
            for g in range(DIL_GROUP):
                rows, prows, slot, _ = addr(u, g)
                qb = q_rows(rows)
                q0 = jnp.concatenate([qb * head0, sel0], axis=1)
                q1 = jnp.concatenate([qb * head1, sel1], axis=1)
                ka = jnp.concatenate(
                    [jnp.concatenate([k_rows(prows), kb_ref[0, prows, :]], axis=1),
                     jnp.concatenate([k_rows(rows), kb_ref[0, rows, :]], axis=1)],
                    axis=0)
                for b in range(n_qb):
                    qa = jnp.concatenate([q0[b * L:(b + 1) * L], q1[b * L:(b + 1) * L]], axis=0)
                    sb = lax.dot_general(ka[b * L:(b + 2) * L], qa, (((1,), (1,)), ((), ())),
                                         preferred_element_type=F32)
                    sb = sb + mask_sc[slot if b == 0 else 1]
                    s_buf[buf, g, b] = sb
                    smax_buf[buf, g, b] = jnp.max(sb, axis=0, keepdims=True)
                vv = jnp.concatenate([v_rows(prows), v_rows(rows)], axis=0)
                vT_buf[buf, g] = vv.astype(F32).T.astype(BF16)

        def consume_group(u, buf, first_pattern=(pattern == 0), m_st=m_st, l_st=l_st, a_st=a_st):
            num, m_blk, l_blk = [], [], []
            for g in range(DIL_GROUP):
                nb_, mb_, lb_ = [], [], []
                for b in range(n_qb):
                    mg = smax_buf[buf, g, b]
                    p = jnp.exp2(s_buf[buf, g, b] - mg)
                    lg = jnp.sum(p, axis=0, keepdims=True)
                    oT = jnp.dot(vT_buf[buf, g, :, b * L:(b + 2) * L], p.astype(BF16),
                                 preferred_element_type=F32)
                    nb_.append(jnp.where(frow, oT[:, :L], oT[:, L:]))
                    mb_.append(jnp.where(frow, jnp.broadcast_to(mg[:, :L], (LANES, L)),
                                         jnp.broadcast_to(mg[:, L:], (LANES, L))))
                    lb_.append(jnp.where(frow, jnp.broadcast_to(lg[:, :L], (LANES, L)),
                                         jnp.broadcast_to(lg[:, L:], (LANES, L))))
                num.append(jnp.concatenate(nb_, axis=1).T)
                m_blk.append(jnp.concatenate(mb_, axis=1).T)
                l_blk.append(jnp.concatenate(lb_, axis=1).T)
            for g in range(DIL_GROUP):
                _, _, _, rows = addr(u, g)
                if first_pattern:
                    l_st[rows, :] = l_blk[g]
                    a_st[rows, :] = num[g]
                    m_st[rows, :] = m_blk[g]
                    continue
                m_old = m_st[rows, :]
                m_new = jnp.maximum(m_old, m_blk[g])
                w_old = jnp.exp2(m_old - m_new)
                w_blk = jnp.exp2(m_blk[g] - m_new)
                l_st[rows, :] = w_old * l_st[rows, :] + w_blk * l_blk[g]
                a_st[rows, :] = w_old * a_st[rows, :] + w_blk * num[g]
                m_st[rows, :] = m_new

        score_group(0, 0)

        def trip(v, carry):
            for c in range(DIL_TRIP):
                u = DIL_TRIP * v + c
                score_group(jnp.minimum(u + 1, n_groups - 1), (c + 1) % 2)
                consume_group(u, c % 2)
            return carry

        lax.fori_loop(0, n_groups // DIL_TRIP, trip, 0)

    for r in range(4):
        rows4 = slice(r * n4, (r + 1) * n4)
        o_ref[0, pl.ds(r, n4, stride=4), :] = acc4[rows4, :] / l4[rows4, :]


def _dil_call(q, k, v, kbs, *, batch, seq_len):
    assert seq_len % (DIL_TQ * max(d for _, d in DIL_PATTERNS)) == 0
    assert seq_len % (DIL_TQ * DIL_GROUP * DIL_TRIP) == 0 and DIL_TRIP % 2 == 0
    q = q.reshape(batch, seq_len, BRANCH_WIDTH)
    k = k.reshape(batch, seq_len, BRANCH_WIDTH)
    v = v.reshape(batch, seq_len, BRANCH_WIDTH)
    blk = pl.BlockSpec((1, seq_len, LANES), lambda b, p: (b, 0, p))
    kb_blk = pl.BlockSpec((1, seq_len, LANES), lambda b, p: (0, 0, p))
    n_qb = DIL_TQ // DIL_BLOCK
    return pl.pallas_call(
        functools.partial(_dil_kernel, seq_len=seq_len),
        grid=(batch, N_PAIRS),
        in_specs=[blk, blk, blk, kb_blk, kb_blk, kb_blk],
        out_specs=blk,
        out_shape=jax.ShapeDtypeStruct((batch, seq_len, BRANCH_WIDTH), F32),
        scratch_shapes=[pltpu.VMEM((seq_len, LANES), F32) for _ in range(6)]
        + [pltpu.VMEM((seq_len, LANES), BF16) for _ in range(3)]
        + [pltpu.VMEM((seq_len, LANES), F32) for _ in range(6)]
        + [pltpu.VMEM((2, 2 * DIL_BLOCK, 2 * DIL_BLOCK), F32),
           pltpu.VMEM((2, DIL_GROUP, n_qb, 2 * DIL_BLOCK, 2 * DIL_BLOCK), F32),
           pltpu.VMEM((2, DIL_GROUP, LANES, DIL_TQ + DIL_BLOCK), BF16),
           pltpu.VMEM((2, DIL_GROUP, n_qb, 1, 2 * DIL_BLOCK), F32)],
        compiler_params=pltpu.CompilerParams(
            dimension_semantics=("arbitrary", "arbitrary"), vmem_limit_bytes=VMEM_LIMIT),
        name="dil_attn",
    )(q, k, v, *kbs)


def _merge_kernel(x_ref, oa_ref, ob_ref, oc_ref, g_ref, wz_ref, wg_ref, wb_ref, wo_ref,
                  fg_ref, out_ref, *, last):
    x = x_ref[...]
    ms = jnp.mean(x * x, axis=-1, keepdims=True)
    h = (x * lax.rsqrt(ms + RMS_EPS) * g_ref[...]).astype(BF16)
    merged = None
    for n, o_ref in enumerate((oa_ref, ob_ref, oc_ref)):
        z = jnp.dot(h, wz_ref[:, n * BRANCH_WIDTH:(n + 1) * BRANCH_WIDTH],
                    preferred_element_type=F32)
        y = (o_ref[...] * (z * jax.nn.sigmoid(z))).astype(BF16)
        br = jnp.dot(y, wb_ref[n], preferred_element_type=F32)
        gate = jax.nn.sigmoid(jnp.dot(h, wg_ref[:, n * D_MODEL:(n + 1) * D_MODEL],
                                      preferred_element_type=F32))
        merged = gate * br if merged is None else merged + gate * br
    out = x + jnp.dot(merged.astype(BF16), wo_ref[...], preferred_element_type=F32)
    if last:
        ms = jnp.mean(out * out, axis=-1, keepdims=True)
        out = out * lax.rsqrt(ms + RMS_EPS) * fg_ref[...]
    out_ref[...] = out


def _merge_call(x2d, oa, ob, oc, g, wz, wg, wb, wo, fg, *, last):
    n_tok = x2d.shape[0]
    tm = TM_MERGE
    const2 = lambda i: (0, 0)
    tok = pl.BlockSpec((tm, BRANCH_WIDTH), lambda i: (i, 0))
    return pl.pallas_call(
        functools.partial(_merge_kernel, last=last),
        grid=(n_tok // tm,),
        in_specs=[
            pl.BlockSpec((tm, D_MODEL), lambda i: (i, 0)), tok, tok, tok,
            pl.BlockSpec((1, D_MODEL), const2),
            pl.BlockSpec((D_MODEL, 3 * BRANCH_WIDTH), const2),
            pl.BlockSpec((D_MODEL, 3 * D_MODEL), const2),
            pl.BlockSpec((3, BRANCH_WIDTH, D_MODEL), lambda i: (0, 0, 0)),
            pl.BlockSpec((D_MODEL, D_MODEL), const2),
            pl.BlockSpec((1, D_MODEL), const2),
        ],
        out_specs=pl.BlockSpec((tm, D_MODEL), lambda i: (i, 0)),
        out_shape=jax.ShapeDtypeStruct((n_tok, D_MODEL), F32),
        compiler_params=pltpu.CompilerParams(
            dimension_semantics=("arbitrary",), vmem_limit_bytes=VMEM_LIMIT),
        name="merge",
    )(x2d, oa.reshape(n_tok, BRANCH_WIDTH), ob.reshape(n_tok, BRANCH_WIDTH),
      oc.reshape(n_tok, BRANCH_WIDTH), g, wz, wg, wb, wo, fg)


def _alibi_slopes(n_heads):
    return [2.0 ** (-ALIBI_MAX_EXP * (h + 1) / n_heads) for h in range(n_heads)]


def _alibi_key_bias(seq_len, slopes_per_pair, dilation=1):
    pos = np.arange(seq_len, dtype=np.float64).reshape(-1, dilation).T.reshape(-1)
    out = np.zeros((seq_len, BRANCH_WIDTH), dtype=BF16)
    for p, pair in enumerate(slopes_per_pair):
        for c, slope in enumerate(pair):
            rest = pos * (slope * LOG2E)
            for x in range(3):
                piece = rest.astype(BF16)
                out[:, p * LANES + 3 * c + x] = piece
                rest = rest - piece.astype(np.float64)
    return jnp.asarray(out)[None]


def _fox_bias_scatter():
    m = [[0.0] * BRANCH_WIDTH for _ in range(3 * LANES)]
    for x in range(3):
        for h in range(N_HEADS):
            m[x * LANES + h][(h // 2) * LANES + 3 * (h % 2) + x] = 1.0
    return jnp.asarray(m, dtype=BF16)


def kernel(x, norm_g, w_in, fox_fb, diff_lam, diff_norm_g, w_branch, w_out, final_g):
    batch, seq_len, _ = x.shape
    depth = norm_g.shape[0]
    n_tok = batch * seq_len
    scale = HEAD_DIM ** -0.5 * LOG2E

    w = BRANCH_WIDTH
    off = {}
    pos = 0
    for name, width in (("diff_q", w), ("diff_k", w), ("diff_v", w), ("diff_z", w),
                        ("fox_q", w), ("fox_k", w), ("fox_v", w), ("fox_f", N_HEADS),
                        ("fox_z", w), ("dil_q", w), ("dil_k", w), ("dil_v", w),
                        ("dil_z", w), ("merge_g", 3 * D_MODEL)):
        off[name] = (pos, pos + width)
        pos += width

    def cols(wl, name, mult=1.0):
        a, b = off[name]
        return wl[:, a:b] * mult if mult != 1.0 else wl[:, a:b]

    ds = _alibi_slopes(DIFF_HEADS)
    cs = _alibi_slopes(N_HEADS)
    diff_kb = _alibi_key_bias(seq_len, [(ds[h], ds[h]) for h in range(DIFF_HEADS)])
    dil_pairs = [(cs[2 * p], cs[2 * p + 1]) for p in range(N_PAIRS)]
    dil_kb = [_alibi_key_bias(seq_len, dil_pairs, d) for _, d in DIL_PATTERNS]
    pmat = _fox_bias_scatter()
    ones_g = jnp.ones((1, BRANCH_WIDTH), F32)
    zero_lam = jnp.zeros((4, LANES), F32)

    x2d = x.reshape(n_tok, D_MODEL)
    for l in range(depth):
        wl = w_in[l]
        w_qkv = jnp.concatenate(
            [cols(wl, "diff_q", scale), cols(wl, "diff_k"), cols(wl, "diff_v"),
             cols(wl, "fox_q", scale), cols(wl, "fox_k"), cols(wl, "fox_v"),
             cols(wl, "dil_q", scale), cols(wl, "dil_k"), cols(wl, "dil_v")],
            axis=1).astype(BF16)
        w_f = jnp.pad(cols(wl, "fox_f"), ((0, 0), (0, LANES - N_HEADS))).astype(BF16)
        fb = jnp.pad(fox_fb[l].astype(F32), (0, LANES - N_HEADS))[None]
        g = norm_g[l].astype(F32)[None]

        (dqT, dk, dvT, fqT, fk, fvT, fkb, cq, ck, cv) = _proj_call(
            x2d, g, w_qkv, w_f, fb, pmat, seq_len=seq_len)

        lam_init = 0.8 - 0.6 * math.exp(-0.3 * l)
        lam_pad = jnp.pad(diff_lam[l].astype(F32), ((0, 0), (0, LANES - HEAD_DIM)))
        oa = _pair_attn_call(dqT, dk, diff_kb, dvT, lam_pad,
                             diff_norm_g[l].astype(F32)[None],
                             batch=batch, seq_len=seq_len, diff=True, lam_init=lam_init)
        ob = _pair_attn_call(fqT, fk, fkb.reshape(batch, seq_len, BRANCH_WIDTH), fvT,
                             zero_lam, ones_g,
                             batch=batch, seq_len=seq_len, diff=False, lam_init=0.0)
        oc = _dil_call(cq, ck, cv, dil_kb, batch=batch, seq_len=seq_len)

        wz = jnp.concatenate([cols(wl, "diff_z"), cols(wl, "fox_z"), cols(wl, "dil_z")],
                             axis=1).astype(BF16)
        wg = cols(wl, "merge_g").astype(BF16)
        x2d = _merge_call(x2d, oa, ob, oc, g, wz, wg, w_branch[l].astype(BF16),
                          w_out[l].astype(BF16), final_g.astype(F32)[None],
                          last=(l == depth - 1))
    return x2d.reshape(batch, seq_len, D_MODEL)
```

```python
import functools
import math

import jax
import jax.numpy as jnp
import numpy as np
from jax import lax
from jax.experimental import pallas as pl
from jax.experimental.pallas import tpu as pltpu

F32 = jnp.float32
BF16 = jnp.bfloat16

D_MODEL = 1024
HEAD_DIM = 64
BRANCH_WIDTH = 512
LANES = 128
N_PAIRS = BRANCH_WIDTH // LANES
DIFF_HEADS = 4
N_HEADS = 8
DIL_PATTERNS = ((128, 1), (512, 4), (2048, 16))
DIL_BLOCK = 128
DIL_TQ = 256
DIL_GROUP = 2
DIL_TRIP = 4
RMS_EPS = 1e-6
LOG2E = math.log2(math.e)
ALIBI_MAX_EXP = 8.0
N_QKV_GROUPS = 9
TQ = 512
TK = 512
KV_SUBS = 2
TM_PROJ = 512
TM_MERGE = 512
VMEM_LIMIT = 56 * 1024 * 1024


def _split3(a):
    a1 = a.astype(BF16)
    r1 = a - a1.astype(F32)
    a2 = r1.astype(BF16)
    a3 = (r1 - a2.astype(F32)).astype(BF16)
    return a1, a2, a3


def _proj_kernel(x_ref, g_ref, w_ref, wf_ref, fb_ref, pmat_ref,
                 dqT_ref, dk_ref, dvT_ref, fqT_ref, fk_ref, fvT_ref, fkb_ref,
                 cq_ref, ck_ref, cv_ref, carry_ref, *, tiles_per_seq):
    i = pl.program_id(0)
    tm = x_ref.shape[0]

    @pl.when(i % tiles_per_seq == 0)
    def _():
        carry_ref[...] = jnp.zeros_like(carry_ref)

    x = x_ref[...]
    ms = jnp.mean(x * x, axis=-1, keepdims=True)
    h = (x * lax.rsqrt(ms + RMS_EPS) * g_ref[...]).astype(BF16)

    def proj(g):
        return jnp.dot(h, w_ref[:, g * BRANCH_WIDTH:(g + 1) * BRANCH_WIDTH],
                       preferred_element_type=F32)

    def put_t(ref, val):
        for t in range(tm // TQ):
            ref[t] = val[t * TQ:(t + 1) * TQ, :].T.astype(BF16)

    f = jnp.dot(h, wf_ref[...], preferred_element_type=F32) + fb_ref[...]
    put_t(dqT_ref, proj(0))
    logf = jnp.minimum(f, 0.0) - jnp.log1p(jnp.exp(-jnp.abs(f)))
    row = lax.broadcasted_iota(jnp.int32, (tm, tm), 0)
    col = lax.broadcasted_iota(jnp.int32, (tm, tm), 1)
    tri = jnp.where(row >= col, 1.0, 0.0).astype(BF16)
    a1, a2, a3 = _split3(logf)
    c = (jnp.dot(tri, a1, preferred_element_type=F32)
         + jnp.dot(tri, a2, preferred_element_type=F32)
         + jnp.dot(tri, a3, preferred_element_type=F32)) + carry_ref[0:1, :]
    carry_ref[0:1, :] = c[tm - 1:tm, :]
    dk_ref[...] = proj(1).astype(BF16)
    put_t(dvT_ref, proj(2))
    b1, b2, b3 = _split3(c * (-LOG2E))
    kb = jnp.dot(jnp.concatenate([b1, b2, b3], axis=1), pmat_ref[...],
                 preferred_element_type=F32)
    fkb_ref[...] = kb.astype(BF16)
    put_t(fqT_ref, proj(3))
    fk_ref[...] = proj(4).astype(BF16)
    put_t(fvT_ref, proj(5))
    cq_ref[...] = proj(6).astype(BF16)
    ck_ref[...] = proj(7).astype(BF16)
    cv_ref[...] = proj(8).astype(BF16)


def _proj_call(x2d, g, w_qkv, w_f, fb, pmat, *, seq_len):
    n_tok = x2d.shape[0]
    tm = TM_PROJ
    nt = n_tok // tm
    const = lambda i: (0, 0)
    tok = pl.BlockSpec((tm, BRANCH_WIDTH), lambda i: (i, 0))
    tr = pl.BlockSpec((tm // TQ, BRANCH_WIDTH, TQ), lambda i: (i, 0, 0))
    nat_shape = jax.ShapeDtypeStruct((n_tok, BRANCH_WIDTH), BF16)
    tr_shape = jax.ShapeDtypeStruct((n_tok // TQ, BRANCH_WIDTH, TQ), BF16)
    return pl.pallas_call(
        functools.partial(_proj_kernel, tiles_per_seq=seq_len // tm),
        grid=(nt,),
        in_specs=[
            pl.BlockSpec((tm, D_MODEL), lambda i: (i, 0)),
            pl.BlockSpec((1, D_MODEL), const),
            pl.BlockSpec((D_MODEL, N_QKV_GROUPS * BRANCH_WIDTH), const),
            pl.BlockSpec((D_MODEL, LANES), const),
            pl.BlockSpec((1, LANES), const),
            pl.BlockSpec((3 * LANES, BRANCH_WIDTH), const),
        ],
        out_specs=[tr, tok, tr, tr, tok, tr, tok, tok, tok, tok],
        out_shape=[tr_shape, nat_shape, tr_shape, tr_shape, nat_shape, tr_shape,
                   nat_shape, nat_shape, nat_shape, nat_shape],
        scratch_shapes=[pltpu.VMEM((8, LANES), F32)],
        compiler_params=pltpu.CompilerParams(
            dimension_semantics=("arbitrary",), vmem_limit_bytes=VMEM_LIMIT),
        name="proj",
    )(x2d, g, w_qkv, w_f, fb, pmat)


def _pair_attn_kernel(qT_ref, qTn_ref, k_ref, kb_ref, vT_ref, lam_ref, g_ref, o_ref,
                      qa_sc, m_sc, l_sc, acc_sc, s_sc, t0_sc, smax_sc, t0max_sc,
                      *, diff, lam_init, nq):
    i = pl.program_id(2)
    row = lax.broadcasted_iota(jnp.int32, (LANES, TQ), 0)

    def build_qa(q_ref, slot):
        qT = q_ref[0, 0].astype(F32)
        qa_sc[slot, :, 0:TQ] = jnp.concatenate(
            [jnp.where(row < HEAD_DIM, qT, 0.0), jnp.where(row < 3, 1.0, 0.0)],
            axis=0).astype(BF16)
        qa_sc[slot, :, TQ:2 * TQ] = jnp.concatenate(
            [jnp.where(row >= HEAD_DIM, qT, 0.0),
             jnp.where((row >= 3) & (row < 6), 1.0, 0.0)], axis=0).astype(BF16)

    tile_keys = KV_SUBS * TK

    def score_rows(row0, n_rows, qa_slot, dst, dst_slot):
        ka = jnp.concatenate([k_ref[0, row0:row0 + n_rows, :],
                              kb_ref[0, row0:row0 + n_rows, :]], axis=1)
        s = jnp.dot(ka, qa_sc[qa_slot], preferred_element_type=F32)
        dst[dst_slot, row0:row0 + n_rows, :] = s
        return jnp.max(s, axis=0, keepdims=True)

    @pl.when(i == 0)
    def _():
        build_qa(qT_ref, 0)
        t0max_sc[0] = score_rows(0, TK, 0, t0_sc, 0)

    def consume(s, sub0, n_sub, tile_max):
        rows = n_sub * TK
        if tile_max is None:
            kr = lax.broadcasted_iota(jnp.int32, (TK, 2 * TQ), 0)
            qc = lax.broadcasted_iota(jnp.int32, (TK, 2 * TQ), 1) % TQ
            tail = jnp.where(kr <= qc, s[rows - TK:], -jnp.inf)
            s = tail if n_sub == 1 else jnp.concatenate([s[:rows - TK], tail], axis=0)
            tile_max = jnp.max(s, axis=0, keepdims=True)
        m_old = m_sc[...]
        m_new = jnp.maximum(m_old, tile_max)
        alpha = jnp.exp2(m_old - m_new)
        p = jnp.exp2(s - m_new)
        l_sc[...] = alpha * l_sc[...] + jnp.sum(p, axis=0, keepdims=True)
        pb = p.astype(BF16)
        pv = jnp.dot(vT_ref[0, sub0], pb[0:TK], preferred_element_type=F32)
        for t in range(1, n_sub):
            pv = pv + jnp.dot(vT_ref[0, sub0 + t], pb[t * TK:(t + 1) * TK],
                              preferred_element_type=F32)
        acc_sc[...] = alpha * acc_sc[...] + pv
        m_sc[...] = m_new

    def full_block(cur, cur_max, cur_slot, sub0, t_next, n_host, par):
        m_old = m_sc[...]
        m_new = jnp.maximum(m_old, cur_max[cur_slot])
        alpha = jnp.exp2(m_old - m_new)
        nslot = t_next % 2
        lsum = pv = nmax = None
        for t in range(KV_SUBS):
            if t < n_host:
                kbase = pl.multiple_of(t_next * tile_keys + t * TK, TK)
                ka = jnp.concatenate([k_ref[0, pl.ds(kbase, TK), :],
                                      kb_ref[0, pl.ds(kbase, TK), :]], axis=1)
                sn = jnp.dot(ka, qa_sc[par], preferred_element_type=F32)
                s_sc[nslot, t * TK:(t + 1) * TK, :] = sn
                cm = jnp.max(sn, axis=0, keepdims=True)
                nmax = cm if nmax is None else jnp.maximum(nmax, cm)
            p = jnp.exp2(cur[cur_slot, t * TK:(t + 1) * TK, :] - m_new)
            ls = jnp.sum(p, axis=0, keepdims=True)
            lsum = ls if lsum is None else lsum + ls
            d = jnp.dot(vT_ref[0, sub0 + t], p.astype(BF16), preferred_element_type=F32)
            pv = d if pv is None else pv + d
        if n_host == KV_SUBS:
            smax_sc[nslot] = nmax
        l_sc[...] = alpha * l_sc[...] + lsum
        acc_sc[...] = alpha * acc_sc[...] + pv
        m_sc[...] = m_new

    def write_output():
        a = acc_sc[...] / l_sc[...]
        a0 = a[:, 0:TQ]
        a1 = a[:, TQ:2 * TQ]
        if diff:
            dl = lam_ref[...]
            lam = (jnp.exp(jnp.sum(dl[0:1] * dl[1:2], axis=1, keepdims=True))
                   - jnp.exp(jnp.sum(dl[2:3] * dl[3:4], axis=1, keepdims=True)) + lam_init)
            oT = a0 - lam * a1
            ms = jnp.mean(oT * oT, axis=0, keepdims=True)
            oT = oT * lax.rsqrt(ms + RMS_EPS)
            o_ref[0] = (oT.T * g_ref[...]) * (1.0 - lam_init)
        else:
            o_ref[0] = jnp.where(row < HEAD_DIM, a0, a1).T

    def diag_block(r, s, sub0, par):
        nslot = (r + 1) % 2
        half = tile_keys // 2
        max_a = score_rows(0, half, 1 - par, t0_sc, nslot)
        consume(s, sub0, r + 1, None)
        max_b = score_rows(half, half, 1 - par, t0_sc, nslot)
        t0max_sc[nslot] = jnp.maximum(max_a, max_b)
        write_output()

    def run_full(j, n_host, par):
        if j == 0:
            full_block(t0_sc, t0max_sc, par, 0, j + 1, n_host, par)
        else:
            full_block(s_sc, smax_sc, j % 2, j * KV_SUBS, j + 1, n_host, par)

    for i_s in range(nq):
        @pl.when(i == i_s)
        def _(i_s=i_s):
            full, r = divmod(i_s, KV_SUBS)
            p_s = i_s % 2
            build_qa(qTn_ref, 1 - p_s)
            m_sc[...] = jnp.full(m_sc.shape, -jnp.inf, F32)
            l_sc[...] = jnp.zeros(l_sc.shape, F32)
            acc_sc[...] = jnp.zeros(acc_sc.shape, F32)
            for j in range(full):
                run_full(j, r + 1 if j + 1 == full else KV_SUBS, p_s)
            rows = (r + 1) * TK
            s = t0_sc[p_s, 0:rows, :] if full == 0 else s_sc[full % 2, 0:rows, :]
            diag_block(r, s, full * KV_SUBS, p_s)


def _pair_attn_call(qT, k, kb, vT, lam_pad, g, *, batch, seq_len, diff, lam_init):
    nq = seq_len // TQ
    nk = seq_len // TK
    qT = qT.reshape(batch, nq, BRANCH_WIDTH, TQ)
    vT = vT.reshape(batch, nk, BRANCH_WIDTH, TK)
    k = k.reshape(batch, seq_len, BRANCH_WIDTH)
    kb_batched = kb.shape[0] != 1
    kb_map = (lambda b, p, i: (b, 0, p)) if kb_batched else (lambda b, p, i: (0, 0, p))
    return pl.pallas_call(
        functools.partial(_pair_attn_kernel, diff=diff, lam_init=lam_init, nq=nq),
        grid=(batch, N_PAIRS, nq),
        in_specs=[
            pl.BlockSpec((1, 1, LANES, TQ), lambda b, p, i: (b, i, p, 0)),
            pl.BlockSpec((1, 1, LANES, TQ),
                         lambda b, p, i: (b, jnp.minimum(i + 1, nq - 1), p, 0)),
            pl.BlockSpec((1, seq_len, LANES), lambda b, p, i: (b, 0, p)),
            pl.BlockSpec((1, seq_len, LANES), kb_map),
            pl.BlockSpec((1, nk, LANES, TK), lambda b, p, i: (b, 0, p, 0)),
            pl.BlockSpec((4, LANES), lambda b, p, i: (0, 0)),
            pl.BlockSpec((1, LANES), lambda b, p, i: (0, p)),
        ],
        out_specs=pl.BlockSpec((1, TQ, LANES), lambda b, p, i: (b, i, p)),
        out_shape=jax.ShapeDtypeStruct((batch, seq_len, BRANCH_WIDTH), F32),
        scratch_shapes=[
            pltpu.VMEM((2, 2 * LANES, 2 * TQ), BF16),
            pltpu.VMEM((1, 2 * TQ), F32),
            pltpu.VMEM((1, 2 * TQ), F32),
            pltpu.VMEM((LANES, 2 * TQ), F32),
            pltpu.VMEM((2, KV_SUBS * TK, 2 * TQ), F32),
            pltpu.VMEM((2, KV_SUBS * TK, 2 * TQ), F32),
            pltpu.VMEM((2, 1, 2 * TQ), F32),
            pltpu.VMEM((2, 1, 2 * TQ), F32),
        ],
        compiler_params=pltpu.CompilerParams(
            dimension_semantics=("arbitrary", "arbitrary", "arbitrary"),
            vmem_limit_bytes=VMEM_LIMIT),
        name="diff_attn" if diff else "fox_attn",
    )(qT, qT, k, kb, vT, lam_pad, g)


def _dil_kernel(q_ref, k_ref, v_ref, kb1_ref, kb4_ref, kb16_ref, o_ref,
                qf, kf, vf, q4f, k4f, v4f, q_sub, k_sub, v_sub, acc, m_sc, l_sc, acc4, m4, l4,
                mask_sc, s_buf, vT_buf, smax_buf, *, seq_len):
    L = DIL_BLOCK
    T = DIL_TQ
    qf[...] = q_ref[0].astype(F32)
    kf[...] = k_ref[0].astype(F32)
    vf[...] = v_ref[0].astype(F32)

    kr = lax.broadcasted_iota(jnp.int32, (2 * L, 2 * L), 0)
    qc = lax.broadcasted_iota(jnp.int32, (2 * L, 2 * L), 1) % L
    band = (kr >= qc) & (kr <= qc + L)
    mask_sc[0] = jnp.where(band & (kr >= L), 0.0, -jnp.inf)
    mask_sc[1] = jnp.where(band, 0.0, -jnp.inf)

    lane = lax.broadcasted_iota(jnp.int32, (T, LANES), 1)
    first = lane < HEAD_DIM
    sel0 = jnp.where(lane < 3, 1.0, 0.0).astype(BF16)
    sel1 = jnp.where((lane >= 3) & (lane < 6), 1.0, 0.0).astype(BF16)
    head0 = jnp.where(first, 1.0, 0.0).astype(BF16)
    head1 = jnp.where(first, 0.0, 1.0).astype(BF16)
    frow = lax.broadcasted_iota(jnp.int32, (LANES, L), 0) < HEAD_DIM
    n_qb = T // L

    assert tuple(d for _, d in DIL_PATTERNS) == (1, 4, 16)
    n4 = seq_len // 4
    n16 = seq_len // 16
    kb_refs = (kb1_ref, kb4_ref, kb16_ref)
    for pattern, (_, d) in enumerate(DIL_PATTERNS):
        nt = seq_len // (T * d)
        n = seq_len // d
        kb_ref = kb_refs[pattern]
        if d == 1:
            q_rows = lambda rows: q_ref[0, rows, :]
            k_rows = lambda rows: k_ref[0, rows, :]
            v_rows = lambda rows: v_ref[0, rows, :]
            m_st, l_st, a_st = m_sc, l_sc, acc
        else:
            if d == 4:
                for r in range(4):
                    dst = slice(r * n4, (r + 1) * n4)
                    for src_f, dst_f, dst_b in ((qf, q4f, q_sub), (kf, k4f, k_sub), (vf, v4f, v_sub)):
                        val = src_f[pl.ds(r, n4, stride=4), :]
                        dst_f[dst, :] = val
                        dst_b[dst, :] = val.astype(BF16)
                    for src_s, dst_s in ((m_sc, m4), (l_sc, l4), (acc, acc4)):
                        dst_s[dst, :] = src_s[pl.ds(r, n4, stride=4), :]
            else:
                for r in range(16):
                    src0 = (r % 4) * n4 + r // 4
                    dst = slice(r * n16, (r + 1) * n16)
                    for src_f, dst_b in ((q4f, q_sub), (k4f, k_sub), (v4f, v_sub)):
                        dst_b[dst, :] = src_f[pl.ds(src0, n16, stride=4), :].astype(BF16)
            q_rows = lambda rows: q_sub[rows, :]
            k_rows = lambda rows: k_sub[rows, :]
            v_rows = lambda rows: v_sub[rows, :]
            m_st, l_st, a_st = m4, l4, acc4

        n_groups = seq_len // (T * DIL_GROUP)

        def addr(u, g, d=d, nt=nt, n=n):
            t = u * DIL_GROUP + g
            r = t // nt
            it = t % nt
            start = pl.multiple_of(r * n + T * it, L)
            prev = pl.multiple_of(jnp.where(it == 0, start, start - L), L)
            if d == 16:
                state_rows = pl.ds((r % 4) * n4 + 4 * T * it + r // 4, T, stride=4)
            else:
                state_rows = pl.ds(start, T)
            return pl.ds(start, T), pl.ds(prev, L), jnp.minimum(it, 1), state_rows

        def score_group(u, buf, q_rows=q_rows, k_rows=k_rows, v_rows=v_rows, kb_ref=kb_ref):
            for g in range(DIL_GROUP):
                rows, prows, slot, _ = addr(u, g)
                qb = q_rows(rows)
                q0 = jnp.concatenate([qb * head0, sel0], axis=1)
                q1 = jnp.concatenate([qb * head1, sel1], axis=1)
                ka = jnp.concatenate(
                    [jnp.concatenate([k_rows(prows), kb_ref[0, prows, :]], axis=1),
                     jnp.concatenate([k_rows(rows), kb_ref[0, rows, :]], axis=1)],
                    axis=0)
                for b in range(n_qb):
                    qa = jnp.concatenate([q0[b * L:(b + 1) * L], q1[b * L:(b + 1) * L]], axis=0)
                    sb = lax.dot_general(ka[b * L:(b + 2) * L], qa, (((1,), (1,)), ((), ())),
                                         preferred_element_type=F32)
                    sb = sb + mask_sc[slot if b == 0 else 1]
                    s_buf[buf, g, b] = sb
                    smax_buf[buf, g, b] = jnp.max(sb, axis=0, keepdims=True)
                vv = jnp.concatenate([v_rows(prows), v_rows(rows)], axis=0)
                vT_buf[buf, g] = vv.astype(F32).T.astype(BF16)

        def consume_group(u, buf, first_pattern=(pattern == 0), m_st=m_st, l_st=l_st, a_st=a_st):
            num, m_blk, l_blk = [], [], []
            for g in range(DIL_GROUP):
                nb_, mb_, lb_ = [], [], []
                for b in range(n_qb):
                    mg = smax_buf[buf, g, b]
                    p = jnp.exp2(s_buf[buf, g, b] - mg)
                    lg = jnp.sum(p, axis=0, keepdims=True)
                    oT = jnp.dot(vT_buf[buf, g, :, b * L:(b + 2) * L], p.astype(BF16),
                                 preferred_element_type=F32)
                    nb_.append(jnp.where(frow, oT[:, :L], oT[:, L:]))
                    mb_.append(jnp.where(frow, jnp.broadcast_to(mg[:, :L], (LANES, L)),
                                         jnp.broadcast_to(mg[:, L:], (LANES, L))))
                    lb_.append(jnp.where(frow, jnp.broadcast_to(lg[:, :L], (LANES, L)),
                                         jnp.broadcast_to(lg[:, L:], (LANES, L))))
                num.append(jnp.concatenate(nb_, axis=1).T)
                m_blk.append(jnp.concatenate(mb_, axis=1).T)
                l_blk.append(jnp.concatenate(lb_, axis=1).T)
            for g in range(DIL_GROUP):
                _, _, _, rows = addr(u, g)
                if first_pattern:
                    l_st[rows, :] = l_blk[g]
                    a_st[rows, :] = num[g]
                    m_st[rows, :] = m_blk[g]
                    continue
                m_old = m_st[rows, :]
                m_new = jnp.maximum(m_old, m_blk[g])
                w_old = jnp.exp2(m_old - m_new)
                w_blk = jnp.exp2(m_blk[g] - m_new)
                l_st[rows, :] = w_old * l_st[rows, :] + w_blk * l_blk[g]
                a_st[rows, :] = w_old * a_st[rows, :] + w_blk * num[g]
                m_st[rows, :] = m_new

        score_group(0, 0)

        def trip(v, carry):
            for c in range(DIL_TRIP):
                u = DIL_TRIP * v + c
                score_group(jnp.minimum(u + 1, n_groups - 1), (c + 1) % 2)
                consume_group(u, c % 2)
            return carry

        lax.fori_loop(0, n_groups // DIL_TRIP, trip, 0)

    for r in range(4):
        rows4 = slice(r * n4, (r + 1) * n4)
        o_ref[0, pl.ds(r, n4, stride=4), :] = acc4[rows4, :] / l4[rows4, :]


def _dil_call(q, k, v, kbs, *, batch, seq_len):
    assert seq_len % (DIL_TQ * max(d for _, d in DIL_PATTERNS)) == 0
    assert seq_len % (DIL_TQ * DIL_GROUP * DIL_TRIP) == 0 and DIL_TRIP % 2 == 0
    q = q.reshape(batch, seq_len, BRANCH_WIDTH)
    k = k.reshape(batch, seq_len, BRANCH_WIDTH)
    v = v.reshape(batch, seq_len, BRANCH_WIDTH)
    blk = pl.BlockSpec((1, seq_len, LANES), lambda b, p: (b, 0, p))
    kb_blk = pl.BlockSpec((1, seq_len, LANES), lambda b, p: (0, 0, p))
    n_qb = DIL_TQ // DIL_BLOCK
    return pl.pallas_call(
        functools.partial(_dil_kernel, seq_len=seq_len),
        grid=(batch, N_PAIRS),
        in_specs=[blk, blk, blk, kb_blk, kb_blk, kb_blk],
        out_specs=blk,
        out_shape=jax.ShapeDtypeStruct((batch, seq_len, BRANCH_WIDTH), F32),
        scratch_shapes=[pltpu.VMEM((seq_len, LANES), F32) for _ in range(6)]
        + [pltpu.VMEM((seq_len, LANES), BF16) for _ in range(3)]
        + [pltpu.VMEM((seq_len, LANES), F32) for _ in range(6)]
        + [pltpu.VMEM((2, 2 * DIL_BLOCK, 2 * DIL_BLOCK), F32),
           pltpu.VMEM((2, DIL_GROUP, n_qb, 2 * DIL_BLOCK, 2 * DIL_BLOCK), F32),
           pltpu.VMEM((2, DIL_GROUP, LANES, DIL_TQ + DIL_BLOCK), BF16),
           pltpu.VMEM((2, DIL_GROUP, n_qb, 1, 2 * DIL_BLOCK), F32)],
        compiler_params=pltpu.CompilerParams(
            dimension_semantics=("arbitrary", "arbitrary"), vmem_limit_bytes=VMEM_LIMIT),
        name="dil_attn",
    )(q, k, v, *kbs)


def _merge_kernel(x_ref, oa_ref, ob_ref, oc_ref, g_ref, wz_ref, wg_ref, wb_ref, wo_ref,
                  fg_ref, out_ref, *, last):
    x = x_ref[...]
    ms = jnp.mean(x * x, axis=-1, keepdims=True)
    h = (x * lax.rsqrt(ms + RMS_EPS) * g_ref[...]).astype(BF16)
    merged = None
    for n, o_ref in enumerate((oa_ref, ob_ref, oc_ref)):
        z = jnp.dot(h, wz_ref[:, n * BRANCH_WIDTH:(n + 1) * BRANCH_WIDTH],
                    preferred_element_type=F32)
        gate = jax.nn.sigmoid(jnp.dot(h, wg_ref[:, n * D_MODEL:(n + 1) * D_MODEL],
                                      preferred_element_type=F32))
        y = (o_ref[...] * (z * jax.nn.sigmoid(z))).astype(BF16)
        br = jnp.dot(y, wb_ref[n], preferred_element_type=F32)
        merged = gate * br if merged is None else merged + gate * br
    out = x + jnp.dot(merged.astype(BF16), wo_ref[...], preferred_element_type=F32)
    if last:
        ms = jnp.mean(out * out, axis=-1, keepdims=True)
        out = out * lax.rsqrt(ms + RMS_EPS) * fg_ref[...]
    out_ref[...] = out


def _merge_call(x2d, oa, ob, oc, g, wz, wg, wb, wo, fg, *, last):
    n_tok = x2d.shape[0]
    tm = TM_MERGE
    const2 = lambda i: (0, 0)
    tok = pl.BlockSpec((tm, BRANCH_WIDTH), lambda i: (i, 0))
    return pl.pallas_call(
        functools.partial(_merge_kernel, last=last),
        grid=(n_tok // tm,),
        in_specs=[
            pl.BlockSpec((tm, D_MODEL), lambda i: (i, 0)), tok, tok, tok,
            pl.BlockSpec((1, D_MODEL), const2),
            pl.BlockSpec((D_MODEL, 3 * BRANCH_WIDTH), const2),
            pl.BlockSpec((D_MODEL, 3 * D_MODEL), const2),
            pl.BlockSpec((3, BRANCH_WIDTH, D_MODEL), lambda i: (0, 0, 0)),
            pl.BlockSpec((D_MODEL, D_MODEL), const2),
            pl.BlockSpec((1, D_MODEL), const2),
        ],
        out_specs=pl.BlockSpec((tm, D_MODEL), lambda i: (i, 0)),
        out_shape=jax.ShapeDtypeStruct((n_tok, D_MODEL), F32),
        compiler_params=pltpu.CompilerParams(
            dimension_semantics=("arbitrary",), vmem_limit_bytes=VMEM_LIMIT),
        name="merge",
    )(x2d, oa.reshape(n_tok, BRANCH_WIDTH), ob.reshape(n_tok, BRANCH_WIDTH),
      oc.reshape(n_tok, BRANCH_WIDTH), g, wz, wg, wb, wo, fg)


def _alibi_slopes(n_heads):
    return [2.0 ** (-ALIBI_MAX_EXP * (h + 1) / n_heads) for h in range(n_heads)]


def _alibi_key_bias(seq_len, slopes_per_pair, dilation=1):
    pos = np.arange(seq_len, dtype=np.float64).reshape(-1, dilation).T.reshape(-1)
    out = np.zeros((seq_len, BRANCH_WIDTH), dtype=BF16)
    for p, pair in enumerate(slopes_per_pair):
        for c, slope in enumerate(pair):
            rest = pos * (slope * LOG2E)
            for x in range(3):
                piece = rest.astype(BF16)
                out[:, p * LANES + 3 * c + x] = piece
                rest = rest - piece.astype(np.float64)
    return jnp.asarray(out)[None]


def _fox_bias_scatter():
    m = [[0.0] * BRANCH_WIDTH for _ in range(3 * LANES)]
    for x in range(3):
        for h in range(N_HEADS):
            m[x * LANES + h][(h // 2) * LANES + 3 * (h % 2) + x] = 1.0
    return jnp.asarray(m, dtype=BF16)


def kernel(x, norm_g, w_in, fox_fb, diff_lam, diff_norm_g, w_branch, w_out, final_g):
    batch, seq_len, _ = x.shape
    depth = norm_g.shape[0]
    n_tok = batch * seq_len
    scale = HEAD_DIM ** -0.5 * LOG2E

    w = BRANCH_WIDTH
    off = {}
    pos = 0
    for name, width in (("diff_q", w), ("diff_k", w), ("diff_v", w), ("diff_z", w),
                        ("fox_q", w), ("fox_k", w), ("fox_v", w), ("fox_f", N_HEADS),
                        ("fox_z", w), ("dil_q", w), ("dil_k", w), ("dil_v", w),
                        ("dil_z", w), ("merge_g", 3 * D_MODEL)):
        off[name] = (pos, pos + width)
        pos += width

    def cols(wl, name, mult=1.0):
        a, b = off[name]
        return wl[:, a:b] * mult if mult != 1.0 else wl[:, a:b]

    ds = _alibi_slopes(DIFF_HEADS)
    cs = _alibi_slopes(N_HEADS)
    diff_kb = _alibi_key_bias(seq_len, [(ds[h], ds[h]) for h in range(DIFF_HEADS)])
    dil_pairs = [(cs[2 * p], cs[2 * p + 1]) for p in range(N_PAIRS)]
    dil_kb = [_alibi_key_bias(seq_len, dil_pairs, d) for _, d in DIL_PATTERNS]
    pmat = _fox_bias_scatter()
    ones_g = jnp.ones((1, BRANCH_WIDTH), F32)
    zero_lam = jnp.zeros((4, LANES), F32)

    x2d = x.reshape(n_tok, D_MODEL)
    for l in range(depth):
        wl = w_in[l]
        w_qkv = jnp.concatenate(
            [cols(wl, "diff_q", scale), cols(wl, "diff_k"), cols(wl, "diff_v"),
             cols(wl, "fox_q", scale), cols(wl, "fox_k"), cols(wl, "fox_v"),
             cols(wl, "dil_q", scale), cols(wl, "dil_k"), cols(wl, "dil_v")],
            axis=1).astype(BF16)
        w_f = jnp.pad(cols(wl, "fox_f"), ((0, 0), (0, LANES - N_HEADS))).astype(BF16)
        fb = jnp.pad(fox_fb[l].astype(F32), (0, LANES - N_HEADS))[None]
        g = norm_g[l].astype(F32)[None]

        (dqT, dk, dvT, fqT, fk, fvT, fkb, cq, ck, cv) = _proj_call(
            x2d, g, w_qkv, w_f, fb, pmat, seq_len=seq_len)

        lam_init = 0.8 - 0.6 * math.exp(-0.3 * l)
        lam_pad = jnp.pad(diff_lam[l].astype(F32), ((0, 0), (0, LANES - HEAD_DIM)))
        oa = _pair_attn_call(dqT, dk, diff_kb, dvT, lam_pad,
                             diff_norm_g[l].astype(F32)[None],
                             batch=batch, seq_len=seq_len, diff=True, lam_init=lam_init)
        ob = _pair_attn_call(fqT, fk, fkb.reshape(batch, seq_len, BRANCH_WIDTH), fvT,
                             zero_lam, ones_g,
                             batch=batch, seq_len=seq_len, diff=False, lam_init=0.0)
        oc = _dil_call(cq, ck, cv, dil_kb, batch=batch, seq_len=seq_len)

        wz = jnp.concatenate([cols(wl, "diff_z"), cols(wl, "fox_z"), cols(wl, "dil_z")],
                             axis=1).astype(BF16)
        wg = cols(wl, "merge_g").astype(BF16)
        x2d = _merge_call(x2d, oa, ob, oc, g, wz, wg, w_branch[l].astype(BF16),
                          w_out[l].astype(BF16), final_g.astype(F32)[None],
                          last=(l == depth - 1))
    return x2d.reshape(batch, seq_len, D_MODEL)
```
